```python
import math
import jax
import jax.numpy as jnp
from jax import lax
import numpy as np

D_MODEL = 2048
BATCH = 4
SEQ = 2048
DEPTH = 4
DEC_BATCH = 32
DEC_SEQ = 4
PAST_LEN = 16384
PAGE_SIZE = 128

N_MIXERS = 4
N_CONV = (DEPTH + 3) // N_MIXERS
N_HGRN = (DEPTH + 2) // N_MIXERS
N_SSM = (DEPTH + 1) // N_MIXERS
N_SWA = DEPTH // N_MIXERS

CONV_WIDTH = 31

HGRN_DK = 128
HGRN_HEADS = D_MODEL // HGRN_DK
HGRN_DV = D_MODEL // HGRN_HEADS
HGRN_CHUNK = 64

SSM_GROUP = 16
SSM_GROUPS = D_MODEL // SSM_GROUP
SSM_STATE = 64
DT_MIN = 0.001
DT_MAX = 0.1

SWA_HEAD_DIM = 64
SWA_HEADS = D_MODEL // SWA_HEAD_DIM
SWA_KV_HEADS = SWA_HEADS // 8
SWA_GROUP = SWA_HEADS // SWA_KV_HEADS
WINDOW = 128
SWA_BLOCK = WINDOW
ROPE_THETA = 10000.0

MOE_GROUPS = 4
MOE_EPG = 4
MOE_EXPERTS = MOE_GROUPS * MOE_EPG
MOE_TOPK = 2
MOE_FF = D_MODEL // 4

ALPHA = (2.0 * DEPTH) ** 0.25
BETA = (8.0 * DEPTH) ** -0.25
LN_EPS = 1e-5
RMS_EPS = 1e-6

kernel_name = 'hybrid_conv_hgrn2_s5_swa_hmoe_step'

F32 = jnp.float32


def layer_norm(x, g, b):
    xf = x.astype(F32)
    mu = jnp.mean(xf, -1, keepdims=True)
    var = jnp.mean(jnp.square(xf - mu), -1, keepdims=True)
    return ((xf - mu) * lax.rsqrt(var + LN_EPS) * g.astype(F32) + b.astype(F32)).astype(x.dtype)


def rope(x, pos):
    half = x.shape[-1] // 2
    inv = ROPE_THETA ** (-jnp.arange(half, dtype=F32) / half)
    ang = pos.astype(F32)[:, None] * inv[None, :]
    cos = jnp.cos(ang)[:, None, :]
    sin = jnp.sin(ang)[:, None, :]
    xf = x.astype(F32)
    x1, x2 = xf[..., :half], xf[..., half:]
    return jnp.concatenate([x1 * cos - x2 * sin, x2 * cos + x1 * sin], -1).astype(x.dtype)


def conv_mixer(x, buf, w_in, b_in, w_dw, b_dw, ln_g, ln_b, w_out, b_out):
    h = x @ w_in + b_in
    a, gate = jnp.split(h, 2, axis=-1)
    u = a * jax.nn.sigmoid(gate)
    ext = jnp.concatenate([buf.astype(u.dtype), u], axis=1)
    c = ext.shape[-1]
    conv = lax.conv_general_dilated(
        ext, w_dw[:, None, :].astype(ext.dtype), window_strides=(1,), padding='VALID',
        dimension_numbers=('NWC', 'WIO', 'NWC'), feature_group_count=c) + b_dw
    z = jax.nn.silu(layer_norm(conv, ln_g, ln_b))
    return z @ w_out + b_out, ext[:, -(CONV_WIDTH - 1):]


def hgrn2_chunk(s, q, k, v, lf):
    c = q.shape[1]
    b = jnp.cumsum(lf, axis=1)
    causal = jnp.tril(jnp.ones((c, c), bool))
    diff = b[:, :, None] - b[:, None, :]
    decay = jnp.exp(jnp.where(causal[None, :, :, None, None], diff, -jnp.inf))
    att = jnp.einsum('nthd,ntshd,nshd->nhts', q, decay, k)
    o = jnp.einsum('nthd,nhde->nthe', q * jnp.exp(b), s) + jnp.einsum('nhts,nshe->nthe', att, v)
    b_last = b[:, -1]
    s_new = jnp.exp(b_last)[..., None] * s + jnp.einsum(
        'nshd,nshe->nhde', k * jnp.exp(b_last[:, None] - b), v)
    return s_new, o


def hgrn2_mixer(x, s0, w_in, lb, norm_g, w_out):
    n, t, _ = x.shape
    hk = HGRN_HEADS * HGRN_DK
    hv = HGRN_HEADS * HGRN_DV
    proj = x @ w_in
    q, f, i, g = jnp.split(proj, [hk, 2 * hk, 2 * hk + hv], axis=-1)
    fg = lb + (1.0 - lb) * jax.nn.sigmoid(f.astype(F32))
    q = jax.nn.silu(q.astype(F32)).reshape(n, t, HGRN_HEADS, HGRN_DK)
    k = (1.0 - fg).reshape(n, t, HGRN_HEADS, HGRN_DK)
    lf = jnp.log(fg).reshape(n, t, HGRN_HEADS, HGRN_DK)
    v = i.astype(F32).reshape(n, t, HGRN_HEADS, HGRN_DV)
    c = HGRN_CHUNK if t % HGRN_CHUNK == 0 else t
    nc = t // c

    def to_chunks(a):
        return jnp.moveaxis(a.reshape(n, nc, c, a.shape[2], a.shape[3]), 1, 0)

    s_fin, o = lax.scan(lambda s, inp: hgrn2_chunk(s, *inp), s0.astype(F32),
                        (to_chunks(q), to_chunks(k), to_chunks(v), to_chunks(lf)))
    o = jnp.moveaxis(o, 0, 1).reshape(n, t, HGRN_HEADS, HGRN_DV)
    o = o * lax.rsqrt(jnp.mean(jnp.square(o), -1, keepdims=True) + RMS_EPS)
    o = o * norm_g.astype(F32).reshape(HGRN_HEADS, HGRN_DV)
    o = o.reshape(n, t, hv) * jax.nn.silu(g.astype(F32))
    return o.astype(x.dtype) @ w_out, s_fin.astype(s0.dtype)


def s5_mixer(x, h0_re, h0_im, lam_re, lam_im, log_dt, b_re, b_im, c_re, c_im, d_skip, w_glu):
    n, t, d = x.shape
    u = x.astype(F32).reshape(n, t, SSM_GROUPS, SSM_GROUP)
    dt = jnp.exp(log_dt.astype(F32))[:, None]
    lr, li = lam_re.astype(F32), lam_im.astype(F32)
    mag = jnp.exp(lr * dt)
    ab_re, ab_im = mag * jnp.cos(li * dt), mag * jnp.sin(li * dt)
    nr, ni = ab_re - 1.0, ab_im
    den = lr * lr + li * li
    z_re, z_im = (nr * lr + ni * li) / den, (ni * lr - nr * li) / den
    bf_re, bf_im = b_re.astype(F32), b_im.astype(F32)
    bb_re = z_re[..., None] * bf_re - z_im[..., None] * bf_im
    bb_im = z_re[..., None] * bf_im + z_im[..., None] * bf_re
    bu_re = jnp.einsum('gpc,ntgc->ntgp', bb_re, u)
    bu_im = jnp.einsum('gpc,ntgc->ntgp', bb_im, u)
    h0r, h0i = h0_re.astype(F32), h0_im.astype(F32)
    bu_re = bu_re.at[:, 0].add(ab_re * h0r - ab_im * h0i)
    bu_im = bu_im.at[:, 0].add(ab_re * h0i + ab_im * h0r)
    a_re = jnp.broadcast_to(ab_re, bu_re.shape)
    a_im = jnp.broadcast_to(ab_im, bu_im.shape)

    def combine(e1, e2):
        a1r, a1i, b1r, b1i = e1
        a2r, a2i, b2r, b2i = e2
        return (a1r * a2r - a1i * a2i, a1r * a2i + a1i * a2r,
                a2r * b1r - a2i * b1i + b2r, a2r * b1i + a2i * b1r + b2i)

    _, _, h_re, h_im = lax.associative_scan(combine, (a_re, a_im, bu_re, bu_im), axis=1)
    y = (jnp.einsum('gcp,ntgp->ntgc', c_re.astype(F32), h_re)
         - jnp.einsum('gcp,ntgp->ntgc', c_im.astype(F32), h_im)
         + d_skip.astype(F32).reshape(SSM_GROUPS, SSM_GROUP) * u)
    y = jax.nn.gelu(y.reshape(n, t, d)).astype(x.dtype)
    a, gate = jnp.split(y @ w_glu, 2, axis=-1)
    return a * jax.nn.sigmoid(gate), h_re[:, -1].astype(h0_re.dtype), h_im[:, -1].astype(h0_im.dtype)


def sink_attention(q, k, v, qpos, kpos, sinks):
    s = jnp.einsum('nbqkgd,nbskd->nbkgqs', q, k).astype(F32) * (SWA_HEAD_DIM ** -0.5)
    rel = qpos[:, :, None] - kpos[:, None, :]
    mask = (rel >= 0) & (rel <= WINDOW) & (kpos[:, None, :] >= 0)
    s = jnp.where(mask[None, :, None, None], s, -jnp.inf)
    sk = sinks.astype(F32).reshape(SWA_KV_HEADS, SWA_GROUP)[None, None, :, :, None, None]
    m = jnp.maximum(jnp.max(s, -1, keepdims=True), sk)
    p = jnp.exp(s - m)
    w = p / (jnp.sum(p, -1, keepdims=True) + jnp.exp(sk - m))
    return jnp.einsum('nbkgqs,nbskd->nbqkgd', w.astype(v.dtype), v)


def swa_mixer(x, k_buf, v_buf, pos0, w_qkv, sinks, w_out):
    n, t, _ = x.shape
    hq = SWA_HEADS * SWA_HEAD_DIM
    hkv = SWA_KV_HEADS * SWA_HEAD_DIM
    q, k, v = jnp.split(x @ w_qkv, [hq, hq + hkv], axis=-1)
    pos = pos0 + jnp.arange(t)
    q = rope(q.reshape(n, t, SWA_HEADS, SWA_HEAD_DIM), pos)
    k = rope(k.reshape(n, t, SWA_KV_HEADS, SWA_HEAD_DIM), pos)
    v = v.reshape(n, t, SWA_KV_HEADS, SWA_HEAD_DIM)
    if k_buf is None:
        nb = t // SWA_BLOCK
        z = jnp.zeros((n, SWA_BLOCK, SWA_KV_HEADS, SWA_HEAD_DIM), k.dtype)
        shp = (n, nb, SWA_BLOCK, SWA_KV_HEADS, SWA_HEAD_DIM)
        keys = jnp.concatenate([jnp.concatenate([z, k[:, :t - SWA_BLOCK]], 1).reshape(shp), k.reshape(shp)], 2)
        vals = jnp.concatenate([jnp.concatenate([z, v[:, :t - SWA_BLOCK]], 1).reshape(shp), v.reshape(shp)], 2)
        qb = q.reshape(n, nb, SWA_BLOCK, SWA_KV_HEADS, SWA_GROUP, SWA_HEAD_DIM)
        qpos = pos.reshape(nb, SWA_BLOCK)
        kpos = qpos[:, :1] - SWA_BLOCK + jnp.arange(2 * SWA_BLOCK)[None, :]
        new_k, new_v = k[:, -WINDOW:], v[:, -WINDOW:]
    else:
        ek = jnp.concatenate([k_buf.astype(k.dtype), k], 1)
        ev = jnp.concatenate([v_buf.astype(v.dtype), v], 1)
        keys, vals = ek[:, None], ev[:, None]
        qb = q.reshape(n, 1, t, SWA_KV_HEADS, SWA_GROUP, SWA_HEAD_DIM)
        qpos = pos[None, :]
        kpos = (pos0 - WINDOW + jnp.arange(WINDOW + t))[None, :]
        new_k, new_v = ek[:, -WINDOW:], ev[:, -WINDOW:]
    o = sink_attention(qb, keys, vals, qpos, kpos, sinks).reshape(n, t, hq)
    return o @ w_out, new_k, new_v


def hier_moe(x, w_group, b_group, w_egate, b_egate, w1, w3, w2):
    n, t, d = x.shape
    xt = x.reshape(n * t, d)
    gl = (xt @ w_group).astype(F32) + b_group.astype(F32)
    gp = jax.nn.softmax(gl, -1)
    gsel = jnp.argmax(gl, -1)
    gval = jnp.take_along_axis(gp, gsel[:, None], -1)
    el = jnp.einsum('md,gde->mge', xt, w_egate).astype(F32) + b_egate.astype(F32)
    el = jnp.take_along_axis(el, gsel[:, None, None], axis=1)[:, 0]
    tv, ti = lax.top_k(el, MOE_TOPK)
    ew = jax.nn.softmax(tv, -1) * gval
    eidx = gsel[:, None] * MOE_EPG + ti
    gates = jnp.sum(jax.nn.one_hot(eidx, MOE_EXPERTS, dtype=F32) * ew[..., None], axis=1)
    h = jax.nn.silu(jnp.einsum('md,edf->mef', xt, w1)) * jnp.einsum('md,edf->mef', xt, w3)
    y = jnp.einsum('mef,efd->md', h * gates[..., None].astype(h.dtype), w2)
    return y.reshape(n, t, d)


def setup_inputs(seed: int = 0) -> dict:
    key = jax.random.key(seed)
    keys = iter(jax.random.split(key, 64))

    def nrm(shape, scale=1.0):
        return scale * jax.random.normal(next(keys), shape, F32)

    d = D_MODEL
    hk = HGRN_HEADS * HGRN_DK
    hv = HGRN_HEADS * HGRN_DV
    hq = SWA_HEADS * SWA_HEAD_DIM
    hkv = SWA_KV_HEADS * SWA_HEAD_DIM
    inp = {}
    inp['x_prompt'] = nrm((BATCH, SEQ, d))
    inp['x_sample'] = nrm((DEC_BATCH, DEC_SEQ, d))
    inp['state_conv'] = nrm((N_CONV, DEC_BATCH, CONV_WIDTH - 1, d), 0.5)
    inp['state_hgrn'] = nrm((N_HGRN, DEC_BATCH, HGRN_HEADS, HGRN_DK, HGRN_DV), 0.5)
    inp['state_ssm_re'] = nrm((N_SSM, DEC_BATCH, SSM_GROUPS, SSM_STATE), 0.1)
    inp['state_ssm_im'] = nrm((N_SSM, DEC_BATCH, SSM_GROUPS, SSM_STATE), 0.1)
    inp['cache_swa_k'] = nrm((N_SWA, DEC_BATCH, WINDOW, SWA_KV_HEADS, SWA_HEAD_DIM))
    inp['cache_swa_v'] = nrm((N_SWA, DEC_BATCH, WINDOW, SWA_KV_HEADS, SWA_HEAD_DIM))
    inp['conv_w_in'] = nrm((N_CONV, d, 2 * d), d ** -0.5)
    inp['conv_b_in'] = nrm((N_CONV, 2 * d), 0.01)
    inp['conv_w_dw'] = nrm((N_CONV, CONV_WIDTH, d), CONV_WIDTH ** -0.5)
    inp['conv_b_dw'] = nrm((N_CONV, d), 0.01)
    inp['conv_ln_g'] = 1.0 + nrm((N_CONV, d), 0.02)
    inp['conv_ln_b'] = nrm((N_CONV, d), 0.01)
    inp['conv_w_out'] = nrm((N_CONV, d, d), BETA * d ** -0.5)
    inp['conv_b_out'] = nrm((N_CONV, d), 0.01)
    inp['hgrn_w_in'] = nrm((N_HGRN, d, 2 * hk + 2 * hv), d ** -0.5)
    inp['hgrn_lb_logits'] = nrm((DEPTH, hk), 0.1)
    inp['hgrn_norm_g'] = 1.0 + nrm((N_HGRN, hv), 0.02)
    inp['hgrn_w_out'] = nrm((N_HGRN, hv, d), BETA * hv ** -0.5)
    n_idx = jnp.arange(SSM_STATE, dtype=F32)
    inp['ssm_lam_re'] = -0.5 + nrm((N_SSM, SSM_GROUPS, SSM_STATE), 0.01)
    inp['ssm_lam_im'] = math.pi * n_idx + nrm((N_SSM, SSM_GROUPS, SSM_STATE), 0.01)
    inp['ssm_log_dt'] = jax.random.uniform(next(keys), (N_SSM, SSM_GROUPS), F32,
                                           math.log(DT_MIN), math.log(DT_MAX))
    inp['ssm_b_re'] = nrm((N_SSM, SSM_GROUPS, SSM_STATE, SSM_GROUP), (2 * SSM_GROUP) ** -0.5)
    inp['ssm_b_im'] = nrm((N_SSM, SSM_GROUPS, SSM_STATE, SSM_GROUP), (2 * SSM_GROUP) ** -0.5)
    inp['ssm_c_re'] = nrm((N_SSM, SSM_GROUPS, SSM_GROUP, SSM_STATE), (2 * SSM_STATE) ** -0.5)
    inp['ssm_c_im'] = nrm((N_SSM, SSM_GROUPS, SSM_GROUP, SSM_STATE), (2 * SSM_STATE) ** -0.5)
    inp['ssm_d'] = nrm((N_SSM, d))
    inp['ssm_w_glu'] = jnp.concatenate([nrm((N_SSM, d, d), BETA * d ** -0.5),
                                        nrm((N_SSM, d, d), d ** -0.5)], axis=-1)
    inp['swa_w_qkv'] = nrm((N_SWA, d, hq + 2 * hkv), d ** -0.5)
    inp['swa_sinks'] = nrm((N_SWA, SWA_HEADS))
    inp['swa_w_out'] = nrm((N_SWA, hq, d), BETA * hq ** -0.5)
    inp['ln1_g'] = 1.0 + nrm((DEPTH, d), 0.02)
    inp['ln1_b'] = nrm((DEPTH, d), 0.01)
    inp['ln2_g'] = 1.0 + nrm((DEPTH, d), 0.02)
    inp['ln2_b'] = nrm((DEPTH, d), 0.01)
    inp['moe_w_group'] = nrm((DEPTH, d, MOE_GROUPS), d ** -0.5)
    inp['moe_b_group'] = nrm((DEPTH, MOE_GROUPS), 0.01)
    inp['moe_w_egate'] = nrm((DEPTH, MOE_GROUPS, d, MOE_EPG), d ** -0.5)
    inp['moe_b_egate'] = nrm((DEPTH, MOE_GROUPS, MOE_EPG), 0.01)
    inp['moe_w1'] = nrm((DEPTH, MOE_EXPERTS, d, MOE_FF), d ** -0.5)
    inp['moe_w3'] = nrm((DEPTH, MOE_EXPERTS, d, MOE_FF), d ** -0.5)
    inp['moe_w2'] = nrm((DEPTH, MOE_EXPERTS, MOE_FF, d), BETA * MOE_FF ** -0.5)
    return inp


def reference(x_prompt, x_sample, state_conv, state_hgrn, state_ssm_re, state_ssm_im,
              cache_swa_k, cache_swa_v,
              conv_w_in, conv_b_in, conv_w_dw, conv_b_dw, conv_ln_g, conv_ln_b, conv_w_out, conv_b_out,
              hgrn_w_in, hgrn_lb_logits, hgrn_norm_g, hgrn_w_out,
              ssm_lam_re, ssm_lam_im, ssm_log_dt, ssm_b_re, ssm_b_im, ssm_c_re, ssm_c_im, ssm_d, ssm_w_glu,
              swa_w_qkv, swa_sinks, swa_w_out,
              ln1_g, ln1_b, ln2_g, ln2_b,
              moe_w_group, moe_b_group, moe_w_egate, moe_b_egate, moe_w1, moe_w3, moe_w2):
    lbs = jnp.cumsum(jax.nn.softmax(hgrn_lb_logits.astype(F32), axis=0), axis=0)
    lbs = lbs - lbs[0]
    xp, xs = x_prompt, x_sample
    bp = xp.shape[0]
    conv_p, conv_s, hgrn_p, hgrn_s = [], [], [], []
    ssm_rp, ssm_ip, ssm_rs, ssm_is = [], [], [], []
    k_p, v_p, k_s, v_s = [], [], [], []
    for i in range(DEPTH):
        kind, j = i % N_MIXERS, i // N_MIXERS
        if kind == 0:
            w = (conv_w_in[j], conv_b_in[j], conv_w_dw[j], conv_b_dw[j],
                 conv_ln_g[j], conv_ln_b[j], conv_w_out[j], conv_b_out[j])
            mp, sp = conv_mixer(xp, jnp.zeros((bp, CONV_WIDTH - 1, D_MODEL), xp.dtype), *w)
            ms, ss = conv_mixer(xs, state_conv[j], *w)
            conv_p.append(sp)
            conv_s.append(ss)
        elif kind == 1:
            w = (hgrn_w_in[j], lbs[i], hgrn_norm_g[j], hgrn_w_out[j])
            s0 = jnp.zeros((bp, HGRN_HEADS, HGRN_DK, HGRN_DV), state_hgrn.dtype)
            mp, sp = hgrn2_mixer(xp, s0, *w)
            ms, ss = hgrn2_mixer(xs, state_hgrn[j], *w)
            hgrn_p.append(sp)
            hgrn_s.append(ss)
        elif kind == 2:
            w = (ssm_lam_re[j], ssm_lam_im[j], ssm_log_dt[j], ssm_b_re[j], ssm_b_im[j],
                 ssm_c_re[j], ssm_c_im[j], ssm_d[j], ssm_w_glu[j])
            h0 = jnp.zeros((bp, SSM_GROUPS, SSM_STATE), state_ssm_re.dtype)
            mp, hr, hi = s5_mixer(xp, h0, h0, *w)
            ms, hrs, his = s5_mixer(xs, state_ssm_re[j], state_ssm_im[j], *w)
            ssm_rp.append(hr)
            ssm_ip.append(hi)
            ssm_rs.append(hrs)
            ssm_is.append(his)
        else:
            w = (swa_w_qkv[j], swa_sinks[j], swa_w_out[j])
            mp, nk, nv = swa_mixer(xp, None, None, 0, *w)
            ms, nks, nvs = swa_mixer(xs, cache_swa_k[j], cache_swa_v[j], PAST_LEN, *w)
            k_p.append(nk)
            v_p.append(nv)
            k_s.append(nks)
            v_s.append(nvs)
        xp = layer_norm(ALPHA * xp + mp, ln1_g[i], ln1_b[i])
        xs = layer_norm(ALPHA * xs + ms, ln1_g[i], ln1_b[i])
        mw = (moe_w_group[i], moe_b_group[i], moe_w_egate[i], moe_b_egate[i],
              moe_w1[i], moe_w3[i], moe_w2[i])
        xp = layer_norm(ALPHA * xp + hier_moe(xp, *mw), ln2_g[i], ln2_b[i])
        xs = layer_norm(ALPHA * xs + hier_moe(xs, *mw), ln2_g[i], ln2_b[i])
    return (xp, xs, jnp.stack(conv_p), jnp.stack(conv_s), jnp.stack(hgrn_p), jnp.stack(hgrn_s),
            jnp.stack(ssm_rp), jnp.stack(ssm_ip), jnp.stack(ssm_rs), jnp.stack(ssm_is),
            jnp.stack(k_p), jnp.stack(v_p), jnp.stack(k_s), jnp.stack(v_s))
```

```python
import functools
import math

import jax
import jax.numpy as jnp
from jax import lax
from jax.experimental import pallas as pl
from jax.experimental.pallas import tpu as pltpu

F32 = jnp.float32
BF16 = jnp.bfloat16

D = 2048
BATCH = 4
SEQ = 2048
DEPTH = 4
DEC_BATCH = 32
DEC_SEQ = 4
PAST_LEN = 16384
CONV_WIDTH = 31
HG_H = 16
HG_DK = 128
HG_DV = 128
SSM_GROUP = 16
SSM_GROUPS = 128
SSM_STATE = 64
HD = 64
NH = 32
NKV = 4
WINDOW = 128
ROPE_THETA = 10000.0
MOE_E = 16
MOE_FF = 512
ALPHA = (2.0 * DEPTH) ** 0.25
LN_EPS = 1e-5
RMS_EPS = 1e-6

TS = 8
NP = BATCH * SEQ
NS = DEC_BATCH * TS
MT = NP + NS
LANES = 128
VMEM_LIMIT = 56 * 1024 * 1024

TM_MM = 768
TN_MM = 512
TM_LN = 384
TM_E = 256
P_SLOTS = ((2 * MT + MOE_E * (TM_E - 1) + TM_E - 1) // TM_E) * TM_E
N_ETILES = P_SLOTS // TM_E


def _cp(*sem):
    return pltpu.CompilerParams(dimension_semantics=sem, vmem_limit_bytes=VMEM_LIMIT)


def _sigmoid(x):
    return 1.0 / (1.0 + jnp.exp(-x))


def _silu(x):
    return x * _sigmoid(x)


def _ln_rows(z, g, b):
    mu = jnp.mean(z, -1, keepdims=True)
    zc = z - mu
    var = jnp.mean(zc * zc, -1, keepdims=True)
    return zc * lax.rsqrt(var + LN_EPS) * g + b


def _mm_glu_kernel(x_ref, wa_ref, wg_ref, ba_ref, bg_ref, o_ref):
    x = x_ref[...]
    a = jnp.dot(x, wa_ref[...], preferred_element_type=F32) + ba_ref[...]
    g = jnp.dot(x, wg_ref[...], preferred_element_type=F32) + bg_ref[...]
    o_ref[...] = (a * _sigmoid(g)).astype(o_ref.dtype)


def mm_glu(x, w, b, out_dtype):
    m, k = x.shape
    n = w.shape[1] // 2
    nj = n // TN_MM
    return pl.pallas_call(
        _mm_glu_kernel,
        grid=(nj, m // TM_MM),
        in_specs=[
            pl.BlockSpec((TM_MM, k), lambda j, i: (i, 0)),
            pl.BlockSpec((k, TN_MM), lambda j, i: (0, j)),
            pl.BlockSpec((k, TN_MM), lambda j, i: (0, j + nj)),
            pl.BlockSpec((1, TN_MM), lambda j, i: (0, j)),
            pl.BlockSpec((1, TN_MM), lambda j, i: (0, j + nj)),
        ],
        out_specs=pl.BlockSpec((TM_MM, TN_MM), lambda j, i: (i, j)),
        out_shape=jax.ShapeDtypeStruct((m, n), out_dtype),
        compiler_params=_cp("parallel", "parallel"),
        name="mm_glu",
    )(x, w, w, b, b)


def _mm_hgrn_kernel(x_ref, w_ref, lb_ref, o_ref, *, nsec):
    j = pl.program_id(0)
    acc = jnp.dot(x_ref[...], w_ref[...], preferred_element_type=F32)

    @pl.when(j < nsec)
    def _():
        o_ref[...] = _silu(acc)

    @pl.when((j >= nsec) & (j < 2 * nsec))
    def _():
        lb = lb_ref[...]
        o_ref[...] = jnp.log(lb + (1.0 - lb) * _sigmoid(acc))

    @pl.when((j >= 2 * nsec) & (j < 3 * nsec))
    def _():
        o_ref[...] = acc

    @pl.when(j >= 3 * nsec)
    def _():
        o_ref[...] = _silu(acc)


def mm_hgrn(x, w, lb_row):
    m, k = x.shape
    n = w.shape[1]
    nsec = (n // 4) // TN_MM
    return pl.pallas_call(
        functools.partial(_mm_hgrn_kernel, nsec=nsec),
        grid=(n // TN_MM, m // TM_MM),
        in_specs=[
            pl.BlockSpec((TM_MM, k), lambda j, i: (i, 0)),
            pl.BlockSpec((k, TN_MM), lambda j, i: (0, j)),
            pl.BlockSpec((1, TN_MM), lambda j, i: (0, j)),
        ],
        out_specs=pl.BlockSpec((TM_MM, TN_MM), lambda j, i: (i, j)),
        out_shape=jax.ShapeDtypeStruct((m, n), F32),
        compiler_params=_cp("parallel", "parallel"),
        name="mm_hgrn",
    )(x, w, lb_row)


def _swap_half_heads(x, lane):
    return jnp.where((lane % HD) < HD // 2, pltpu.roll(x, LANES - HD // 2, 1),
                     pltpu.roll(x, HD // 2, 1))


def _mm_rope_kernel(x_ref, w_ref, cos_ref, sin_ref, o_ref, *, n_rope, scale):
    acc = jnp.dot(x_ref[...], w_ref[...], preferred_element_type=F32)
    cos = cos_ref[...]
    sin = sin_ref[...]
    lane = lax.broadcasted_iota(jnp.int32, cos.shape, 1)
    for c in range(acc.shape[1] // LANES):
        xc = acc[:, c * LANES:(c + 1) * LANES]
        if c < n_rope:
            xc = (xc * cos + _swap_half_heads(xc, lane) * sin) * scale
        o_ref[:, c * LANES:(c + 1) * LANES] = xc


def mm_rope(x, w, cos, sin, n_rope, scale):
    m, k = x.shape
    n = w.shape[1]
    return pl.pallas_call(
        functools.partial(_mm_rope_kernel, n_rope=n_rope, scale=scale),
        grid=(n // TN_MM, m // TM_MM),
        in_specs=[
            pl.BlockSpec((TM_MM, k), lambda j, i: (i, 0)),
            pl.BlockSpec((k, TN_MM), lambda j, i: (0, j)),
            pl.BlockSpec((TM_MM, LANES), lambda j, i: (i, 0)),
            pl.BlockSpec((TM_MM, LANES), lambda j, i: (i, 0)),
        ],
        out_specs=pl.BlockSpec((TM_MM, TN_MM), lambda j, i: (i, j)),
        out_shape=jax.ShapeDtypeStruct((m, n), F32),
        compiler_params=_cp("parallel", "parallel"),
        name="mm_rope",
    )(x, w, cos, sin)


def _mm_res_ln_kernel(x_ref, w_ref, b_ref, res_ref, g_ref, beta_ref, of_ref, ob_ref):
    y = jnp.dot(x_ref[...], w_ref[...], preferred_element_type=F32) + b_ref[...]
    out = _ln_rows(ALPHA * res_ref[...] + y, g_ref[...], beta_ref[...])
    of_ref[...] = out
    ob_ref[...] = out.astype(BF16)


def mm_res_ln(x, w, b, res, g, beta):
    m, k = x.shape
    row = lambda i: (i, 0)
    fix = lambda i: (0, 0)
    return pl.pallas_call(
        _mm_res_ln_kernel,
        grid=(m // TM_LN,),
        in_specs=[
            pl.BlockSpec((TM_LN, k), row),
            pl.BlockSpec((k, D), fix),
            pl.BlockSpec((1, D), fix),
            pl.BlockSpec((TM_LN, D), row),
            pl.BlockSpec((1, D), fix),
            pl.BlockSpec((1, D), fix),
        ],
        out_specs=[pl.BlockSpec((TM_LN, D), row), pl.BlockSpec((TM_LN, D), row)],
        out_shape=[jax.ShapeDtypeStruct((m, D), F32), jax.ShapeDtypeStruct((m, D), BF16)],
        compiler_params=_cp("parallel"),
        name="mm_res_ln",
    )(x, w, b, res, g, beta)


def _res_ln_kernel(res_ref, y_ref, g_ref, beta_ref, of_ref, ob_ref):
    out = _ln_rows(ALPHA * res_ref[...] + y_ref[...].astype(F32), g_ref[...], beta_ref[...])
    of_ref[...] = out
    ob_ref[...] = out.astype(BF16)


def res_ln(res, y, g, beta):
    m = res.shape[0]
    row = lambda i: (i, 0)
    fix = lambda i: (0, 0)
    return pl.pallas_call(
        _res_ln_kernel,
        grid=(m // TM_LN,),
        in_specs=[pl.BlockSpec((TM_LN, D), row), pl.BlockSpec((TM_LN, D), row),
                  pl.BlockSpec((1, D), fix), pl.BlockSpec((1, D), fix)],
        out_specs=[pl.BlockSpec((TM_LN, D), row), pl.BlockSpec((TM_LN, D), row)],
        out_shape=[jax.ShapeDtypeStruct((m, D), F32), jax.ShapeDtypeStruct((m, D), BF16)],
        compiler_params=_cp("parallel"),
        name="res_ln",
    )(res, y, g, beta)


HALO = 32
CONV_RC = 64
CONV_LC = 512


def _conv_kernel(u_ref, buf_ref, wdw_ref, bdw_ref, g_ref, b_ref, z_ref, ext_ref, c_ref, *, tt):
    i = pl.program_id(1)

    @pl.when(i == 0)
    def _():
        ext_ref[0:HALO, :] = buf_ref[0]

    @pl.when(i > 0)
    def _():
        ext_ref[0:HALO, :] = ext_ref[tt:tt + HALO, :]

    ext_ref[HALO:HALO + tt, :] = u_ref[...]
    pad = HALO - (CONV_WIDTH - 1)
    rc = min(CONV_RC, tt)
    for r0 in range(0, tt, rc):
        for c0 in range(0, D, CONV_LC):
            acc = jnp.zeros((rc, CONV_LC), F32) + bdw_ref[:, c0:c0 + CONV_LC]
            for w in range(CONV_WIDTH):
                acc = acc + (ext_ref[r0 + pad + w:r0 + pad + w + rc, c0:c0 + CONV_LC]
                             * wdw_ref[w:w + 1, c0:c0 + CONV_LC])
            c_ref[r0:r0 + rc, c0:c0 + CONV_LC] = acc
    z_ref[...] = _silu(_ln_rows(c_ref[...], g_ref[...], b_ref[...])).astype(z_ref.dtype)


def conv_ln_silu(u, row0_blocks, nseq, t, tt, buf, wdw, bdw, g, b, out_dtype):
    nt = t // tt
    fix = lambda s, i: (0, 0)
    return pl.pallas_call(
        functools.partial(_conv_kernel, tt=tt),
        grid=(nseq, nt),
        in_specs=[
            pl.BlockSpec((tt, D), lambda s, i: (row0_blocks + s * nt + i, 0)),
            pl.BlockSpec((1, HALO, D), lambda s, i: (s, 0, 0)),
            pl.BlockSpec((CONV_WIDTH, D), fix),
            pl.BlockSpec((1, D), fix),
            pl.BlockSpec((1, D), fix),
            pl.BlockSpec((1, D), fix),
        ],
        out_specs=pl.BlockSpec((tt, D), lambda s, i: (s * nt + i, 0)),
        out_shape=jax.ShapeDtypeStruct((nseq * t, D), out_dtype),
        scratch_shapes=[pltpu.VMEM((HALO + tt, D), F32), pltpu.VMEM((tt, D), F32)],
        compiler_params=_cp("parallel", "arbitrary"),
        name="conv_ln_silu",
    )(u, buf, wdw, bdw, g, b)


def _hgrn_kernel(p_ref, s0_ref, ng_ref, o_ref, sf_ref, st_ref, pst_ref, ob_ref, *, tt, sc, nvalid):
    i = pl.program_id(1)
    hk = HG_H * HG_DK

    @pl.when(i == 0)
    def _():
        for h in range(HG_H):
            st_ref[h * HG_DV:(h + 1) * HG_DV, :] = s0_ref[0, h].T

    ones = jnp.ones((LANES, LANES), BF16)
    row = lax.broadcasted_iota(jnp.int32, (sc, hk), 0)

    def sub_chunk(ci, carry):
        r0 = pl.multiple_of(ci * sc, sc)
        q = p_ref[pl.ds(r0, sc), 0:hk]
        lf = p_ref[pl.ds(r0, sc), hk:2 * hk]
        v = p_ref[pl.ds(r0, sc), 2 * hk:3 * hk]
        if nvalid < sc:
            lf = jnp.where(row < nvalid, lf, 0.0)
        kk = 1.0 - jnp.exp(lf)
        bl = jnp.zeros((sc, hk), F32)
        for s in range(nvalid):
            bl = bl + jnp.where(row >= s, lf[s:s + 1, :], 0.0)
        btot = bl[sc - 1:sc, :]
        qd = q * jnp.exp(bl)
        kd = kk * jnp.exp(btot - bl)
        for s in range(nvalid):
            e = jnp.exp(jnp.where(row >= s, bl - bl[s:s + 1, :], -jnp.inf))
            pst_ref[s * sc:(s + 1) * sc, :] = q * e * kk[s:s + 1, :]
        o_acc = []
        for h in range(HG_H):
            sl = slice(h * HG_DK, (h + 1) * HG_DK)
            ph = pst_ref[0:nvalid * sc, sl]
            p_hi = ph.astype(BF16)
            p_lo = (ph - p_hi.astype(F32)).astype(BF16)
            rs = (jnp.dot(p_hi, ones, preferred_element_type=F32)
                  + jnp.dot(p_lo, ones, preferred_element_type=F32))
            vh = v[:, sl]
            st_h = st_ref[h * HG_DV:(h + 1) * HG_DV, :]
            oh = lax.dot_general(qd[:, sl].astype(BF16), st_h.astype(BF16),
                                 (((1,), (1,)), ((), ())), preferred_element_type=F32)
            for s in range(nvalid):
                oh = oh + rs[s * sc:(s + 1) * sc, :] * vh[s:s + 1, :]
            o_acc.append(oh)
            upd = jnp.dot(vh.T.astype(BF16), kd[:, sl].astype(BF16), preferred_element_type=F32)
            st_ref[h * HG_DV:(h + 1) * HG_DV, :] = jnp.exp(btot[:, sl]) * st_h + upd
        ob_ref[pl.ds(r0, sc), :] = jnp.concatenate(o_acc, axis=1)
        return carry

    lax.fori_loop(0, tt // sc, sub_chunk, 0)

    o = ob_ref[...]
    o2 = o * o
    parts = []
    for h in range(HG_H):
        sl = slice(h * HG_DV, (h + 1) * HG_DV)
        x2 = o2[:, sl]
        x_hi = x2.astype(BF16)
        x_lo = (x2 - x_hi.astype(F32)).astype(BF16)
        ms = (jnp.dot(x_hi, ones, preferred_element_type=F32)
              + jnp.dot(x_lo, ones, preferred_element_type=F32)) * (1.0 / HG_DV)
        parts.append(o[:, sl] * lax.rsqrt(ms + RMS_EPS))
    on = jnp.concatenate(parts, axis=1) * ng_ref[...] * p_ref[:, 3 * hk:4 * hk]
    o_ref[...] = on.astype(o_ref.dtype)

    @pl.when(i == pl.num_programs(1) - 1)
    def _():
        for h in range(HG_H):
            sf_ref[0, h] = st_ref[h * HG_DV:(h + 1) * HG_DV, :].T


def hgrn_scan(p, row0_blocks, nseq, t, tt, sc, nvalid, s0, norm_g, out_dtype):
    nt = t // tt
    return pl.pallas_call(
        functools.partial(_hgrn_kernel, tt=tt, sc=sc, nvalid=nvalid),
        grid=(nseq, nt),
        in_specs=[
            pl.BlockSpec((tt, 4 * D), lambda s, i: (row0_blocks + s * nt + i, 0)),
            pl.BlockSpec((1, HG_H, HG_DK, HG_DV), lambda s, i: (s, 0, 0, 0)),
            pl.BlockSpec((1, D), lambda s, i: (0, 0)),
        ],
        out_specs=[
            pl.BlockSpec((tt, D), lambda s, i: (s * nt + i, 0)),
            pl.BlockSpec((1, HG_H, HG_DK, HG_DV), lambda s, i: (s, 0, 0, 0)),
        ],
        out_shape=[jax.ShapeDtypeStruct((nseq * t, D), out_dtype),
                   jax.ShapeDtypeStruct((nseq, HG_H, HG_DK, HG_DV), F32)],
        scratch_shapes=[pltpu.VMEM((HG_H * HG_DV, HG_DK), F32),
                        pltpu.VMEM((sc * sc, D), F32),
                        pltpu.VMEM((tt, D), F32)],
        compiler_params=_cp("parallel", "arbitrary"),
        name="hgrn_scan",
    )(p, s0, norm_g)


S5_GB = 8


def _s5_kernel(u_ref, h0_ref, h0s_ref, t_ref, bp_ref, cp_ref, ar_ref, ai_ref, y_ref, hf_ref,
               con_ref, hp_ref, *, nc, nb):
    p2 = 2 * SSM_STATE
    for gi in range(S5_GB):
        con_ref[gi] = jnp.dot(u_ref[gi], bp_ref[gi], preferred_element_type=F32)
    ar = ar_ref[...]
    ai = ai_ref[...]

    def chunk(c, carry):
        h, hs = carry
        r0 = pl.multiple_of(c * nb, nb)
        hp_ref[:, pl.ds(r0, nb), :] = h
        cc = con_ref[:, pl.ds(r0, nb), :]
        h2 = ar * h + ai * hs + cc[:, :, 0:p2]
        hs2 = ar * hs - ai * h + cc[:, :, p2:2 * p2]
        return h2, hs2

    h, _ = lax.fori_loop(0, nc, chunk, (h0_ref[...], h0s_ref[...]))
    hf_ref[...] = h
    for gi in range(S5_GB):
        y = (jnp.dot(u_ref[gi], t_ref[gi], preferred_element_type=F32)
             + jnp.dot(hp_ref[gi].astype(BF16), cp_ref[gi], preferred_element_type=F32))
        y_ref[gi] = jax.nn.gelu(y).astype(y_ref.dtype)


def s5_lti(u, h0, h0s, tmat, bpow, cpow, ar, ai, nc, nb):
    g, rows, lw = u.shape
    p2 = 2 * SSM_STATE
    blk = lambda j: (j, 0, 0)
    return pl.pallas_call(
        functools.partial(_s5_kernel, nc=nc, nb=nb),
        grid=(g // S5_GB,),
        in_specs=[
            pl.BlockSpec((S5_GB, rows, lw), blk),
            pl.BlockSpec((S5_GB, nb, p2), blk),
            pl.BlockSpec((S5_GB, nb, p2), blk),
            pl.BlockSpec((S5_GB, lw, lw), blk),
            pl.BlockSpec((S5_GB, lw, 2 * p2), blk),
            pl.BlockSpec((S5_GB, p2, lw), blk),
            pl.BlockSpec((S5_GB, 1, p2), blk),
            pl.BlockSpec((S5_GB, 1, p2), blk),
        ],
        out_specs=[pl.BlockSpec((S5_GB, rows, lw), blk), pl.BlockSpec((S5_GB, nb, p2), blk)],
        out_shape=[jax.ShapeDtypeStruct((g, rows, lw), BF16),
                   jax.ShapeDtypeStruct((g, nb, p2), F32)],
        scratch_shapes=[pltpu.VMEM((S5_GB, rows, 2 * p2), F32),
                        pltpu.VMEM((S5_GB, rows, p2), F32)],
        compiler_params=_cp("parallel"),
        name="s5_lti",
    )(u, h0, h0s, tmat, bpow, cpow, ar, ai)


def _s5_operators(lam_re, lam_im, log_dt, b_re, b_im, c_re, c_im, d_skip, lc, nv):
    hp = lax.Precision.HIGHEST
    dt = jnp.exp(log_dt)[:, None]
    lr, li = lam_re, lam_im
    mag = jnp.exp(lr * dt)
    ab_re, ab_im = mag * jnp.cos(li * dt), mag * jnp.sin(li * dt)
    nr, ni = ab_re - 1.0, ab_im
    den = lr * lr + li * li
    z_re, z_im = (nr * lr + ni * li) / den, (ni * lr - nr * li) / den
    bb_re = z_re[..., None] * b_re - z_im[..., None] * b_im
    bb_im = z_re[..., None] * b_im + z_im[..., None] * b_re
    l = jnp.arange(lc + 1, dtype=F32)[None, :, None]
    pmag = jnp.exp(l * (lr * dt)[:, None, :])
    pw_re = pmag * jnp.cos(l * (li * dt)[:, None, :])
    pw_im = pmag * jnp.sin(l * (li * dt)[:, None, :])
    w_re = pw_re[..., None] * bb_re[:, None] - pw_im[..., None] * bb_im[:, None]
    w_im = pw_re[..., None] * bb_im[:, None] + pw_im[..., None] * bb_re[:, None]
    kl = (jnp.einsum('gcp,glpd->glcd', c_re, w_re[:, :lc], precision=hp)
          - jnp.einsum('gcp,glpd->glcd', c_im, w_im[:, :lc], precision=hp))
    eye = jnp.eye(SSM_GROUP, dtype=F32)
    kl = kl.at[:, 0].add(d_skip.reshape(SSM_GROUPS, SSM_GROUP)[:, :, None] * eye[None])
    s_idx = jnp.arange(lc)[:, None]
    t_idx = jnp.arange(lc)[None, :]
    lag = jnp.clip(t_idx - s_idx, 0, lc - 1)
    tm = kl[:, lag]
    tm = jnp.where((t_idx >= s_idx)[None, :, :, None, None], tm, 0.0)
    tmat = tm.transpose(0, 1, 4, 2, 3).reshape(SSM_GROUPS, lc * SSM_GROUP, lc * SSM_GROUP)
    ca_re = (c_re[:, None] * pw_re[:, 1:, None, :] - c_im[:, None] * pw_im[:, 1:, None, :])
    ca_im = (c_re[:, None] * pw_im[:, 1:, None, :] + c_im[:, None] * pw_re[:, 1:, None, :])
    cpow = jnp.concatenate([ca_re, -ca_im], axis=-1)
    cpow = cpow.transpose(0, 3, 1, 2).reshape(SSM_GROUPS, 2 * SSM_STATE, lc * SSM_GROUP)
    s_all = jnp.arange(lc)
    e_idx = jnp.clip(nv - 1 - s_all, 0, lc)
    valid = (s_all < nv)[None, :, None, None]
    bw_re = jnp.where(valid, w_re[:, e_idx], 0.0)
    bw_im = jnp.where(valid, w_im[:, e_idx], 0.0)
    bw_re = bw_re.transpose(0, 1, 3, 2).reshape(SSM_GROUPS, lc * SSM_GROUP, SSM_STATE)
    bw_im = bw_im.transpose(0, 1, 3, 2).reshape(SSM_GROUPS, lc * SSM_GROUP, SSM_STATE)
    bpow = jnp.concatenate([bw_re, bw_im, bw_im, bw_re], axis=-1)
    a_re, a_im = pw_re[:, nv], pw_im[:, nv]
    ar = jnp.concatenate([a_re, a_re], -1)[:, None, :]
    ai = jnp.concatenate([-a_im, a_im], -1)[:, None, :]
    return tmat.astype(BF16), bpow.astype(BF16), cpow.astype(BF16), ar, ai


def _attn_kernel(q_ref, kv_ref, pk_ref, pv_ref, sink_ref, o_ref, *, tq, nblk, always_prev):
    idx = pl.program_id(0)
    kvw = NKV * HD
    kv = kv_ref[0]
    to = kv.shape[0]
    pkn = pk_ref.shape[1]
    rowp = lax.broadcasted_iota(jnp.int32, (tq, pkn), 0)
    colp = lax.broadcasted_iota(jnp.int32, (tq, pkn), 1)
    mask_p = colp >= rowp
    if not always_prev:
        mask_p = mask_p & ((idx % nblk) > 0)
    rowo = lax.broadcasted_iota(jnp.int32, (tq, to), 0)
    colo = lax.broadcasted_iota(jnp.int32, (tq, to), 1)
    mask_o = colo <= rowo
    low = lax.broadcasted_iota(jnp.int32, (tq, LANES), 1) < HD
    nt = (((1,), (1,)), ((), ()))
    group = NH // NKV
    for kb in range(NKV // 2):
        ko = kv[:, kb * LANES:(kb + 1) * LANES].astype(BF16)
        vo = kv[:, kvw + kb * LANES:kvw + (kb + 1) * LANES].astype(BF16)
        kp = pk_ref[0, :, kb * LANES:(kb + 1) * LANES].astype(BF16)
        vp = pv_ref[0, :, kb * LANES:(kb + 1) * LANES].astype(BF16)
        for hi in range(2):
            j = 2 * kb + hi
            for pr in range(group // 2):
                qb = (group * j) // 2 + pr
                q2 = q_ref[0, :, qb * LANES:(qb + 1) * LANES]
                out = jnp.zeros((tq, LANES), F32)
                for e in range(2):
                    sel = low if e == 0 else jnp.logical_not(low)
                    qe = jnp.where(sel, q2, 0.0)
                    if e != hi:
                        qe = pltpu.roll(qe, HD, 1)
                    qe = qe.astype(BF16)
                    s_p = lax.dot_general(qe, kp, nt, preferred_element_type=F32)
                    s_o = lax.dot_general(qe, ko, nt, preferred_element_type=F32)
                    s_p = jnp.where(mask_p, s_p, -jnp.inf)
                    s_o = jnp.where(mask_o, s_o, -jnp.inf)
                    sk = sink_ref[2 * qb + e]
                    m = jnp.maximum(jnp.maximum(jnp.max(s_p, -1, keepdims=True),
                                                jnp.max(s_o, -1, keepdims=True)), sk)
                    p_p = jnp.exp(s_p - m)
                    p_o = jnp.exp(s_o - m)
                    den = (jnp.sum(p_p, -1, keepdims=True) + jnp.sum(p_o, -1, keepdims=True)
                           + jnp.exp(sk - m))
                    oe = (jnp.dot((p_p / den).astype(BF16), vp, preferred_element_type=F32)
                          + jnp.dot((p_o / den).astype(BF16), vo, preferred_element_type=F32))
                    if e != hi:
                        oe = pltpu.roll(oe, HD, 1)
                    out = jnp.where(sel, oe, out)
                o_ref[0, :, qb * LANES:(qb + 1) * LANES] = out.astype(o_ref.dtype)


def swa_attention(q3, kv3, pk3, pv3, pk_map, pv_map, sinks, nblk, always_prev):
    nb, tq, _ = q3.shape
    to = kv3.shape[1]
    kvw = NKV * HD
    return pl.pallas_call(
        functools.partial(_attn_kernel, tq=tq, nblk=nblk, always_prev=always_prev),
        grid=(nb,),
        in_specs=[
            pl.BlockSpec((1, tq, D), lambda i: (i, 0, 0)),
            pl.BlockSpec((1, to, 2 * kvw), lambda i: (i, 0, 0)),
            pl.BlockSpec((1, pk3.shape[1], kvw), pk_map),
            pl.BlockSpec((1, pv3.shape[1], kvw), pv_map),
            pl.BlockSpec(memory_space=pltpu.SMEM),
        ],
        out_specs=pl.BlockSpec((1, tq, D), lambda i: (i, 0, 0)),
        out_shape=jax.ShapeDtypeStruct((nb, tq, D), BF16),
        compiler_params=_cp("parallel"),
        name="swa_attention",
    )(q3, kv3, pk3, pv3, sinks)


def _router_kernel(x_ref, w_ref, b_ref, o_ref):
    l = jnp.dot(x_ref[...], w_ref[...], preferred_element_type=F32,
                precision=lax.Precision.HIGHEST) + b_ref[...]
    lane = lax.broadcasted_iota(jnp.int32, l.shape, 1).astype(F32)
    big = 1e9
    ninf = -jnp.inf
    gmask = lane < 4.0
    gl = jnp.where(gmask, l, ninf)
    gm = jnp.max(gl, -1, keepdims=True)
    gsel = jnp.min(jnp.where(gl == gm, lane, big), -1, keepdims=True)
    gsum = jnp.sum(jnp.where(gmask, jnp.exp(gl - gm), 0.0), -1, keepdims=True)
    gval = 1.0 / gsum
    lo = 4.0 + 4.0 * gsel
    emask = (lane >= lo) & (lane < lo + 4.0)
    el = jnp.where(emask, l, ninf)
    m1 = jnp.max(el, -1, keepdims=True)
    i1 = jnp.min(jnp.where(emask & (el == m1), lane, big), -1, keepdims=True)
    emask2 = emask & (lane != i1)
    el2 = jnp.where(emask2, l, ninf)
    m2 = jnp.max(el2, -1, keepdims=True)
    i2 = jnp.min(jnp.where(emask2 & (el2 == m2), lane, big), -1, keepdims=True)
    r = jnp.exp(m2 - m1)
    w1 = gval / (1.0 + r)
    w2 = gval * r / (1.0 + r)
    o_ref[...] = jnp.where(lane == 0.0, i1 - 4.0,
                           jnp.where(lane == 1.0, i2 - 4.0,
                                     jnp.where(lane == 2.0, w1,
                                               jnp.where(lane == 3.0, w2, 0.0))))


def moe_router(x, w, b):
    m = x.shape[0]
    return pl.pallas_call(
        _router_kernel,
        grid=(m // TM_LN,),
        in_specs=[pl.BlockSpec((TM_LN, D), lambda i: (i, 0)),
                  pl.BlockSpec((D, LANES), lambda i: (0, 0)),
                  pl.BlockSpec((1, LANES), lambda i: (0, 0))],
        out_specs=pl.BlockSpec((TM_LN, LANES), lambda i: (i, 0)),
        out_shape=jax.ShapeDtypeStruct((m, LANES), F32),
        compiler_params=_cp("parallel"),
        name="moe_router",
    )(x, w, b)


def _ffn_kernel(te_ref, nt_ref, xs_ref, g_ref, w1_ref, w3_ref, w2_ref, y_ref, w1b, w3b, w2b):
    i = pl.program_id(0)

    @pl.when(i < nt_ref[0])
    def _():
        prev = te_ref[jnp.maximum(i - 1, 0)]

        @pl.when((i == 0) | (te_ref[i] != prev))
        def _():
            w1b[...] = w1_ref[0].astype(BF16)
            w3b[...] = w3_ref[0].astype(BF16)
            w2b[...] = w2_ref[0].astype(BF16)

        x = xs_ref[...]
        h1 = jnp.dot(x, w1b[...], preferred_element_type=F32)
        h3 = jnp.dot(x, w3b[...], preferred_element_type=F32)
        h = _silu(h1) * h3 * g_ref[...]
        y_ref[...] = jnp.dot(h.astype(BF16), w2b[...], preferred_element_type=F32).astype(y_ref.dtype)

    @pl.when(i >= nt_ref[0])
    def _():
        y_ref[...] = jnp.zeros(y_ref.shape, y_ref.dtype)


def moe_ffn(tile_expert, n_tiles, xs, gate, w1, w3, w2):
    wmap = lambda i, te, nt: (te[i], 0, 0)
    grid_spec = pltpu.PrefetchScalarGridSpec(
        num_scalar_prefetch=2,
        grid=(N_ETILES,),
        in_specs=[
            pl.BlockSpec((TM_E, D), lambda i, te, nt: (i, 0)),
            pl.BlockSpec((TM_E, 1), lambda i, te, nt: (i, 0)),
            pl.BlockSpec((1, D, MOE_FF), wmap),
            pl.BlockSpec((1, D, MOE_FF), wmap),
            pl.BlockSpec((1, MOE_FF, D), wmap),
        ],
        out_specs=pl.BlockSpec((TM_E, D), lambda i, te, nt: (i, 0)),
        scratch_shapes=[pltpu.VMEM((D, MOE_FF), BF16), pltpu.VMEM((D, MOE_FF), BF16),
                        pltpu.VMEM((MOE_FF, D), BF16)],
    )
    return pl.pallas_call(
        _ffn_kernel,
        grid_spec=grid_spec,
        out_shape=jax.ShapeDtypeStruct((P_SLOTS, D), F32),
        compiler_params=_cp("arbitrary"),
        name="moe_ffn",
    )(tile_expert, n_tiles, xs, gate, w1, w3, w2)


def _dispatch_plan(route):
    e_flat = route[:, 0:2].astype(jnp.int32).reshape(-1)
    w_flat = route[:, 2:4].reshape(-1)
    onehot = (e_flat[:, None] == jnp.arange(MOE_E)[None, :]).astype(jnp.int32)
    csum = jnp.cumsum(onehot, axis=0)
    rank = jnp.take_along_axis(csum - onehot, e_flat[:, None], axis=1)[:, 0]
    counts = csum[-1]
    padded = ((counts + TM_E - 1) // TM_E) * TM_E
    ends = jnp.cumsum(padded)
    starts = ends - padded
    pos = starts[e_flat] + rank
    tok = jnp.arange(2 * MT, dtype=jnp.int32) // 2
    src = jnp.zeros((P_SLOTS,), jnp.int32).at[pos].set(tok)
    gate = jnp.zeros((P_SLOTS,), F32).at[pos].set(w_flat)
    n_tiles = (ends[-1] // TM_E).astype(jnp.int32)
    tile_start = jnp.arange(N_ETILES, dtype=jnp.int32) * TM_E
    te = jnp.minimum(jnp.searchsorted(ends, tile_start, side='right'), MOE_E - 1).astype(jnp.int32)
    te = jnp.where(jnp.arange(N_ETILES) < n_tiles, te, te[jnp.maximum(n_tiles - 1, 0)])
    return src, gate, te, n_tiles.reshape(1), pos.reshape(MT, 2)


def hier_moe_block(xf, xb, i, moe_w_group, moe_b_group, moe_w_egate, moe_b_egate,
                   moe_w1, moe_w3, moe_w2, ln_g, ln_b):
    wr = jnp.concatenate([moe_w_group[i], moe_w_egate[i].transpose(1, 0, 2).reshape(D, MOE_E)], axis=1)
    wr = jnp.pad(wr, ((0, 0), (0, LANES - wr.shape[1])))
    br = jnp.concatenate([moe_b_group[i], moe_b_egate[i].reshape(MOE_E)])
    br = jnp.pad(br, (0, LANES - br.shape[0]))[None, :]
    route = moe_router(xf, wr, br)
    src, gate, te, n_tiles, pos = _dispatch_plan(route)
    xs = jnp.take(xb, src, axis=0)
    ys = moe_ffn(te, n_tiles, xs, gate[:, None], moe_w1[i], moe_w3[i], moe_w2[i])
    y = jnp.take(ys, pos[:, 0], axis=0) + jnp.take(ys, pos[:, 1], axis=0)
    return res_ln(xf, y, ln_g[i][None, :], ln_b[i][None, :])


def _sample_rows(a):
    return a[NP:].reshape(DEC_BATCH, TS, a.shape[-1])


def conv_layer(xf, xb, state, w_in, b_in, w_dw, b_dw, g, b, w_out, b_out, ln_g, ln_b):
    u = mm_glu(xb, w_in.astype(BF16), b_in[None, :], F32)
    tt_p = 256
    zbuf = jnp.zeros((BATCH, HALO, D), F32)
    z_p = conv_ln_silu(u, 0, BATCH, SEQ, tt_p, zbuf, w_dw, b_dw[None, :], g[None, :], b[None, :], BF16)
    pad = HALO - (CONV_WIDTH - 1)
    sbuf = jnp.pad(state, ((0, 0), (pad, 0), (0, 0)))
    z_s = conv_ln_silu(u, NP // TS, DEC_BATCH, TS, TS, sbuf, w_dw, b_dw[None, :], g[None, :], b[None, :], F32)
    z = jnp.concatenate([z_p, z_s.astype(BF16)], axis=0)
    xf, xb = mm_res_ln(z, w_out.astype(BF16), b_out[None, :], xf, ln_g[None, :], ln_b[None, :])
    new_p = u[:NP].reshape(BATCH, SEQ, D)[:, SEQ - (CONV_WIDTH - 1):]
    new_s = jnp.concatenate([state[:, DEC_SEQ:], _sample_rows(u)[:, :DEC_SEQ]], axis=1)
    return xf, xb, new_p, new_s


def hgrn_layer(xf, xb, state, w_in, lb, norm_g, w_out, ln_g, ln_b):
    hk = HG_H * HG_DK
    lb_row = jnp.concatenate([jnp.zeros((hk,), F32), lb, jnp.zeros((2 * hk,), F32)])[None, :]
    p = mm_hgrn(xb, w_in.astype(BF16), lb_row)
    s0_p = jnp.zeros((BATCH, HG_H, HG_DK, HG_DV), F32)
    o_p, sf_p = hgrn_scan(p, 0, BATCH, SEQ, 128, 16, 16, s0_p, norm_g[None, :], BF16)
    o_s, sf_s = hgrn_scan(p, NP // TS, DEC_BATCH, TS, TS, TS, DEC_SEQ, state, norm_g[None, :], F32)
    o = jnp.concatenate([o_p, o_s.astype(BF16)], axis=0)
    zero_b = jnp.zeros((1, D), F32)
    xf, xb = mm_res_ln(o, w_out.astype(BF16), zero_b, xf, ln_g[None, :], ln_b[None, :])
    return xf, xb, sf_p, sf_s


def s5_layer(xf, xb, h_re, h_im, lam_re, lam_im, log_dt, b_re, b_im, c_re, c_im, d_skip, w_glu,
             ln_g, ln_b):
    g, c, p = SSM_GROUPS, SSM_GROUP, SSM_STATE
    lc, nbp = 16, 8
    nc = SEQ // lc
    ops = _s5_operators(lam_re, lam_im, log_dt, b_re, b_im, c_re, c_im, d_skip, lc, lc)
    up = xb[:NP].reshape(BATCH, nc, lc, g, c)
    up = jnp.pad(up, ((0, nbp - BATCH), (0, 0), (0, 0), (0, 0), (0, 0)))
    up = up.transpose(3, 1, 0, 2, 4).reshape(g, nc * nbp, lc * c)
    zero_h = jnp.zeros((g, nbp, 2 * p), F32)
    y_p, hf_p = s5_lti(up, zero_h, zero_h, *ops, nc, nbp)
    y_p = y_p.reshape(g, nc, nbp, lc, c)[:, :, :BATCH].transpose(2, 1, 3, 0, 4).reshape(NP, D)
    hf_p = hf_p[:, :BATCH].transpose(1, 0, 2)
    ops_s = _s5_operators(lam_re, lam_im, log_dt, b_re, b_im, c_re, c_im, d_skip, TS, DEC_SEQ)
    us = _sample_rows(xb).reshape(DEC_BATCH, TS, g, c).transpose(2, 0, 1, 3).reshape(g, DEC_BATCH, TS * c)
    h0 = jnp.concatenate([h_re, h_im], -1).transpose(1, 0, 2)
    h0s = jnp.concatenate([h_im, h_re], -1).transpose(1, 0, 2)
    y_s, hf_s = s5_lti(us, h0, h0s, *ops_s, 1, DEC_BATCH)
    y_s = y_s.reshape(g, DEC_BATCH, TS, c).transpose(1, 2, 0, 3).reshape(NS, D)
    hf_s = hf_s.transpose(1, 0, 2)
    y = jnp.concatenate([y_p, y_s], axis=0)
    m = mm_glu(y, w_glu.astype(BF16), jnp.zeros((1, 2 * D), F32), BF16)
    xf, xb = res_ln(xf, m, ln_g[None, :], ln_b[None, :])
    return xf, xb, hf_p[..., :p], hf_p[..., p:], hf_s[..., :p], hf_s[..., p:]


def _rope_tables(pos):
    half = HD // 2
    inv = ROPE_THETA ** (-jnp.arange(half, dtype=F32) / half)
    ang = pos.astype(F32)[:, None] * inv[None, :]
    cos, sin = jnp.cos(ang), jnp.sin(ang)
    cos_t = jnp.tile(jnp.concatenate([cos, cos], -1), (1, LANES // HD))
    sin_t = jnp.tile(jnp.concatenate([-sin, sin], -1), (1, LANES // HD))
    return cos_t, sin_t


def swa_layer(xf, xb, cache_k, cache_v, w_qkv, sinks, w_out, ln_g, ln_b):
    hq = NH * HD
    kvw = NKV * HD
    pos = jnp.concatenate([jnp.tile(jnp.arange(SEQ), BATCH),
                           jnp.tile(PAST_LEN + jnp.arange(TS), DEC_BATCH)])
    cos_t, sin_t = _rope_tables(pos)
    wb = w_qkv.astype(BF16)
    q = mm_rope(xb, wb[:, :hq], cos_t, sin_t, TN_MM // LANES, HD ** -0.5)
    kv = mm_rope(xb, wb[:, hq:], cos_t, sin_t, kvw // LANES, 1.0)
    nblk = SEQ // WINDOW
    q3 = q[:NP].reshape(BATCH * nblk, WINDOW, D)
    kv3 = kv[:NP].reshape(BATCH * nblk, WINDOW, 2 * kvw)
    prev = lambda i: (jnp.maximum(i - 1, 0), 0, 0)
    prev_v = lambda i: (jnp.maximum(i - 1, 0), 0, 1)
    o_p = swa_attention(q3, kv3, kv3, kv3, prev, prev_v, sinks, nblk, False).reshape(NP, D)
    qs = _sample_rows(q)
    kvs = _sample_rows(kv)
    kvs_pad = jnp.pad(kvs, ((0, 0), (0, WINDOW - TS), (0, 0)))
    ck = cache_k.reshape(DEC_BATCH, WINDOW, kvw)
    cv = cache_v.reshape(DEC_BATCH, WINDOW, kvw)
    same = lambda i: (i, 0, 0)
    o_s = swa_attention(qs, kvs_pad, ck, cv, same, same, sinks, 1, True).reshape(NS, D)
    o = jnp.concatenate([o_p, o_s], axis=0)
    xf, xb = mm_res_ln(o, w_out.astype(BF16), jnp.zeros((1, D), F32), xf, ln_g[None, :], ln_b[None, :])
    kvp = kv[:NP].reshape(BATCH, SEQ, 2 * kvw)[:, SEQ - WINDOW:]
    nk_p = kvp[..., :kvw].reshape(BATCH, WINDOW, NKV, HD)
    nv_p = kvp[..., kvw:].reshape(BATCH, WINDOW, NKV, HD)
    nk_s = jnp.concatenate([ck[:, DEC_SEQ:], kvs[:, :DEC_SEQ, :kvw]], axis=1).reshape(DEC_BATCH, WINDOW, NKV, HD)
    nv_s = jnp.concatenate([cv[:, DEC_SEQ:], kvs[:, :DEC_SEQ, kvw:]], axis=1).reshape(DEC_BATCH, WINDOW, NKV, HD)
    return xf, xb, nk_p, nv_p, nk_s, nv_s


def kernel(x_prompt, x_sample, state_conv, state_hgrn, state_ssm_re, state_ssm_im, cache_swa_k, cache_swa_v, conv_w_in, conv_b_in, conv_w_dw, conv_b_dw, conv_ln_g, conv_ln_b, conv_w_out, conv_b_out, hgrn_w_in, hgrn_lb_logits, hgrn_norm_g, hgrn_w_out, ssm_lam_re, ssm_lam_im, ssm_log_dt, ssm_b_re, ssm_b_im, ssm_c_re, ssm_c_im, ssm_d, ssm_w_glu, swa_w_qkv, swa_sinks, swa_w_out, ln1_g, ln1_b, ln2_g, ln2_b, moe_w_group, moe_b_group, moe_w_egate, moe_b_egate, moe_w1, moe_w3, moe_w2):
    lbs = jnp.cumsum(jax.nn.softmax(hgrn_lb_logits.astype(F32), axis=0), axis=0)
    lbs = lbs - lbs[0]
    xs_pad = jnp.pad(x_sample, ((0, 0), (0, TS - DEC_SEQ), (0, 0)))
    xf = jnp.concatenate([x_prompt.reshape(NP, D), xs_pad.reshape(NS, D)], axis=0)
    xb = xf.astype(BF16)
    moe = (moe_w_group, moe_b_group, moe_w_egate, moe_b_egate, moe_w1, moe_w3, moe_w2, ln2_g, ln2_b)

    xf, xb, conv_p, conv_s = conv_layer(
        xf, xb, state_conv[0], conv_w_in[0], conv_b_in[0], conv_w_dw[0], conv_b_dw[0],
        conv_ln_g[0], conv_ln_b[0], conv_w_out[0], conv_b_out[0], ln1_g[0], ln1_b[0])
    xf, xb = hier_moe_block(xf, xb, 0, *moe)

    xf, xb, hg_p, hg_s = hgrn_layer(xf, xb, state_hgrn[0], hgrn_w_in[0], lbs[1], hgrn_norm_g[0],
                                    hgrn_w_out[0], ln1_g[1], ln1_b[1])
    xf, xb = hier_moe_block(xf, xb, 1, *moe)

    xf, xb, hr_p, hi_p, hr_s, hi_s = s5_layer(
        xf, xb, state_ssm_re[0], state_ssm_im[0], ssm_lam_re[0], ssm_lam_im[0], ssm_log_dt[0],
        ssm_b_re[0], ssm_b_im[0], ssm_c_re[0], ssm_c_im[0], ssm_d[0], ssm_w_glu[0], ln1_g[2], ln1_b[2])
    xf, xb = hier_moe_block(xf, xb, 2, *moe)

    xf, xb, k_p, v_p, k_s, v_s = swa_layer(xf, xb, cache_swa_k[0], cache_swa_v[0], swa_w_qkv[0],
                                            swa_sinks[0], swa_w_out[0], ln1_g[3], ln1_b[3])
    xf, xb = hier_moe_block(xf, xb, 3, *moe)

    y_p = xf[:NP].reshape(BATCH, SEQ, D)
    y_s = _sample_rows(xf)[:, :DEC_SEQ]
    return (y_p, y_s, conv_p[None], conv_s[None], hg_p[None], hg_s[None],
            hr_p[None], hi_p[None], hr_s[None], hi_s[None],
            k_p[None], v_p[None], k_s[None], v_s[None])
```

```python
import functools

import jax
import jax.numpy as jnp
from jax import lax
from jax.experimental import pallas as pl
from jax.experimental.pallas import tpu as pltpu

F32 = jnp.float32
BF16 = jnp.bfloat16

D = 2048
BATCH = 4
SEQ = 2048
DEPTH = 4
DEC_BATCH = 32
DEC_SEQ = 4
PAST_LEN = 16384
CONV_WIDTH = 31
HG_H = 16
HG_DK = 128
HG_DV = 128
SSM_GROUP = 16
SSM_GROUPS = 128
SSM_STATE = 64
HD = 64
NH = 32
NKV = 4
WINDOW = 128
ROPE_THETA = 10000.0
MOE_E = 16
MOE_FF = 512
ALPHA = (2.0 * DEPTH) ** 0.25
LN_EPS = 1e-5
RMS_EPS = 1e-6

TS = 8
NP = BATCH * SEQ
NS = DEC_BATCH * TS
MT = NP + NS
LANES = 128
VMEM_LIMIT = 56 * 1024 * 1024

TM_MM = 768
TN_MM = 512
TM_LN = 384
TM_E = 512
P_SLOTS = ((2 * MT + MOE_E * (TM_E - 1) + TM_E - 1) // TM_E) * TM_E
N_ETILES = P_SLOTS // TM_E


def _cp(*sem):
    return pltpu.CompilerParams(dimension_semantics=sem, vmem_limit_bytes=VMEM_LIMIT)


def _sigmoid(x):
    return 1.0 / (1.0 + jnp.exp(-x))


def _silu(x):
    return x * _sigmoid(x)


def _ln_rows(z, g, b):
    mu = jnp.mean(z, -1, keepdims=True)
    zc = z - mu
    var = jnp.mean(zc * zc, -1, keepdims=True)
    return zc * lax.rsqrt(var + LN_EPS) * g + b


def _mm_glu_kernel(x_ref, wa_ref, wg_ref, ba_ref, bg_ref, o_ref):
    x = x_ref[...]
    a = jnp.dot(x, wa_ref[...], preferred_element_type=F32) + ba_ref[...]
    g = jnp.dot(x, wg_ref[...], preferred_element_type=F32) + bg_ref[...]
    o_ref[...] = (a * _sigmoid(g)).astype(o_ref.dtype)


def mm_glu(x, w, b, out_dtype):
    m, k = x.shape
    n = w.shape[1] // 2
    nj = n // TN_MM
    return pl.pallas_call(
        _mm_glu_kernel,
        grid=(nj, m // TM_MM),
        in_specs=[
            pl.BlockSpec((TM_MM, k), lambda j, i: (i, 0)),
            pl.BlockSpec((k, TN_MM), lambda j, i: (0, j)),
            pl.BlockSpec((k, TN_MM), lambda j, i: (0, j + nj)),
            pl.BlockSpec((1, TN_MM), lambda j, i: (0, j)),
            pl.BlockSpec((1, TN_MM), lambda j, i: (0, j + nj)),
        ],
        out_specs=pl.BlockSpec((TM_MM, TN_MM), lambda j, i: (i, j)),
        out_shape=jax.ShapeDtypeStruct((m, n), out_dtype),
        compiler_params=_cp("parallel", "parallel"),
        name="mm_glu",
    )(x, w, w, b, b)


def _mm_hgrn_kernel(x_ref, w_ref, lb_ref, o_ref, *, nsec):
    j = pl.program_id(0)
    acc = jnp.dot(x_ref[...], w_ref[...], preferred_element_type=F32)

    @pl.when(j < nsec)
    def _():
        o_ref[...] = _silu(acc)

    @pl.when((j >= nsec) & (j < 2 * nsec))
    def _():
        lb = lb_ref[...]
        o_ref[...] = jnp.log(lb + (1.0 - lb) * _sigmoid(acc))

    @pl.when((j >= 2 * nsec) & (j < 3 * nsec))
    def _():
        o_ref[...] = acc

    @pl.when(j >= 3 * nsec)
    def _():
        o_ref[...] = _silu(acc)


def mm_hgrn(x, w, lb_row):
    m, k = x.shape
    n = w.shape[1]
    nsec = (n // 4) // TN_MM
    return pl.pallas_call(
        functools.partial(_mm_hgrn_kernel, nsec=nsec),
        grid=(n // TN_MM, m // TM_MM),
        in_specs=[
            pl.BlockSpec((TM_MM, k), lambda j, i: (i, 0)),
            pl.BlockSpec((k, TN_MM), lambda j, i: (0, j)),
            pl.BlockSpec((1, TN_MM), lambda j, i: (0, j)),
        ],
        out_specs=pl.BlockSpec((TM_MM, TN_MM), lambda j, i: (i, j)),
        out_shape=jax.ShapeDtypeStruct((m, n), F32),
        compiler_params=_cp("parallel", "parallel"),
        name="mm_hgrn",
    )(x, w, lb_row)


def _swap_half_heads(x, lane):
    return jnp.where((lane % HD) < HD // 2, pltpu.roll(x, LANES - HD // 2, 1),
                     pltpu.roll(x, HD // 2, 1))


def _mm_rope_kernel(x_ref, w_ref, cos_ref, sin_ref, o_ref, *, n_rope, scale):
    acc = jnp.dot(x_ref[...], w_ref[...], preferred_element_type=F32)
    cos = cos_ref[...]
    sin = sin_ref[...]
    lane = lax.broadcasted_iota(jnp.int32, cos.shape, 1)
    for c in range(acc.shape[1] // LANES):
        xc = acc[:, c * LANES:(c + 1) * LANES]
        if c < n_rope:
            xc = (xc * cos + _swap_half_heads(xc, lane) * sin) * scale
        o_ref[:, c * LANES:(c + 1) * LANES] = xc


def mm_rope(x, w, cos, sin, n_rope, scale):
    m, k = x.shape
    n = w.shape[1]
    return pl.pallas_call(
        functools.partial(_mm_rope_kernel, n_rope=n_rope, scale=scale),
        grid=(n // TN_MM, m // TM_MM),
        in_specs=[
            pl.BlockSpec((TM_MM, k), lambda j, i: (i, 0)),
            pl.BlockSpec((k, TN_MM), lambda j, i: (0, j)),
            pl.BlockSpec((TM_MM, LANES), lambda j, i: (i, 0)),
            pl.BlockSpec((TM_MM, LANES), lambda j, i: (i, 0)),
        ],
        out_specs=pl.BlockSpec((TM_MM, TN_MM), lambda j, i: (i, j)),
        out_shape=jax.ShapeDtypeStruct((m, n), F32),
        compiler_params=_cp("parallel", "parallel"),
        name="mm_rope",
    )(x, w, cos, sin)


def _mm_res_ln_kernel(x_ref, w_ref, b_ref, res_ref, g_ref, beta_ref, of_ref, ob_ref):
    y = jnp.dot(x_ref[...], w_ref[...], preferred_element_type=F32) + b_ref[...]
    out = _ln_rows(ALPHA * res_ref[...] + y, g_ref[...], beta_ref[...])
    of_ref[...] = out
    ob_ref[...] = out.astype(BF16)


def mm_res_ln(x, w, b, res, g, beta):
    m, k = x.shape
    row = lambda i: (i, 0)
    fix = lambda i: (0, 0)
    return pl.pallas_call(
        _mm_res_ln_kernel,
        grid=(m // TM_LN,),
        in_specs=[
            pl.BlockSpec((TM_LN, k), row),
            pl.BlockSpec((k, D), fix),
            pl.BlockSpec((1, D), fix),
            pl.BlockSpec((TM_LN, D), row),
            pl.BlockSpec((1, D), fix),
            pl.BlockSpec((1, D), fix),
        ],
        out_specs=[pl.BlockSpec((TM_LN, D), row), pl.BlockSpec((TM_LN, D), row)],
        out_shape=[jax.ShapeDtypeStruct((m, D), F32), jax.ShapeDtypeStruct((m, D), BF16)],
        compiler_params=_cp("parallel"),
        name="mm_res_ln",
    )(x, w, b, res, g, beta)


def _res_ln_kernel(res_ref, y_ref, g_ref, beta_ref, of_ref, ob_ref):
    out = _ln_rows(ALPHA * res_ref[...] + y_ref[...].astype(F32), g_ref[...], beta_ref[...])
    of_ref[...] = out
    ob_ref[...] = out.astype(BF16)


def res_ln(res, y, g, beta):
    m = res.shape[0]
    row = lambda i: (i, 0)
    fix = lambda i: (0, 0)
    return pl.pallas_call(
        _res_ln_kernel,
        grid=(m // TM_LN,),
        in_specs=[pl.BlockSpec((TM_LN, D), row), pl.BlockSpec((TM_LN, D), row),
                  pl.BlockSpec((1, D), fix), pl.BlockSpec((1, D), fix)],
        out_specs=[pl.BlockSpec((TM_LN, D), row), pl.BlockSpec((TM_LN, D), row)],
        out_shape=[jax.ShapeDtypeStruct((m, D), F32), jax.ShapeDtypeStruct((m, D), BF16)],
        compiler_params=_cp("parallel"),
        name="res_ln",
    )(res, y, g, beta)


HALO = 32
CONV_RC = 64
CONV_LC = 512


def _conv_kernel(u_ref, buf_ref, wdw_ref, bdw_ref, g_ref, b_ref, z_ref, ext_ref, c_ref, *, tt):
    i = pl.program_id(1)

    @pl.when(i == 0)
    def _():
        ext_ref[0:HALO, :] = buf_ref[0]

    @pl.when(i > 0)
    def _():
        ext_ref[0:HALO, :] = ext_ref[tt:tt + HALO, :]

    ext_ref[HALO:HALO + tt, :] = u_ref[...]
    pad = HALO - (CONV_WIDTH - 1)
    rc = min(CONV_RC, tt)
    for r0 in range(0, tt, rc):
        for c0 in range(0, D, CONV_LC):
            acc = jnp.zeros((rc, CONV_LC), F32) + bdw_ref[:, c0:c0 + CONV_LC]
            for w in range(CONV_WIDTH):
                acc = acc + (ext_ref[r0 + pad + w:r0 + pad + w + rc, c0:c0 + CONV_LC]
                             * wdw_ref[w:w + 1, c0:c0 + CONV_LC])
            c_ref[r0:r0 + rc, c0:c0 + CONV_LC] = acc
    z_ref[...] = _silu(_ln_rows(c_ref[...], g_ref[...], b_ref[...])).astype(z_ref.dtype)


def conv_ln_silu(u, row0_blocks, nseq, t, tt, buf, wdw, bdw, g, b, out_dtype):
    nt = t // tt
    fix = lambda s, i: (0, 0)
    return pl.pallas_call(
        functools.partial(_conv_kernel, tt=tt),
        grid=(nseq, nt),
        in_specs=[
            pl.BlockSpec((tt, D), lambda s, i: (row0_blocks + s * nt + i, 0)),
            pl.BlockSpec((1, HALO, D), lambda s, i: (s, 0, 0)),
            pl.BlockSpec((CONV_WIDTH, D), fix),
            pl.BlockSpec((1, D), fix),
            pl.BlockSpec((1, D), fix),
            pl.BlockSpec((1, D), fix),
        ],
        out_specs=pl.BlockSpec((tt, D), lambda s, i: (s * nt + i, 0)),
        out_shape=jax.ShapeDtypeStruct((nseq * t, D), out_dtype),
        scratch_shapes=[pltpu.VMEM((HALO + tt, D), F32), pltpu.VMEM((tt, D), F32)],
        compiler_params=_cp("parallel", "arbitrary"),
        name="conv_ln_silu",
    )(u, buf, wdw, bdw, g, b)


def _hgrn_kernel(p_ref, s0_ref, ng_ref, o_ref, sf_ref, st_ref, pst_ref, ob_ref, *, tt, sc, nvalid):
    i = pl.program_id(1)
    hk = HG_H * HG_DK

    @pl.when(i == 0)
    def _():
        for h in range(HG_H):
            st_ref[h * HG_DV:(h + 1) * HG_DV, :] = s0_ref[0, h].T

    ones = jnp.ones((LANES, LANES), BF16)
    row = lax.broadcasted_iota(jnp.int32, (sc, hk), 0)

    def sub_chunk(ci, carry):
        r0 = pl.multiple_of(ci * sc, sc)
        q = p_ref[pl.ds(r0, sc), 0:hk]
        lf = p_ref[pl.ds(r0, sc), hk:2 * hk]
        v = p_ref[pl.ds(r0, sc), 2 * hk:3 * hk]
        if nvalid < sc:
            lf = jnp.where(row < nvalid, lf, 0.0)
        kk = 1.0 - jnp.exp(lf)
        bl = jnp.zeros((sc, hk), F32)
        for s in range(nvalid):
            bl = bl + jnp.where(row >= s, lf[s:s + 1, :], 0.0)
        btot = bl[sc - 1:sc, :]
        qd = q * jnp.exp(bl)
        kd = kk * jnp.exp(btot - bl)
        for s in range(nvalid):
            e = jnp.exp(jnp.where(row >= s, bl - bl[s:s + 1, :], -jnp.inf))
            pst_ref[s * sc:(s + 1) * sc, :] = q * e * kk[s:s + 1, :]
        o_acc = []
        for h in range(HG_H):
            sl = slice(h * HG_DK, (h + 1) * HG_DK)
            ph = pst_ref[0:nvalid * sc, sl]
            p_hi = ph.astype(BF16)
            p_lo = (ph - p_hi.astype(F32)).astype(BF16)
            rs = (jnp.dot(p_hi, ones, preferred_element_type=F32)
                  + jnp.dot(p_lo, ones, preferred_element_type=F32))
            vh = v[:, sl]
            st_h = st_ref[h * HG_DV:(h + 1) * HG_DV, :]
            oh = lax.dot_general(qd[:, sl].astype(BF16), st_h.astype(BF16),
                                 (((1,), (1,)), ((), ())), preferred_element_type=F32)
            for s in range(nvalid):
                oh = oh + rs[s * sc:(s + 1) * sc, :] * vh[s:s + 1, :]
            o_acc.append(oh)
            upd = jnp.dot(vh.T.astype(BF16), kd[:, sl].astype(BF16), preferred_element_type=F32)
            st_ref[h * HG_DV:(h + 1) * HG_DV, :] = jnp.exp(btot[:, sl]) * st_h + upd
        ob_ref[pl.ds(r0, sc), :] = jnp.concatenate(o_acc, axis=1)
        return carry

    lax.fori_loop(0, tt // sc, sub_chunk, 0)

    o = ob_ref[...]
    o2 = o * o
    parts = []
    for h in range(HG_H):
        sl = slice(h * HG_DV, (h + 1) * HG_DV)
        x2 = o2[:, sl]
        x_hi = x2.astype(BF16)
        x_lo = (x2 - x_hi.astype(F32)).astype(BF16)
        ms = (jnp.dot(x_hi, ones, preferred_element_type=F32)
              + jnp.dot(x_lo, ones, preferred_element_type=F32)) * (1.0 / HG_DV)
        parts.append(o[:, sl] * lax.rsqrt(ms + RMS_EPS))
    on = jnp.concatenate(parts, axis=1) * ng_ref[...] * p_ref[:, 3 * hk:4 * hk]
    o_ref[...] = on.astype(o_ref.dtype)

    @pl.when(i == pl.num_programs(1) - 1)
    def _():
        for h in range(HG_H):
            sf_ref[0, h] = st_ref[h * HG_DV:(h + 1) * HG_DV, :].T


def hgrn_scan(p, row0_blocks, nseq, t, tt, sc, nvalid, s0, norm_g, out_dtype):
    nt = t // tt
    return pl.pallas_call(
        functools.partial(_hgrn_kernel, tt=tt, sc=sc, nvalid=nvalid),
        grid=(nseq, nt),
        in_specs=[
            pl.BlockSpec((tt, 4 * D), lambda s, i: (row0_blocks + s * nt + i, 0)),
            pl.BlockSpec((1, HG_H, HG_DK, HG_DV), lambda s, i: (s, 0, 0, 0)),
            pl.BlockSpec((1, D), lambda s, i: (0, 0)),
        ],
        out_specs=[
            pl.BlockSpec((tt, D), lambda s, i: (s * nt + i, 0)),
            pl.BlockSpec((1, HG_H, HG_DK, HG_DV), lambda s, i: (s, 0, 0, 0)),
        ],
        out_shape=[jax.ShapeDtypeStruct((nseq * t, D), out_dtype),
                   jax.ShapeDtypeStruct((nseq, HG_H, HG_DK, HG_DV), F32)],
        scratch_shapes=[pltpu.VMEM((HG_H * HG_DV, HG_DK), F32),
                        pltpu.VMEM((sc * sc, D), F32),
                        pltpu.VMEM((tt, D), F32)],
        compiler_params=_cp("parallel", "arbitrary"),
        name="hgrn_scan",
    )(p, s0, norm_g)


S5_GPB = LANES // SSM_GROUP
S5_NJ = D // LANES
S5_SW = S5_GPB * SSM_STATE
S5_LC = 8


def _s5_kernel(u_ref, h0_ref, t_ref, bp_ref, cp_ref, ar_ref, ai_ref, y_ref, hf_ref,
               con_ref, hp_ref, *, nc):
    sw = S5_SW
    u = u_ref[0]
    con_ref[...] = jnp.dot(u, bp_ref[0], preferred_element_type=F32)
    ar = ar_ref[0]
    ai = ai_ref[0]

    def step(hr, hi, cr, ci):
        return ar * hr - ai * hi + cr, ar * hi + ai * hr + ci

    if nc == 1:
        hr, hi = h0_ref[0, :, 0:sw], h0_ref[0, :, sw:2 * sw]
        hp_ref[:, 0:sw] = hr
        hp_ref[:, sw:2 * sw] = hi
        hr, hi = step(hr, hi, con_ref[:, 0:sw], con_ref[:, sw:2 * sw])
    else:
        half = 4
        top = lax.broadcasted_iota(jnp.int32, (2 * half, sw), 0) < half

        def pair(k, carry):
            hr, hi = carry
            r0 = pl.multiple_of(k * 2 * half, 2 * half)
            cr = con_ref[pl.ds(r0, 2 * half), 0:sw]
            ci = con_ref[pl.ds(r0, 2 * half), sw:2 * sw]
            ar1, ai1 = step(hr, hi, cr, ci)
            ar1 = jnp.where(top, ar1, pltpu.roll(ar1, half, 0))
            ai1 = jnp.where(top, ai1, pltpu.roll(ai1, half, 0))
            hp_ref[pl.ds(r0, 2 * half), 0:sw] = jnp.where(top, hr, ar1)
            hp_ref[pl.ds(r0, 2 * half), sw:2 * sw] = jnp.where(top, hi, ai1)
            br, bi = step(ar1, ai1, cr, ci)
            br = jnp.where(top, pltpu.roll(br, half, 0), br)
            bi = jnp.where(top, pltpu.roll(bi, half, 0), bi)
            return br, bi

        hr, hi = lax.fori_loop(0, nc // 2, pair, (h0_ref[0, :, 0:sw], h0_ref[0, :, sw:2 * sw]))
    hf_ref[0, :, 0:sw] = hr
    hf_ref[0, :, sw:2 * sw] = hi
    y = (jnp.dot(u, t_ref[0], preferred_element_type=F32)
         + jnp.dot(hp_ref[...].astype(BF16), cp_ref[0], preferred_element_type=F32))
    y_ref[0] = jax.nn.gelu(y).astype(y_ref.dtype)


def s5_lti(u, h0, tmat, bpow, cpow, ar, ai, nc):
    nj, rows, lw = u.shape
    hrows = h0.shape[1]
    blk = lambda j: (j, 0, 0)
    return pl.pallas_call(
        functools.partial(_s5_kernel, nc=nc),
        grid=(nj,),
        in_specs=[
            pl.BlockSpec((1, rows, lw), blk),
            pl.BlockSpec((1, hrows, 2 * S5_SW), blk),
            pl.BlockSpec((1, lw, lw), blk),
            pl.BlockSpec((1, lw, 2 * S5_SW), blk),
            pl.BlockSpec((1, 2 * S5_SW, lw), blk),
            pl.BlockSpec((1, 1, S5_SW), blk),
            pl.BlockSpec((1, 1, S5_SW), blk),
        ],
        out_specs=[pl.BlockSpec((1, rows, lw), blk), pl.BlockSpec((1, hrows, 2 * S5_SW), blk)],
        out_shape=[jax.ShapeDtypeStruct((nj, rows, lw), BF16),
                   jax.ShapeDtypeStruct((nj, hrows, 2 * S5_SW), F32)],
        scratch_shapes=[pltpu.VMEM((rows, 2 * S5_SW), F32), pltpu.VMEM((rows, 2 * S5_SW), F32)],
        compiler_params=_cp("parallel"),
        name="s5_lti",
    )(u, h0, tmat, bpow, cpow, ar, ai)


def _s5_operators(lam_re, lam_im, log_dt, b_re, b_im, c_re, c_im, d_skip, nvs):
    hp = lax.Precision.HIGHEST
    gpb, nj, c, p, lc = S5_GPB, S5_NJ, SSM_GROUP, SSM_STATE, S5_LC
    dt = jnp.exp(log_dt)[:, None]
    lr, li = lam_re, lam_im
    mag = jnp.exp(lr * dt)
    ab_re, ab_im = mag * jnp.cos(li * dt), mag * jnp.sin(li * dt)
    nr, ni = ab_re - 1.0, ab_im
    den = lr * lr + li * li
    z_re, z_im = (nr * lr + ni * li) / den, (ni * lr - nr * li) / den
    bb_re = z_re[..., None] * b_re - z_im[..., None] * b_im
    bb_im = z_re[..., None] * b_im + z_im[..., None] * b_re
    l = jnp.arange(lc + 1, dtype=F32)[None, :, None]
    pmag = jnp.exp(l * (lr * dt)[:, None, :])
    pw_re = pmag * jnp.cos(l * (li * dt)[:, None, :])
    pw_im = pmag * jnp.sin(l * (li * dt)[:, None, :])
    w_re = pw_re[..., None] * bb_re[:, None] - pw_im[..., None] * bb_im[:, None]
    w_im = pw_re[..., None] * bb_im[:, None] + pw_im[..., None] * bb_re[:, None]
    kl = (jnp.einsum('gcp,glpd->glcd', c_re, w_re[:, :lc], precision=hp)
          - jnp.einsum('gcp,glpd->glcd', c_im, w_im[:, :lc], precision=hp))
    kl = kl.at[:, 0].add(d_skip.reshape(SSM_GROUPS, c)[:, :, None] * jnp.eye(c, dtype=F32)[None])
    eye = jnp.eye(gpb, dtype=F32)
    s_idx = jnp.arange(lc)[:, None]
    t_idx = jnp.arange(lc)[None, :]
    lag = jnp.clip(t_idx - s_idx, 0, lc - 1)
    tm = jnp.where((t_idx >= s_idx)[None, :, :, None, None], kl[:, lag], 0.0)
    tm = tm.reshape(nj, gpb, lc, lc, c, c)
    tmat = jnp.einsum('jgstoc,gh->jsgctho', tm, eye).reshape(nj, lc * LANES, lc * LANES)
    ca_re = c_re[:, None] * pw_re[:, 1:, None, :] - c_im[:, None] * pw_im[:, 1:, None, :]
    ca_im = c_re[:, None] * pw_im[:, 1:, None, :] + c_im[:, None] * pw_re[:, 1:, None, :]

    def state_rows(a):
        a = a.reshape(nj, gpb, lc, c, p)
        return jnp.einsum('jgtop,gh->jgptho', a, eye).reshape(nj, gpb * p, lc * LANES)

    cpow = jnp.concatenate([state_rows(ca_re), state_rows(-ca_im)], axis=1)
    s_all = jnp.arange(lc)

    def state_cols(a, nv):
        e_idx = jnp.clip(nv - 1 - s_all, 0, lc)
        valid = (s_all < nv)[None, :, None, None]
        a = jnp.where(valid, a[:, e_idx], 0.0).reshape(nj, gpb, lc, p, c)
        return jnp.einsum('jgspc,gh->jsgchp', a, eye).reshape(nj, lc * LANES, gpb * p)

    state_ops = []
    for nv in nvs:
        bpow = jnp.concatenate([state_cols(w_re, nv), state_cols(w_im, nv)], axis=-1)
        ar = pw_re[:, nv].reshape(nj, 1, gpb * p)
        ai = pw_im[:, nv].reshape(nj, 1, gpb * p)
        state_ops.append((bpow.astype(BF16), ar, ai))
    return tmat.astype(BF16), cpow.astype(BF16), state_ops


def _attn_kernel(q_ref, kv_ref, pk_ref, pv_ref, sink_ref, o_ref, *, tq, nblk, always_prev):
    idx = pl.program_id(0)
    kvw = NKV * HD
    group = NH // NKV
    pkn = pk_ref.shape[1]
    to = kv_ref.shape[1]
    nk = pkn + to
    rows = group * tq
    a = lax.broadcasted_iota(jnp.int32, (rows, nk), 0) % tq
    col = lax.broadcasted_iota(jnp.int32, (rows, nk), 1)
    prev_ok = col >= a
    if not always_prev:
        prev_ok = prev_ok & ((idx % nblk) > 0)
    mask = ((col < pkn) & prev_ok) | ((col >= pkn) & ((col - pkn) <= a))
    hrow = lax.broadcasted_iota(jnp.int32, (rows, 1), 0) // tq
    low = lax.broadcasted_iota(jnp.int32, (tq, LANES), 1) < HD
    nt = (((1,), (1,)), ((), ()))
    for kb in range(NKV // 2):
        lanes = slice(kb * LANES, (kb + 1) * LANES)
        kcat = jnp.concatenate([pk_ref[0, :, lanes], kv_ref[0, :, lanes]], axis=0).astype(BF16)
        vcat = jnp.concatenate([pv_ref[0, :, lanes],
                                kv_ref[0, :, kvw + kb * LANES:kvw + (kb + 1) * LANES]], axis=0).astype(BF16)
        for hi in range(2):
            j = 2 * kb + hi
            parts = []
            sk = jnp.zeros((rows, 1), F32)
            for h8 in range(group):
                qb, e = (group * j + h8) // 2, h8 % 2
                q2 = q_ref[0, :, qb * LANES:(qb + 1) * LANES]
                qe = jnp.where(low if e == 0 else jnp.logical_not(low), q2, 0.0)
                if e != hi:
                    qe = pltpu.roll(qe, HD, 1)
                parts.append(qe)
                sk = jnp.where(hrow == h8, sink_ref[group * j + h8], sk)
            q8 = jnp.concatenate(parts, axis=0).astype(BF16)
            s = lax.dot_general(q8, kcat, nt, preferred_element_type=F32)
            s = jnp.where(mask, s, -jnp.inf)
            m = jnp.maximum(jnp.max(s, -1, keepdims=True), sk)
            p = jnp.exp(s - m)
            inv = 1.0 / (jnp.sum(p, -1, keepdims=True) + jnp.exp(sk - m))
            o8 = jnp.dot((p * inv).astype(BF16), vcat, preferred_element_type=F32)
            for pr in range(group // 2):
                qb = (group * j) // 2 + pr
                pair = []
                for e in range(2):
                    oe = o8[(2 * pr + e) * tq:(2 * pr + e + 1) * tq, :]
                    if e != hi:
                        oe = pltpu.roll(oe, HD, 1)
                    pair.append(oe)
                o_ref[0, :, qb * LANES:(qb + 1) * LANES] = jnp.where(low, pair[0], pair[1]).astype(o_ref.dtype)


def swa_attention(q3, kv3, pk3, pv3, pk_map, pv_map, sinks, nblk, always_prev):
    nb, tq, _ = q3.shape
    to = kv3.shape[1]
    kvw = NKV * HD
    return pl.pallas_call(
        functools.partial(_attn_kernel, tq=tq, nblk=nblk, always_prev=always_prev),
        grid=(nb,),
        in_specs=[
            pl.BlockSpec((1, tq, D), lambda i: (i, 0, 0)),
            pl.BlockSpec((1, to, 2 * kvw), lambda i: (i, 0, 0)),
            pl.BlockSpec((1, pk3.shape[1], kvw), pk_map),
            pl.BlockSpec((1, pv3.shape[1], kvw), pv_map),
            pl.BlockSpec(memory_space=pltpu.SMEM),
        ],
        out_specs=pl.BlockSpec((1, tq, D), lambda i: (i, 0, 0)),
        out_shape=jax.ShapeDtypeStruct((nb, tq, D), BF16),
        compiler_params=_cp("parallel"),
        name="swa_attention",
    )(q3, kv3, pk3, pv3, sinks)


R_E1, R_E2, R_W1, R_W2, R_R1, R_R2 = 0, 1, 2, 3, 4, 5


def _router_kernel(x_ref, w_ref, b_ref, o_ref, cnt_ref, carry_ref):
    i = pl.program_id(0)

    @pl.when(i == 0)
    def _():
        carry_ref[...] = jnp.zeros(carry_ref.shape, F32)

    l = jnp.dot(x_ref[...], w_ref[...], preferred_element_type=F32,
                precision=lax.Precision.HIGHEST) + b_ref[...]
    tm = l.shape[0]
    lane = lax.broadcasted_iota(jnp.int32, l.shape, 1).astype(F32)
    big = 1e9
    ninf = -jnp.inf
    gmask = lane < 4.0
    gl = jnp.where(gmask, l, ninf)
    gm = jnp.max(gl, -1, keepdims=True)
    gsel = jnp.min(jnp.where(gl == gm, lane, big), -1, keepdims=True)
    gsum = jnp.sum(jnp.where(gmask, jnp.exp(gl - gm), 0.0), -1, keepdims=True)
    gval = 1.0 / gsum
    lo = 4.0 + 4.0 * gsel
    emask = (lane >= lo) & (lane < lo + 4.0)
    el = jnp.where(emask, l, ninf)
    m1 = jnp.max(el, -1, keepdims=True)
    i1 = jnp.min(jnp.where(emask & (el == m1), lane, big), -1, keepdims=True)
    emask2 = emask & (lane != i1)
    el2 = jnp.where(emask2, l, ninf)
    m2 = jnp.max(el2, -1, keepdims=True)
    i2 = jnp.min(jnp.where(emask2 & (el2 == m2), lane, big), -1, keepdims=True)
    r = jnp.exp(m2 - m1)
    w1 = gval / (1.0 + r)
    w2 = gval * r / (1.0 + r)
    e1 = i1 - 4.0
    e2 = i2 - 4.0
    oh = ((lane == e1) | (lane == e2)).astype(F32)
    tri = (lax.broadcasted_iota(jnp.int32, (tm, tm), 1)
           < lax.broadcasted_iota(jnp.int32, (tm, tm), 0)).astype(BF16)
    before = jnp.dot(tri, oh.astype(BF16), preferred_element_type=F32) + carry_ref[...]
    r1 = jnp.sum(jnp.where(lane == e1, before, 0.0), -1, keepdims=True)
    r2 = jnp.sum(jnp.where(lane == e2, before, 0.0), -1, keepdims=True)
    carry_ref[...] = carry_ref[...] + jnp.sum(oh, axis=0, keepdims=True)
    cnt_ref[...] = carry_ref[...]
    rec = jnp.zeros(l.shape, F32)
    for ln, val in ((R_E1, e1), (R_E2, e2), (R_W1, w1), (R_W2, w2), (R_R1, r1), (R_R2, r2)):
        rec = jnp.where(lane == float(ln), val, rec)
    o_ref[...] = rec


def moe_router(x, w, b):
    m = x.shape[0]
    return pl.pallas_call(
        _router_kernel,
        grid=(m // TM_LN,),
        in_specs=[pl.BlockSpec((TM_LN, D), lambda i: (i, 0)),
                  pl.BlockSpec((D, LANES), lambda i: (0, 0)),
                  pl.BlockSpec((1, LANES), lambda i: (0, 0))],
        out_specs=[pl.BlockSpec((TM_LN, LANES), lambda i: (i, 0)),
                   pl.BlockSpec((1, LANES), lambda i: (0, 0))],
        out_shape=[jax.ShapeDtypeStruct((m, LANES), F32), jax.ShapeDtypeStruct((1, LANES), F32)],
        scratch_shapes=[pltpu.VMEM((1, LANES), F32)],
        compiler_params=_cp("arbitrary"),
        name="moe_router",
    )(x, w, b)


def _ffn_kernel(te_ref, nt_ref, xs_ref, w1_ref, w3_ref, w2_ref, y_ref, w1b, w3b, w2b):
    i = pl.program_id(0)

    @pl.when(i < nt_ref[0])
    def _():
        prev = te_ref[jnp.maximum(i - 1, 0)]

        @pl.when((i == 0) | (te_ref[i] != prev))
        def _():
            w1b[...] = w1_ref[0, 0].astype(BF16)
            w3b[...] = w3_ref[0, 0].astype(BF16)
            w2b[...] = w2_ref[0, 0].astype(BF16)

        x = xs_ref[...].astype(BF16)
        h1 = jnp.dot(x, w1b[...], preferred_element_type=F32)
        h3 = jnp.dot(x, w3b[...], preferred_element_type=F32)
        h = _silu(h1) * h3
        y_ref[...] = jnp.dot(h.astype(BF16), w2b[...], preferred_element_type=F32)

    @pl.when(i >= nt_ref[0])
    def _():
        y_ref[...] = jnp.zeros(y_ref.shape, y_ref.dtype)


def moe_ffn(layer, tile_expert, n_tiles, xs, w1, w3, w2):
    wmap = lambda i, te, nt: (layer, te[i], 0, 0)
    grid_spec = pltpu.PrefetchScalarGridSpec(
        num_scalar_prefetch=2,
        grid=(N_ETILES,),
        in_specs=[
            pl.BlockSpec((TM_E, D), lambda i, te, nt: (i, 0)),
            pl.BlockSpec((1, 1, D, MOE_FF), wmap),
            pl.BlockSpec((1, 1, D, MOE_FF), wmap),
            pl.BlockSpec((1, 1, MOE_FF, D), wmap),
        ],
        out_specs=pl.BlockSpec((TM_E, D), lambda i, te, nt: (i, 0)),
        scratch_shapes=[pltpu.VMEM((D, MOE_FF), BF16), pltpu.VMEM((D, MOE_FF), BF16),
                        pltpu.VMEM((MOE_FF, D), BF16)],
    )
    return pl.pallas_call(
        _ffn_kernel,
        grid_spec=grid_spec,
        out_shape=jax.ShapeDtypeStruct((P_SLOTS, D), F32),
        compiler_params=_cp("arbitrary"),
        name="moe_ffn",
    )(tile_expert, n_tiles, xs, w1, w3, w2)


def _combine_ln_kernel(res_ref, y1_ref, y2_ref, rec_ref, g_ref, beta_ref, of_ref, ob_ref):
    rec = rec_ref[...]
    y = rec[:, R_W1:R_W1 + 1] * y1_ref[...] + rec[:, R_W2:R_W2 + 1] * y2_ref[...]
    out = _ln_rows(ALPHA * res_ref[...] + y, g_ref[...], beta_ref[...])
    of_ref[...] = out
    ob_ref[...] = out.astype(BF16)


def combine_ln(res, y1, y2, rec, g, beta):
    m = res.shape[0]
    row = lambda i: (i, 0)
    fix = lambda i: (0, 0)
    return pl.pallas_call(
        _combine_ln_kernel,
        grid=(m // TM_LN,),
        in_specs=[pl.BlockSpec((TM_LN, D), row), pl.BlockSpec((TM_LN, D), row),
                  pl.BlockSpec((TM_LN, D), row), pl.BlockSpec((TM_LN, LANES), row),
                  pl.BlockSpec((1, D), fix), pl.BlockSpec((1, D), fix)],
        out_specs=[pl.BlockSpec((TM_LN, D), row), pl.BlockSpec((TM_LN, D), row)],
        out_shape=[jax.ShapeDtypeStruct((m, D), F32), jax.ShapeDtypeStruct((m, D), BF16)],
        compiler_params=_cp("parallel"),
        name="combine_ln",
    )(res, y1, y2, rec, g, beta)


def _dispatch_plan(rec, counts_row):
    counts = counts_row[0, :MOE_E].astype(jnp.int32)
    padded = ((counts + TM_E - 1) // TM_E) * TM_E
    ends = jnp.cumsum(padded)
    starts = ends - padded
    ids = jnp.arange(MOE_E, dtype=jnp.int32)[None, :]
    e = rec[:, R_E1:R_E2 + 1].astype(jnp.int32)
    rank = rec[:, R_R1:R_R2 + 1].astype(jnp.int32)
    start_of = jnp.sum(jnp.where(e[:, :, None] == ids[None], starts[None, None, :], 0), axis=-1)
    pos = start_of + rank
    tok = jnp.broadcast_to(jnp.arange(MT, dtype=jnp.int32)[:, None], (MT, 2))
    src = jnp.zeros((P_SLOTS,), jnp.int32).at[pos.reshape(-1)].set(tok.reshape(-1))
    n_tiles = (ends[-1] // TM_E).astype(jnp.int32)
    tile_start = jnp.arange(N_ETILES, dtype=jnp.int32) * TM_E
    te = jnp.sum((tile_start[:, None] >= ends[None, :]).astype(jnp.int32), axis=1)
    last = jnp.sum(jnp.where(jnp.arange(N_ETILES) == n_tiles - 1, te, 0))
    te = jnp.minimum(jnp.where(jnp.arange(N_ETILES) < n_tiles, te, last), MOE_E - 1).astype(jnp.int32)
    return src, te, n_tiles.reshape(1), pos


def hier_moe_block(xf, i, moe_w_group, moe_b_group, moe_w_egate, moe_b_egate,
                   moe_w1, moe_w3, moe_w2, ln_g, ln_b):
    wr = jnp.concatenate([moe_w_group[i], moe_w_egate[i].transpose(1, 0, 2).reshape(D, MOE_E)], axis=1)
    wr = jnp.pad(wr, ((0, 0), (0, LANES - wr.shape[1])))
    br = jnp.concatenate([moe_b_group[i], moe_b_egate[i].reshape(MOE_E)])
    br = jnp.pad(br, (0, LANES - br.shape[0]))[None, :]
    rec, counts = moe_router(xf, wr, br)
    src, te, n_tiles, pos = _dispatch_plan(rec, counts)
    xs = jnp.take(xf, src, axis=0)
    ys = moe_ffn(i, te, n_tiles, xs, moe_w1, moe_w3, moe_w2)
    y1 = jnp.take(ys, pos[:, 0], axis=0)
    y2 = jnp.take(ys, pos[:, 1], axis=0)
    return combine_ln(xf, y1, y2, rec, ln_g[i][None, :], ln_b[i][None, :])


def _sample_rows(a):
    return a[NP:].reshape(DEC_BATCH, TS, a.shape[-1])


def conv_layer(xf, xb, state, w_in, b_in, w_dw, b_dw, g, b, w_out, b_out, ln_g, ln_b):
    u = mm_glu(xb, w_in.astype(BF16), b_in[None, :], F32)
    tt_p = 256
    zbuf = jnp.zeros((BATCH, HALO, D), F32)
    z_p = conv_ln_silu(u, 0, BATCH, SEQ, tt_p, zbuf, w_dw, b_dw[None, :], g[None, :], b[None, :], BF16)
    pad = HALO - (CONV_WIDTH - 1)
    sbuf = jnp.pad(state, ((0, 0), (pad, 0), (0, 0)))
    z_s = conv_ln_silu(u, NP // TS, DEC_BATCH, TS, TS, sbuf, w_dw, b_dw[None, :], g[None, :], b[None, :], F32)
    z = jnp.concatenate([z_p, z_s.astype(BF16)], axis=0)
    xf, xb = mm_res_ln(z, w_out.astype(BF16), b_out[None, :], xf, ln_g[None, :], ln_b[None, :])
    new_p = u[:NP].reshape(BATCH, SEQ, D)[:, SEQ - (CONV_WIDTH - 1):]
    new_s = jnp.concatenate([state[:, DEC_SEQ:], _sample_rows(u)[:, :DEC_SEQ]], axis=1)
    return xf, xb, new_p, new_s


def hgrn_layer(xf, xb, state, w_in, lb, norm_g, w_out, ln_g, ln_b):
    hk = HG_H * HG_DK
    lb_row = jnp.concatenate([jnp.zeros((hk,), F32), lb, jnp.zeros((2 * hk,), F32)])[None, :]
    p = mm_hgrn(xb, w_in.astype(BF16), lb_row)
    s0_p = jnp.zeros((BATCH, HG_H, HG_DK, HG_DV), F32)
    o_p, sf_p = hgrn_scan(p, 0, BATCH, SEQ, 128, 16, 16, s0_p, norm_g[None, :], BF16)
    o_s, sf_s = hgrn_scan(p, NP // TS, DEC_BATCH, TS, TS, TS, DEC_SEQ, state, norm_g[None, :], F32)
    o = jnp.concatenate([o_p, o_s.astype(BF16)], axis=0)
    zero_b = jnp.zeros((1, D), F32)
    xf, xb = mm_res_ln(o, w_out.astype(BF16), zero_b, xf, ln_g[None, :], ln_b[None, :])
    return xf, xb, sf_p, sf_s


def s5_layer(xf, xb, h_re, h_im, lam_re, lam_im, log_dt, b_re, b_im, c_re, c_im, d_skip, w_glu,
             ln_g, ln_b):
    nj, sw, lc = S5_NJ, S5_SW, S5_LC
    tmat, cpow, (ops_p, ops_s) = _s5_operators(lam_re, lam_im, log_dt, b_re, b_im, c_re, c_im, d_skip,
                                               (lc, DEC_SEQ))
    nc = SEQ // lc
    up = xb[:NP].reshape(BATCH, nc, lc, nj, LANES).transpose(3, 1, 0, 2, 4).reshape(nj, nc * BATCH, lc * LANES)
    y_p, hf_p = s5_lti(up, jnp.zeros((nj, 2 * BATCH, 2 * sw), F32), tmat, ops_p[0], cpow, ops_p[1], ops_p[2], nc)
    y_p = y_p.reshape(nj, nc, BATCH, lc, LANES).transpose(2, 1, 3, 0, 4).reshape(NP, D)
    hf_p = hf_p[:, :BATCH].transpose(1, 0, 2)
    us = _sample_rows(xb).reshape(DEC_BATCH, TS, nj, LANES).transpose(2, 0, 1, 3).reshape(nj, DEC_BATCH, TS * LANES)
    h0 = jnp.concatenate([h_re.reshape(DEC_BATCH, nj, sw), h_im.reshape(DEC_BATCH, nj, sw)], -1).transpose(1, 0, 2)
    y_s, hf_s = s5_lti(us, h0, tmat, ops_s[0], cpow, ops_s[1], ops_s[2], 1)
    y_s = y_s.reshape(nj, DEC_BATCH, TS, LANES).transpose(1, 2, 0, 3).reshape(NS, D)
    hf_s = hf_s.transpose(1, 0, 2)
    y = jnp.concatenate([y_p, y_s], axis=0)
    m = mm_glu(y, w_glu.astype(BF16), jnp.zeros((1, 2 * D), F32), BF16)
    xf, xb = res_ln(xf, m, ln_g[None, :], ln_b[None, :])
    st = lambda h, n: (h[..., :sw].reshape(n, SSM_GROUPS, SSM_STATE), h[..., sw:].reshape(n, SSM_GROUPS, SSM_STATE))
    return (xf, xb) + st(hf_p, BATCH) + st(hf_s, DEC_BATCH)


def _rope_tables(pos):
    half = HD // 2
    inv = ROPE_THETA ** (-jnp.arange(half, dtype=F32) / half)
    ang = pos.astype(F32)[:, None] * inv[None, :]
    cos, sin = jnp.cos(ang), jnp.sin(ang)
    cos_t = jnp.tile(jnp.concatenate([cos, cos], -1), (1, LANES // HD))
    sin_t = jnp.tile(jnp.concatenate([-sin, sin], -1), (1, LANES // HD))
    return cos_t, sin_t


def swa_layer(xf, xb, cache_k, cache_v, w_qkv, sinks, w_out, ln_g, ln_b):
    hq = NH * HD
    kvw = NKV * HD
    pos = jnp.concatenate([jnp.tile(jnp.arange(SEQ), BATCH),
                           jnp.tile(PAST_LEN + jnp.arange(TS), DEC_BATCH)])
    cos_t, sin_t = _rope_tables(pos)
    wb = w_qkv.astype(BF16)
    q = mm_rope(xb, wb[:, :hq], cos_t, sin_t, TN_MM // LANES, HD ** -0.5)
    kv = mm_rope(xb, wb[:, hq:], cos_t, sin_t, kvw // LANES, 1.0)
    nblk = SEQ // WINDOW
    q3 = q[:NP].reshape(BATCH * nblk, WINDOW, D)
    kv3 = kv[:NP].reshape(BATCH * nblk, WINDOW, 2 * kvw)
    prev = lambda i: (jnp.maximum(i - 1, 0), 0, 0)
    prev_v = lambda i: (jnp.maximum(i - 1, 0), 0, 1)
    o_p = swa_attention(q3, kv3, kv3, kv3, prev, prev_v, sinks, nblk, False).reshape(NP, D)
    qs = _sample_rows(q)
    kvs = _sample_rows(kv)
    kvs_pad = jnp.pad(kvs, ((0, 0), (0, WINDOW - TS), (0, 0)))
    ck = cache_k.reshape(DEC_BATCH, WINDOW, kvw)
    cv = cache_v.reshape(DEC_BATCH, WINDOW, kvw)
    same = lambda i: (i, 0, 0)
    o_s = swa_attention(qs, kvs_pad, ck, cv, same, same, sinks, 1, True).reshape(NS, D)
    o = jnp.concatenate([o_p, o_s], axis=0)
    xf, xb = mm_res_ln(o, w_out.astype(BF16), jnp.zeros((1, D), F32), xf, ln_g[None, :], ln_b[None, :])
    kvp = kv[:NP].reshape(BATCH, SEQ, 2 * kvw)[:, SEQ - WINDOW:]
    nk_p = kvp[..., :kvw].reshape(BATCH, WINDOW, NKV, HD)
    nv_p = kvp[..., kvw:].reshape(BATCH, WINDOW, NKV, HD)
    nk_s = jnp.concatenate([ck[:, DEC_SEQ:], kvs[:, :DEC_SEQ, :kvw]], axis=1).reshape(DEC_BATCH, WINDOW, NKV, HD)
    nv_s = jnp.concatenate([cv[:, DEC_SEQ:], kvs[:, :DEC_SEQ, kvw:]], axis=1).reshape(DEC_BATCH, WINDOW, NKV, HD)
    return xf, xb, nk_p, nv_p, nk_s, nv_s


def kernel(x_prompt, x_sample, state_conv, state_hgrn, state_ssm_re, state_ssm_im, cache_swa_k, cache_swa_v, conv_w_in, conv_b_in, conv_w_dw, conv_b_dw, conv_ln_g, conv_ln_b, conv_w_out, conv_b_out, hgrn_w_in, hgrn_lb_logits, hgrn_norm_g, hgrn_w_out, ssm_lam_re, ssm_lam_im, ssm_log_dt, ssm_b_re, ssm_b_im, ssm_c_re, ssm_c_im, ssm_d, ssm_w_glu, swa_w_qkv, swa_sinks, swa_w_out, ln1_g, ln1_b, ln2_g, ln2_b, moe_w_group, moe_b_group, moe_w_egate, moe_b_egate, moe_w1, moe_w3, moe_w2):
    lbs = jnp.cumsum(jax.nn.softmax(hgrn_lb_logits.astype(F32), axis=0), axis=0)
    lbs = lbs - lbs[0]
    xs_pad = jnp.pad(x_sample, ((0, 0), (0, TS - DEC_SEQ), (0, 0)))
    xf = jnp.concatenate([x_prompt.reshape(NP, D), xs_pad.reshape(NS, D)], axis=0)
    xb = xf.astype(BF16)
    moe = (moe_w_group, moe_b_group, moe_w_egate, moe_b_egate, moe_w1, moe_w3, moe_w2, ln2_g, ln2_b)

    xf, xb, conv_p, conv_s = conv_layer(
        xf, xb, state_conv[0], conv_w_in[0], conv_b_in[0], conv_w_dw[0], conv_b_dw[0],
        conv_ln_g[0], conv_ln_b[0], conv_w_out[0], conv_b_out[0], ln1_g[0], ln1_b[0])
    xf, xb = hier_moe_block(xf, 0, *moe)

    xf, xb, hg_p, hg_s = hgrn_layer(xf, xb, state_hgrn[0], hgrn_w_in[0], lbs[1], hgrn_norm_g[0],
                                    hgrn_w_out[0], ln1_g[1], ln1_b[1])
    xf, xb = hier_moe_block(xf, 1, *moe)

    xf, xb, hr_p, hi_p, hr_s, hi_s = s5_layer(
        xf, xb, state_ssm_re[0], state_ssm_im[0], ssm_lam_re[0], ssm_lam_im[0], ssm_log_dt[0],
        ssm_b_re[0], ssm_b_im[0], ssm_c_re[0], ssm_c_im[0], ssm_d[0], ssm_w_glu[0], ln1_g[2], ln1_b[2])
    xf, xb = hier_moe_block(xf, 2, *moe)

    xf, xb, k_p, v_p, k_s, v_s = swa_layer(xf, xb, cache_swa_k[0], cache_swa_v[0], swa_w_qkv[0],
                                            swa_sinks[0], swa_w_out[0], ln1_g[3], ln1_b[3])
    xf, xb = hier_moe_block(xf, 3, *moe)

    y_p = xf[:NP].reshape(BATCH, SEQ, D)
    y_s = _sample_rows(xf)[:, :DEC_SEQ]
    return (y_p, y_s, conv_p[None], conv_s[None], hg_p[None], hg_s[None],
            hr_p[None], hi_p[None], hr_s[None], hi_s[None],
            k_p[None], v_p[None], k_s[None], v_s[None])
```

```python
import functools

import jax
import jax.numpy as jnp
from jax import lax
from jax.experimental import pallas as pl
from jax.experimental.pallas import tpu as pltpu

F32 = jnp.float32
BF16 = jnp.bfloat16

D = 2048
BATCH = 4
SEQ = 2048
DEPTH = 4
DEC_BATCH = 32
DEC_SEQ = 4
PAST_LEN = 16384
CONV_WIDTH = 31
HG_H = 16
HG_DK = 128
HG_DV = 128
SSM_GROUP = 16
SSM_GROUPS = 128
SSM_STATE = 64
HD = 64
NH = 32
NKV = 4
WINDOW = 128
ROPE_THETA = 10000.0
MOE_E = 16
MOE_FF = 512
ALPHA = (2.0 * DEPTH) ** 0.25
LN_EPS = 1e-5
RMS_EPS = 1e-6

TS = 8
NP = BATCH * SEQ
NS = DEC_BATCH * TS
MT = NP + NS
LANES = 128
VMEM_LIMIT = 56 * 1024 * 1024

TM_MM = 768
TN_MM = 512
TM_LN = 384
TM_E = 512
P_SLOTS = ((2 * MT + MOE_E * (TM_E - 1) + TM_E - 1) // TM_E) * TM_E
N_ETILES = P_SLOTS // TM_E
DMA_UNROLL = 8


def _cp(*sem):
    return pltpu.CompilerParams(dimension_semantics=sem, vmem_limit_bytes=VMEM_LIMIT)


def _sigmoid(x):
    return 1.0 / (1.0 + jnp.exp(-x))


def _silu(x):
    return x * _sigmoid(x)


def _ln_rows(z, g, b):
    mu = jnp.mean(z, -1, keepdims=True)
    zc = z - mu
    var = jnp.mean(zc * zc, -1, keepdims=True)
    return zc * lax.rsqrt(var + LN_EPS) * g + b


def _cast_weights_once(pairs):
    @pl.when(pl.program_id(1) == 0)
    def _():
        for src, dst in pairs:
            dst[...] = src[...].astype(BF16)


def _mm_glu_kernel(x_ref, wa_ref, wg_ref, ba_ref, bg_ref, o_ref, wab, wgb):
    _cast_weights_once(((wa_ref, wab), (wg_ref, wgb)))
    x = x_ref[...]
    a = jnp.dot(x, wab[...], preferred_element_type=F32) + ba_ref[...]
    g = jnp.dot(x, wgb[...], preferred_element_type=F32) + bg_ref[...]
    o_ref[...] = (a * _sigmoid(g)).astype(o_ref.dtype)


def mm_glu(x, w, b, out_dtype):
    m, k = x.shape
    n = w.shape[1] // 2
    nj = n // TN_MM
    return pl.pallas_call(
        _mm_glu_kernel,
        grid=(nj, m // TM_MM),
        in_specs=[
            pl.BlockSpec((TM_MM, k), lambda j, i: (i, 0)),
            pl.BlockSpec((k, TN_MM), lambda j, i: (0, j)),
            pl.BlockSpec((k, TN_MM), lambda j, i: (0, j + nj)),
            pl.BlockSpec((1, TN_MM), lambda j, i: (0, j)),
            pl.BlockSpec((1, TN_MM), lambda j, i: (0, j + nj)),
        ],
        out_specs=pl.BlockSpec((TM_MM, TN_MM), lambda j, i: (i, j)),
        out_shape=jax.ShapeDtypeStruct((m, n), out_dtype),
        scratch_shapes=[pltpu.VMEM((k, TN_MM), BF16), pltpu.VMEM((k, TN_MM), BF16)],
        compiler_params=_cp("parallel", "arbitrary"),
        name="mm_glu",
    )(x, w, w, b, b)


def _mm_hgrn_kernel(x_ref, w_ref, lb_ref, o_ref, wb, *, nsec):
    j = pl.program_id(0)
    _cast_weights_once(((w_ref, wb),))
    acc = jnp.dot(x_ref[...], wb[...], preferred_element_type=F32)

    @pl.when(j < nsec)
    def _():
        o_ref[...] = _silu(acc)

    @pl.when((j >= nsec) & (j < 2 * nsec))
    def _():
        lb = lb_ref[...]
        o_ref[...] = jnp.log(lb + (1.0 - lb) * _sigmoid(acc))

    @pl.when((j >= 2 * nsec) & (j < 3 * nsec))
    def _():
        o_ref[...] = acc

    @pl.when(j >= 3 * nsec)
    def _():
        o_ref[...] = _silu(acc)


def mm_hgrn(x, w, lb_row):
    m, k = x.shape
    n = w.shape[1]
    nsec = (n // 4) // TN_MM
    return pl.pallas_call(
        functools.partial(_mm_hgrn_kernel, nsec=nsec),
        grid=(n // TN_MM, m // TM_MM),
        in_specs=[
            pl.BlockSpec((TM_MM, k), lambda j, i: (i, 0)),
            pl.BlockSpec((k, TN_MM), lambda j, i: (0, j)),
            pl.BlockSpec((1, TN_MM), lambda j, i: (0, j)),
        ],
        out_specs=pl.BlockSpec((TM_MM, TN_MM), lambda j, i: (i, j)),
        out_shape=jax.ShapeDtypeStruct((m, n), F32),
        scratch_shapes=[pltpu.VMEM((k, TN_MM), BF16)],
        compiler_params=_cp("parallel", "arbitrary"),
        name="mm_hgrn",
    )(x, w, lb_row)


def _swap_half_heads(x, lane):
    return jnp.where((lane % HD) < HD // 2, pltpu.roll(x, LANES - HD // 2, 1),
                     pltpu.roll(x, HD // 2, 1))


def _mm_rope_kernel(x_ref, w_ref, cos_ref, sin_ref, o_ref, wb, *, n_rope, scale):
    _cast_weights_once(((w_ref, wb),))
    acc = jnp.dot(x_ref[...], wb[...], preferred_element_type=F32)
    cos = cos_ref[...]
    sin = sin_ref[...]
    lane = lax.broadcasted_iota(jnp.int32, cos.shape, 1)
    for c in range(acc.shape[1] // LANES):
        xc = acc[:, c * LANES:(c + 1) * LANES]
        if c < n_rope:
            xc = (xc * cos + _swap_half_heads(xc, lane) * sin) * scale
        o_ref[:, c * LANES:(c + 1) * LANES] = xc


def mm_rope(x, w, col0, n, cos, sin, n_rope, scale):
    m, k = x.shape
    j0 = col0 // TN_MM
    return pl.pallas_call(
        functools.partial(_mm_rope_kernel, n_rope=n_rope, scale=scale),
        grid=(n // TN_MM, m // TM_MM),
        in_specs=[
            pl.BlockSpec((TM_MM, k), lambda j, i: (i, 0)),
            pl.BlockSpec((k, TN_MM), lambda j, i: (0, j + j0)),
            pl.BlockSpec((TM_MM, LANES), lambda j, i: (i, 0)),
            pl.BlockSpec((TM_MM, LANES), lambda j, i: (i, 0)),
        ],
        out_specs=pl.BlockSpec((TM_MM, TN_MM), lambda j, i: (i, j)),
        out_shape=jax.ShapeDtypeStruct((m, n), F32),
        scratch_shapes=[pltpu.VMEM((k, TN_MM), BF16)],
        compiler_params=_cp("parallel", "arbitrary"),
        name="mm_rope",
    )(x, w, cos, sin)


def _mm_res_ln_kernel(x_ref, w_ref, b_ref, res_ref, g_ref, beta_ref, of_ref, ob_ref):
    y = jnp.dot(x_ref[...], w_ref[...], preferred_element_type=F32) + b_ref[...]
    out = _ln_rows(ALPHA * res_ref[...] + y, g_ref[...], beta_ref[...])
    of_ref[...] = out
    ob_ref[...] = out.astype(BF16)


def mm_res_ln(x, w, b, res, g, beta):
    m, k = x.shape
    row = lambda i: (i, 0)
    fix = lambda i: (0, 0)
    return pl.pallas_call(
        _mm_res_ln_kernel,
        grid=(m // TM_LN,),
        in_specs=[
            pl.BlockSpec((TM_LN, k), row),
            pl.BlockSpec((k, D), fix),
            pl.BlockSpec((1, D), fix),
            pl.BlockSpec((TM_LN, D), row),
            pl.BlockSpec((1, D), fix),
            pl.BlockSpec((1, D), fix),
        ],
        out_specs=[pl.BlockSpec((TM_LN, D), row), pl.BlockSpec((TM_LN, D), row)],
        out_shape=[jax.ShapeDtypeStruct((m, D), F32), jax.ShapeDtypeStruct((m, D), BF16)],
        compiler_params=_cp("parallel"),
        name="mm_res_ln",
    )(x, w, b, res, g, beta)


def _res_ln_kernel(res_ref, y_ref, g_ref, beta_ref, of_ref, ob_ref):
    out = _ln_rows(ALPHA * res_ref[...] + y_ref[...].astype(F32), g_ref[...], beta_ref[...])
    of_ref[...] = out
    ob_ref[...] = out.astype(BF16)


def res_ln(res, y, g, beta):
    m = res.shape[0]
    row = lambda i: (i, 0)
    fix = lambda i: (0, 0)
    return pl.pallas_call(
        _res_ln_kernel,
        grid=(m // TM_LN,),
        in_specs=[pl.BlockSpec((TM_LN, D), row), pl.BlockSpec((TM_LN, D), row),
                  pl.BlockSpec((1, D), fix), pl.BlockSpec((1, D), fix)],
        out_specs=[pl.BlockSpec((TM_LN, D), row), pl.BlockSpec((TM_LN, D), row)],
        out_shape=[jax.ShapeDtypeStruct((m, D), F32), jax.ShapeDtypeStruct((m, D), BF16)],
        compiler_params=_cp("parallel"),
        name="res_ln",
    )(res, y, g, beta)


HALO = 32
CONV_RC = 64
CONV_LC = 512


def _conv_kernel(u_ref, buf_ref, wdw_ref, bdw_ref, g_ref, b_ref, z_ref, ext_ref, c_ref, *, tt):
    i = pl.program_id(1)

    @pl.when(i == 0)
    def _():
        ext_ref[0:HALO, :] = buf_ref[0]

    @pl.when(i > 0)
    def _():
        ext_ref[0:HALO, :] = ext_ref[tt:tt + HALO, :]

    ext_ref[HALO:HALO + tt, :] = u_ref[...]
    pad = HALO - (CONV_WIDTH - 1)
    rc = min(CONV_RC, tt)
    for r0 in range(0, tt, rc):
        for c0 in range(0, D, CONV_LC):
            acc = jnp.zeros((rc, CONV_LC), F32) + bdw_ref[:, c0:c0 + CONV_LC]
            for w in range(CONV_WIDTH):
                acc = acc + (ext_ref[r0 + pad + w:r0 + pad + w + rc, c0:c0 + CONV_LC]
                             * wdw_ref[w:w + 1, c0:c0 + CONV_LC])
            c_ref[r0:r0 + rc, c0:c0 + CONV_LC] = acc
    z_ref[...] = _silu(_ln_rows(c_ref[...], g_ref[...], b_ref[...])).astype(z_ref.dtype)


def conv_ln_silu(u, row0_blocks, nseq, t, tt, buf, wdw, bdw, g, b, out_dtype):
    nt = t // tt
    fix = lambda s, i: (0, 0)
    return pl.pallas_call(
        functools.partial(_conv_kernel, tt=tt),
        grid=(nseq, nt),
        in_specs=[
            pl.BlockSpec((tt, D), lambda s, i: (row0_blocks + s * nt + i, 0)),
            pl.BlockSpec((1, HALO, D), lambda s, i: (s, 0, 0)),
            pl.BlockSpec((CONV_WIDTH, D), fix),
            pl.BlockSpec((1, D), fix),
            pl.BlockSpec((1, D), fix),
            pl.BlockSpec((1, D), fix),
        ],
        out_specs=pl.BlockSpec((tt, D), lambda s, i: (s * nt + i, 0)),
        out_shape=jax.ShapeDtypeStruct((nseq * t, D), out_dtype),
        scratch_shapes=[pltpu.VMEM((HALO + tt, D), F32), pltpu.VMEM((tt, D), F32)],
        compiler_params=_cp("parallel", "arbitrary"),
        name="conv_ln_silu",
    )(u, buf, wdw, bdw, g, b)


def _hgrn_kernel(p_ref, s0_ref, ng_ref, o_ref, sf_ref, st_ref, pst_ref, ob_ref, *, tt, sc, nvalid):
    i = pl.program_id(1)
    hk = HG_H * HG_DK

    @pl.when(i == 0)
    def _():
        for h in range(HG_H):
            st_ref[h * HG_DV:(h + 1) * HG_DV, :] = s0_ref[0, h].T

    ones = jnp.ones((LANES, LANES), BF16)
    row = lax.broadcasted_iota(jnp.int32, (sc, hk), 0)

    def sub_chunk(ci, carry):
        r0 = pl.multiple_of(ci * sc, sc)
        q = p_ref[pl.ds(r0, sc), 0:hk]
        lf = p_ref[pl.ds(r0, sc), hk:2 * hk]
        v = p_ref[pl.ds(r0, sc), 2 * hk:3 * hk]
        if nvalid < sc:
            lf = jnp.where(row < nvalid, lf, 0.0)
        kk = 1.0 - jnp.exp(lf)
        bl = jnp.zeros((sc, hk), F32)
        for s in range(nvalid):
            bl = bl + jnp.where(row >= s, lf[s:s + 1, :], 0.0)
        btot = bl[sc - 1:sc, :]
        qd = q * jnp.exp(bl)
        kd = kk * jnp.exp(btot - bl)
        for s in range(nvalid):
            e = jnp.exp(jnp.where(row >= s, bl - bl[s:s + 1, :], -jnp.inf))
            pst_ref[s * sc:(s + 1) * sc, :] = q * e * kk[s:s + 1, :]
        o_acc = []
        for h in range(HG_H):
            sl = slice(h * HG_DK, (h + 1) * HG_DK)
            ph = pst_ref[0:nvalid * sc, sl]
            p_hi = ph.astype(BF16)
            p_lo = (ph - p_hi.astype(F32)).astype(BF16)
            rs = (jnp.dot(p_hi, ones, preferred_element_type=F32)
                  + jnp.dot(p_lo, ones, preferred_element_type=F32))
            vh = v[:, sl]
            st_h = st_ref[h * HG_DV:(h + 1) * HG_DV, :]
            oh = lax.dot_general(qd[:, sl].astype(BF16), st_h.astype(BF16),
                                 (((1,), (1,)), ((), ())), preferred_element_type=F32)
            for s in range(nvalid):
                oh = oh + rs[s * sc:(s + 1) * sc, :] * vh[s:s + 1, :]
            o_acc.append(oh)
            upd = jnp.dot(vh.T.astype(BF16), kd[:, sl].astype(BF16), preferred_element_type=F32)
            st_ref[h * HG_DV:(h + 1) * HG_DV, :] = jnp.exp(btot[:, sl]) * st_h + upd
        ob_ref[pl.ds(r0, sc), :] = jnp.concatenate(o_acc, axis=1)
        return carry

    lax.fori_loop(0, tt // sc, sub_chunk, 0)

    o = ob_ref[...]
    o2 = o * o
    parts = []
    for h in range(HG_H):
        sl = slice(h * HG_DV, (h + 1) * HG_DV)
        x2 = o2[:, sl]
        x_hi = x2.astype(BF16)
        x_lo = (x2 - x_hi.astype(F32)).astype(BF16)
        ms = (jnp.dot(x_hi, ones, preferred_element_type=F32)
              + jnp.dot(x_lo, ones, preferred_element_type=F32)) * (1.0 / HG_DV)
        parts.append(o[:, sl] * lax.rsqrt(ms + RMS_EPS))
    on = jnp.concatenate(parts, axis=1) * ng_ref[...] * p_ref[:, 3 * hk:4 * hk]
    o_ref[...] = on.astype(o_ref.dtype)

    @pl.when(i == pl.num_programs(1) - 1)
    def _():
        for h in range(HG_H):
            sf_ref[0, h] = st_ref[h * HG_DV:(h + 1) * HG_DV, :].T


def hgrn_scan(p, row0_blocks, nseq, t, tt, sc, nvalid, s0, norm_g, out_dtype):
    nt = t // tt
    return pl.pallas_call(
        functools.partial(_hgrn_kernel, tt=tt, sc=sc, nvalid=nvalid),
        grid=(nseq, nt),
        in_specs=[
            pl.BlockSpec((tt, 4 * D), lambda s, i: (row0_blocks + s * nt + i, 0)),
            pl.BlockSpec((1, HG_H, HG_DK, HG_DV), lambda s, i: (s, 0, 0, 0)),
            pl.BlockSpec((1, D), lambda s, i: (0, 0)),
        ],
        out_specs=[
            pl.BlockSpec((tt, D), lambda s, i: (s * nt + i, 0)),
            pl.BlockSpec((1, HG_H, HG_DK, HG_DV), lambda s, i: (s, 0, 0, 0)),
        ],
        out_shape=[jax.ShapeDtypeStruct((nseq * t, D), out_dtype),
                   jax.ShapeDtypeStruct((nseq, HG_H, HG_DK, HG_DV), F32)],
        scratch_shapes=[pltpu.VMEM((HG_H * HG_DV, HG_DK), F32),
                        pltpu.VMEM((sc * sc, D), F32),
                        pltpu.VMEM((tt, D), F32)],
        compiler_params=_cp("parallel", "arbitrary"),
        name="hgrn_scan",
    )(p, s0, norm_g)


S5_GPB = LANES // SSM_GROUP
S5_NJ = D // LANES
S5_SW = S5_GPB * SSM_STATE
S5_LC = 8


def _spread_groups(blk, rep, mask):
    full = jnp.dot(blk.astype(BF16), rep, preferred_element_type=F32)
    return jnp.where(mask, full, 0.0).astype(BF16)


def _group_mask(rows, row_w, cols, col_w):
    r = lax.broadcasted_iota(jnp.int32, (rows, cols), 0) // row_w
    c = lax.broadcasted_iota(jnp.int32, (rows, cols), 1) // col_w
    return r == c


def _rep_matrix(w, cols):
    r = lax.broadcasted_iota(jnp.int32, (w, cols), 0)
    c = lax.broadcasted_iota(jnp.int32, (w, cols), 1) % w
    return (r == c).astype(BF16)


def _s5_kernel(u_ref, h0_ref, klt_ref, wtr_ref, wti_ref, car_ref, cai_ref, ar_ref, ai_ref, y_ref, hf_ref,
               t_s, bp_s, cp_s, con_ref, hp_ref, *, nc, nv):
    sw = S5_SW
    lc = S5_LC
    c, p = SSM_GROUP, SSM_STATE
    rep_c = _rep_matrix(c, LANES)
    rep_p = _rep_matrix(p, sw)
    t_s[...] = jnp.zeros(t_s.shape, BF16)
    mask_t = _group_mask(LANES, c, LANES, c)
    for l in range(lc):
        blk = _spread_groups(klt_ref[0, l], rep_c, mask_t)
        for s in range(lc - l):
            t_s[s * LANES:(s + 1) * LANES, (s + l) * LANES:(s + l + 1) * LANES] = blk
    mask_b = _group_mask(LANES, c, sw, p)
    for s in range(lc):
        if s < nv:
            bp_s[s * LANES:(s + 1) * LANES, 0:sw] = _spread_groups(wtr_ref[0, nv - 1 - s], rep_p, mask_b)
            bp_s[s * LANES:(s + 1) * LANES, sw:2 * sw] = _spread_groups(wti_ref[0, nv - 1 - s], rep_p, mask_b)
        else:
            bp_s[s * LANES:(s + 1) * LANES, :] = jnp.zeros((LANES, 2 * sw), BF16)
    mask_c = _group_mask(sw, p, LANES, c)
    for t in range(lc):
        cp_s[0:sw, t * LANES:(t + 1) * LANES] = _spread_groups(car_ref[0, t], rep_c, mask_c)
        cp_s[sw:2 * sw, t * LANES:(t + 1) * LANES] = _spread_groups(cai_ref[0, t], rep_c, mask_c)

    u = u_ref[0]
    con_ref[...] = jnp.dot(u, bp_s[...], preferred_element_type=F32)
    ar = ar_ref[0]
    ai = ai_ref[0]

    def step(hr, hi, cr, ci):
        return ar * hr - ai * hi + cr, ar * hi + ai * hr + ci

    if nc == 1:
        hr, hi = h0_ref[0, :, 0:sw], h0_ref[0, :, sw:2 * sw]
        hp_ref[:, 0:sw] = hr
        hp_ref[:, sw:2 * sw] = hi
        hr, hi = step(hr, hi, con_ref[:, 0:sw], con_ref[:, sw:2 * sw])
    else:
        half = 4
        top = lax.broadcasted_iota(jnp.int32, (2 * half, sw), 0) < half

        def pair(k, carry):
            hr, hi = carry
            r0 = pl.multiple_of(k * 2 * half, 2 * half)
            cr = con_ref[pl.ds(r0, 2 * half), 0:sw]
            ci = con_ref[pl.ds(r0, 2 * half), sw:2 * sw]
            ar1, ai1 = step(hr, hi, cr, ci)
            ar1 = jnp.where(top, ar1, pltpu.roll(ar1, half, 0))
            ai1 = jnp.where(top, ai1, pltpu.roll(ai1, half, 0))
            hp_ref[pl.ds(r0, 2 * half), 0:sw] = jnp.where(top, hr, ar1)
            hp_ref[pl.ds(r0, 2 * half), sw:2 * sw] = jnp.where(top, hi, ai1)
            br, bi = step(ar1, ai1, cr, ci)
            br = jnp.where(top, pltpu.roll(br, half, 0), br)
            bi = jnp.where(top, pltpu.roll(bi, half, 0), bi)
            return br, bi

        hr, hi = lax.fori_loop(0, nc // 2, pair, (h0_ref[0, :, 0:sw], h0_ref[0, :, sw:2 * sw]))
    hf_ref[0, :, 0:sw] = hr
    hf_ref[0, :, sw:2 * sw] = hi
    y = (jnp.dot(u, t_s[...], preferred_element_type=F32)
         + jnp.dot(hp_ref[...].astype(BF16), cp_s[...], preferred_element_type=F32))
    y_ref[0] = jax.nn.gelu(y).astype(y_ref.dtype)


def s5_lti(u, h0, ops, nc, nv):
    klt, wtr, wti, car, cai, pw_re, pw_im = ops
    nj, rows, lw = u.shape
    hrows = h0.shape[1]
    sw, lc, c, p = S5_SW, S5_LC, SSM_GROUP, SSM_STATE
    ar = pw_re[:, nv].reshape(nj, 1, sw)
    ai = pw_im[:, nv].reshape(nj, 1, sw)
    blk = lambda j: (j, 0, 0)
    blk4 = lambda j: (j, 0, 0, 0)
    return pl.pallas_call(
        functools.partial(_s5_kernel, nc=nc, nv=nv),
        grid=(nj,),
        in_specs=[
            pl.BlockSpec((1, rows, lw), blk),
            pl.BlockSpec((1, hrows, 2 * sw), blk),
            pl.BlockSpec((1, lc, LANES, c), blk4),
            pl.BlockSpec((1, lc, LANES, p), blk4),
            pl.BlockSpec((1, lc, LANES, p), blk4),
            pl.BlockSpec((1, lc, sw, c), blk4),
            pl.BlockSpec((1, lc, sw, c), blk4),
            pl.BlockSpec((1, 1, sw), blk),
            pl.BlockSpec((1, 1, sw), blk),
        ],
        out_specs=[pl.BlockSpec((1, rows, lw), blk), pl.BlockSpec((1, hrows, 2 * sw), blk)],
        out_shape=[jax.ShapeDtypeStruct((nj, rows, lw), BF16),
                   jax.ShapeDtypeStruct((nj, hrows, 2 * sw), F32)],
        scratch_shapes=[pltpu.VMEM((lw, lw), BF16), pltpu.VMEM((lw, 2 * sw), BF16),
                        pltpu.VMEM((2 * sw, lw), BF16),
                        pltpu.VMEM((rows, 2 * sw), F32), pltpu.VMEM((rows, 2 * sw), F32)],
        compiler_params=_cp("parallel"),
        name="s5_lti",
    )(u, h0, klt, wtr, wti, car, cai, ar, ai)


def _s5_operators(lam_re, lam_im, log_dt, b_re, b_im, c_re, c_im, d_skip):
    hp = lax.Precision.HIGHEST
    gpb, nj, c, p, lc = S5_GPB, S5_NJ, SSM_GROUP, SSM_STATE, S5_LC
    dt = jnp.exp(log_dt)[:, None]
    lr, li = lam_re, lam_im
    mag = jnp.exp(lr * dt)
    ab_re, ab_im = mag * jnp.cos(li * dt), mag * jnp.sin(li * dt)
    nr, ni = ab_re - 1.0, ab_im
    den = lr * lr + li * li
    z_re, z_im = (nr * lr + ni * li) / den, (ni * lr - nr * li) / den
    bbt_re = (z_re[..., None] * b_re - z_im[..., None] * b_im).transpose(0, 2, 1)
    bbt_im = (z_re[..., None] * b_im + z_im[..., None] * b_re).transpose(0, 2, 1)
    l = jnp.arange(lc + 1, dtype=F32)[None, :, None]
    pmag = jnp.exp(l * (lr * dt)[:, None, :])
    pw_re = pmag * jnp.cos(l * (li * dt)[:, None, :])
    pw_im = pmag * jnp.sin(l * (li * dt)[:, None, :])
    wt_re = pw_re[:, :lc, None, :] * bbt_re[:, None] - pw_im[:, :lc, None, :] * bbt_im[:, None]
    wt_im = pw_re[:, :lc, None, :] * bbt_im[:, None] + pw_im[:, :lc, None, :] * bbt_re[:, None]
    klt = (jnp.einsum('glcp,gop->glco', wt_re, c_re, precision=hp)
           - jnp.einsum('glcp,gop->glco', wt_im, c_im, precision=hp))
    klt = klt.at[:, 0].add(d_skip.reshape(SSM_GROUPS, c)[:, :, None] * jnp.eye(c, dtype=F32)[None])
    ct_re, ct_im = c_re.transpose(0, 2, 1)[:, None], c_im.transpose(0, 2, 1)[:, None]
    ca_re = ct_re * pw_re[:, 1:, :, None] - ct_im * pw_im[:, 1:, :, None]
    ca_im = ct_re * pw_im[:, 1:, :, None] + ct_im * pw_re[:, 1:, :, None]

    def by_block(a):
        _, ll, r, w = a.shape
        return a.reshape(nj, gpb, ll, r, w).transpose(0, 2, 1, 3, 4).reshape(nj, ll, gpb * r, w)

    def pw_block(a):
        return a.reshape(nj, gpb, lc + 1, p).transpose(0, 2, 1, 3).reshape(nj, lc + 1, gpb * p)

    return (by_block(klt), by_block(wt_re), by_block(wt_im), by_block(ca_re), by_block(-ca_im),
            pw_block(pw_re), pw_block(pw_im))


def _attn_kernel(q_ref, kv_ref, pk_ref, pv_ref, sink_ref, o_ref, *, tq, nblk, always_prev):
    idx = pl.program_id(0)
    kvw = NKV * HD
    group = NH // NKV
    pkn = pk_ref.shape[1]
    to = kv_ref.shape[1]
    nk = pkn + to
    rows = group * tq
    a = lax.broadcasted_iota(jnp.int32, (rows, nk), 0) % tq
    col = lax.broadcasted_iota(jnp.int32, (rows, nk), 1)
    prev_ok = col >= a
    if not always_prev:
        prev_ok = prev_ok & ((idx % nblk) > 0)
    mask = ((col < pkn) & prev_ok) | ((col >= pkn) & ((col - pkn) <= a))
    hrow = lax.broadcasted_iota(jnp.int32, (rows, 1), 0) // tq
    low = lax.broadcasted_iota(jnp.int32, (tq, LANES), 1) < HD
    nt = (((1,), (1,)), ((), ()))
    for kb in range(NKV // 2):
        lanes = slice(kb * LANES, (kb + 1) * LANES)
        kcat = jnp.concatenate([pk_ref[0, :, lanes], kv_ref[0, :, lanes]], axis=0).astype(BF16)
        vcat = jnp.concatenate([pv_ref[0, :, lanes],
                                kv_ref[0, :, kvw + kb * LANES:kvw + (kb + 1) * LANES]], axis=0).astype(BF16)
        for hi in range(2):
            j = 2 * kb + hi
            parts = []
            sk = jnp.zeros((rows, 1), F32)
            for h8 in range(group):
                qb, e = (group * j + h8) // 2, h8 % 2
                q2 = q_ref[0, :, qb * LANES:(qb + 1) * LANES]
                qe = jnp.where(low if e == 0 else jnp.logical_not(low), q2, 0.0)
                if e != hi:
                    qe = pltpu.roll(qe, HD, 1)
                parts.append(qe)
                sk = jnp.where(hrow == h8, sink_ref[group * j + h8], sk)
            q8 = jnp.concatenate(parts, axis=0).astype(BF16)
            s = lax.dot_general(q8, kcat, nt, preferred_element_type=F32)
            s = jnp.where(mask, s, -jnp.inf)
            m = jnp.maximum(jnp.max(s, -1, keepdims=True), sk)
            p = jnp.exp(s - m)
            inv = 1.0 / (jnp.sum(p, -1, keepdims=True) + jnp.exp(sk - m))
            o8 = jnp.dot((p * inv).astype(BF16), vcat, preferred_element_type=F32)
            for pr in range(group // 2):
                qb = (group * j) // 2 + pr
                pair = []
                for e in range(2):
                    oe = o8[(2 * pr + e) * tq:(2 * pr + e + 1) * tq, :]
                    if e != hi:
                        oe = pltpu.roll(oe, HD, 1)
                    pair.append(oe)
                o_ref[0, :, qb * LANES:(qb + 1) * LANES] = jnp.where(low, pair[0], pair[1]).astype(o_ref.dtype)


def swa_attention(q3, kv3, pk3, pv3, nb, blk0, pk_map, pv_map, sinks, nblk, always_prev):
    tq = q3.shape[1]
    to = kv3.shape[1]
    kvw = NKV * HD
    return pl.pallas_call(
        functools.partial(_attn_kernel, tq=tq, nblk=nblk, always_prev=always_prev),
        grid=(nb,),
        in_specs=[
            pl.BlockSpec((1, tq, D), lambda i: (i + blk0, 0, 0)),
            pl.BlockSpec((1, to, 2 * kvw), lambda i: (i, 0, 0)),
            pl.BlockSpec((1, pk3.shape[1], kvw), pk_map),
            pl.BlockSpec((1, pv3.shape[1], kvw), pv_map),
            pl.BlockSpec(memory_space=pltpu.SMEM),
        ],
        out_specs=pl.BlockSpec((1, tq, D), lambda i: (i, 0, 0)),
        out_shape=jax.ShapeDtypeStruct((nb, tq, D), BF16),
        compiler_params=_cp("parallel"),
        name="swa_attention",
    )(q3, kv3, pk3, pv3, sinks)


R_E1, R_E2, R_W1, R_W2, R_R1, R_R2 = 0, 1, 2, 3, 4, 5


def _router_kernel(x_ref, w_ref, b_ref, o_ref, cnt_ref, carry_ref):
    i = pl.program_id(0)

    @pl.when(i == 0)
    def _():
        carry_ref[...] = jnp.zeros(carry_ref.shape, F32)

    l = jnp.dot(x_ref[...], w_ref[...], preferred_element_type=F32,
                precision=lax.Precision.HIGHEST) + b_ref[...]
    tm = l.shape[0]
    lane = lax.broadcasted_iota(jnp.int32, l.shape, 1).astype(F32)
    big = 1e9
    ninf = -jnp.inf
    gmask = lane < 4.0
    gl = jnp.where(gmask, l, ninf)
    gm = jnp.max(gl, -1, keepdims=True)
    gsel = jnp.min(jnp.where(gl == gm, lane, big), -1, keepdims=True)
    gsum = jnp.sum(jnp.where(gmask, jnp.exp(gl - gm), 0.0), -1, keepdims=True)
    gval = 1.0 / gsum
    lo = 4.0 + 4.0 * gsel
    emask = (lane >= lo) & (lane < lo + 4.0)
    el = jnp.where(emask, l, ninf)
    m1 = jnp.max(el, -1, keepdims=True)
    i1 = jnp.min(jnp.where(emask & (el == m1), lane, big), -1, keepdims=True)
    emask2 = emask & (lane != i1)
    el2 = jnp.where(emask2, l, ninf)
    m2 = jnp.max(el2, -1, keepdims=True)
    i2 = jnp.min(jnp.where(emask2 & (el2 == m2), lane, big), -1, keepdims=True)
    r = jnp.exp(m2 - m1)
    w1 = gval / (1.0 + r)
    w2 = gval * r / (1.0 + r)
    e1 = i1 - 4.0
    e2 = i2 - 4.0
    oh = ((lane == e1) | (lane == e2)).astype(F32)
    tri = (lax.broadcasted_iota(jnp.int32, (tm, tm), 1)
           < lax.broadcasted_iota(jnp.int32, (tm, tm), 0)).astype(BF16)
    before = jnp.dot(tri, oh.astype(BF16), preferred_element_type=F32) + carry_ref[...]
    r1 = jnp.sum(jnp.where(lane == e1, before, 0.0), -1, keepdims=True)
    r2 = jnp.sum(jnp.where(lane == e2, before, 0.0), -1, keepdims=True)
    carry_ref[...] = carry_ref[...] + jnp.sum(oh, axis=0, keepdims=True)
    cnt_ref[...] = carry_ref[...]
    rec = jnp.zeros(l.shape, F32)
    for ln, val in ((R_E1, e1), (R_E2, e2), (R_W1, w1), (R_W2, w2), (R_R1, r1), (R_R2, r2)):
        rec = jnp.where(lane == float(ln), val, rec)
    o_ref[...] = rec


def moe_router(x, w, b):
    m = x.shape[0]
    return pl.pallas_call(
        _router_kernel,
        grid=(m // TM_LN,),
        in_specs=[pl.BlockSpec((TM_LN, D), lambda i: (i, 0)),
                  pl.BlockSpec((D, LANES), lambda i: (0, 0)),
                  pl.BlockSpec((1, LANES), lambda i: (0, 0))],
        out_specs=[pl.BlockSpec((TM_LN, LANES), lambda i: (i, 0)),
                   pl.BlockSpec((1, LANES), lambda i: (0, 0))],
        out_shape=[jax.ShapeDtypeStruct((m, LANES), F32), jax.ShapeDtypeStruct((1, LANES), F32)],
        scratch_shapes=[pltpu.VMEM((1, LANES), F32)],
        compiler_params=_cp("arbitrary"),
        name="moe_router",
    )(x, w, b)


def _slots_kernel(rec_ref, starts_ref, o_ref):
    rec = rec_ref[...]
    lane = lax.broadcasted_iota(jnp.int32, rec.shape, 1).astype(F32)
    starts = starts_ref[...]
    out = jnp.zeros(rec.shape, F32)
    for k, (le, lr) in enumerate(((R_E1, R_R1), (R_E2, R_R2))):
        e = rec[:, le:le + 1]
        s = jnp.sum(jnp.where(lane == e, starts, 0.0), -1, keepdims=True) + rec[:, lr:lr + 1]
        out = jnp.where(lane == float(k), s, out)
    o_ref[...] = out


def moe_slots(rec, starts_row):
    m = rec.shape[0]
    return pl.pallas_call(
        _slots_kernel,
        grid=(m // TM_MM,),
        in_specs=[pl.BlockSpec((TM_MM, LANES), lambda i: (i, 0)), pl.BlockSpec((1, LANES), lambda i: (0, 0))],
        out_specs=pl.BlockSpec((TM_MM, LANES), lambda i: (i, 0)),
        out_shape=jax.ShapeDtypeStruct((m, LANES), F32),
        compiler_params=_cp("parallel"),
        name="moe_slots",
    )(rec, starts_row)


def _row_gather_start(idx_ref, base, n, src_hbm, dst, sem):
    def body(it, carry):
        for u in range(DMA_UNROLL):
            r = it * DMA_UNROLL + u
            pltpu.make_async_copy(src_hbm.at[pl.ds(idx_ref[base + r], 1)], dst.at[pl.ds(r, 1)], sem).start()
        return carry
    lax.fori_loop(0, n // DMA_UNROLL, body, 0)


def _row_gather_wait(n, src_hbm, dst, sem):
    pltpu.make_async_copy(src_hbm.at[pl.ds(0, n)], dst.at[pl.ds(0, n)], sem).wait()


def _ffn_kernel(te_ref, nt_ref, src_ref, x_hbm, w1_ref, w3_ref, w2_ref, y_ref, xbuf, sem, w1b, w3b, w2b):
    i = pl.program_id(0)
    nt = nt_ref[0]
    slot = i % 2

    @pl.when(i == 0)
    def _():
        _row_gather_start(src_ref, 0, TM_E, x_hbm, xbuf.at[0], sem.at[0])

    @pl.when(i + 1 < nt)
    def _():
        _row_gather_start(src_ref, (i + 1) * TM_E, TM_E, x_hbm, xbuf.at[1 - slot], sem.at[1 - slot])

    @pl.when(i < nt)
    def _():
        prev = te_ref[jnp.maximum(i - 1, 0)]

        @pl.when((i == 0) | (te_ref[i] != prev))
        def _():
            w1b[...] = w1_ref[0, 0].astype(BF16)
            w3b[...] = w3_ref[0, 0].astype(BF16)
            w2b[...] = w2_ref[0, 0].astype(BF16)

        _row_gather_wait(TM_E, x_hbm, xbuf.at[slot], sem.at[slot])
        x = xbuf[slot].astype(BF16)
        h1 = jnp.dot(x, w1b[...], preferred_element_type=F32)
        h3 = jnp.dot(x, w3b[...], preferred_element_type=F32)
        h = _silu(h1) * h3
        y_ref[...] = jnp.dot(h.astype(BF16), w2b[...], preferred_element_type=F32)

    @pl.when(i >= nt)
    def _():
        y_ref[...] = jnp.zeros(y_ref.shape, y_ref.dtype)


def moe_ffn(layer, tile_expert, n_tiles, src, x, w1, w3, w2):
    wmap = lambda i, te, nt, src: (layer, te[i], 0, 0)
    grid_spec = pltpu.PrefetchScalarGridSpec(
        num_scalar_prefetch=3,
        grid=(N_ETILES,),
        in_specs=[
            pl.BlockSpec(memory_space=pl.ANY),
            pl.BlockSpec((1, 1, D, MOE_FF), wmap),
            pl.BlockSpec((1, 1, D, MOE_FF), wmap),
            pl.BlockSpec((1, 1, MOE_FF, D), wmap),
        ],
        out_specs=pl.BlockSpec((TM_E, D), lambda i, te, nt, src: (i, 0)),
        scratch_shapes=[pltpu.VMEM((2, TM_E, D), F32), pltpu.SemaphoreType.DMA((2,)),
                        pltpu.VMEM((D, MOE_FF), BF16), pltpu.VMEM((D, MOE_FF), BF16),
                        pltpu.VMEM((MOE_FF, D), BF16)],
    )
    return pl.pallas_call(
        _ffn_kernel,
        grid_spec=grid_spec,
        out_shape=jax.ShapeDtypeStruct((P_SLOTS, D), F32),
        compiler_params=_cp("arbitrary"),
        name="moe_ffn",
    )(tile_expert, n_tiles, src, x, w1, w3, w2)


def _combine_ln_kernel(pos_ref, res_ref, rec_ref, g_ref, beta_ref, ys_hbm, of_ref, ob_ref, ybuf, sem):
    i = pl.program_id(0)
    slot = i % 2
    n = 2 * TM_LN

    @pl.when(i == 0)
    def _():
        _row_gather_start(pos_ref, 0, n, ys_hbm, ybuf.at[0], sem.at[0])

    @pl.when(i + 1 < pl.num_programs(0))
    def _():
        _row_gather_start(pos_ref, (i + 1) * n, n, ys_hbm, ybuf.at[1 - slot], sem.at[1 - slot])

    _row_gather_wait(n, ys_hbm, ybuf.at[slot], sem.at[slot])
    rec = rec_ref[...]
    y = (rec[:, R_W1:R_W1 + 1] * ybuf[slot, 0:TM_LN, :]
         + rec[:, R_W2:R_W2 + 1] * ybuf[slot, TM_LN:2 * TM_LN, :])
    out = _ln_rows(ALPHA * res_ref[...] + y, g_ref[...], beta_ref[...])
    of_ref[...] = out
    ob_ref[...] = out.astype(BF16)


def combine_ln(pos_tiles, res, rec, g, beta, ys):
    m = res.shape[0]
    row = lambda i, pos: (i, 0)
    fix = lambda i, pos: (0, 0)
    grid_spec = pltpu.PrefetchScalarGridSpec(
        num_scalar_prefetch=1,
        grid=(m // TM_LN,),
        in_specs=[pl.BlockSpec((TM_LN, D), row), pl.BlockSpec((TM_LN, LANES), row),
                  pl.BlockSpec((1, D), fix), pl.BlockSpec((1, D), fix),
                  pl.BlockSpec(memory_space=pl.ANY)],
        out_specs=[pl.BlockSpec((TM_LN, D), row), pl.BlockSpec((TM_LN, D), row)],
        scratch_shapes=[pltpu.VMEM((2, 2 * TM_LN, D), F32), pltpu.SemaphoreType.DMA((2,))],
    )
    return pl.pallas_call(
        _combine_ln_kernel,
        grid_spec=grid_spec,
        out_shape=[jax.ShapeDtypeStruct((m, D), F32), jax.ShapeDtypeStruct((m, D), BF16)],
        compiler_params=_cp("arbitrary"),
        name="combine_ln",
    )(pos_tiles, res, rec, g, beta, ys)


def _dispatch_plan(rec, counts_row):
    counts = counts_row[0, :MOE_E].astype(jnp.int32)
    padded = ((counts + TM_E - 1) // TM_E) * TM_E
    ends = jnp.cumsum(padded)
    starts = ends - padded
    starts_row = jnp.pad(starts.astype(F32), (0, LANES - MOE_E))[None, :]
    pos = moe_slots(rec, starts_row)[:, 0:2].astype(jnp.int32)
    tok = jnp.broadcast_to(jnp.arange(MT, dtype=jnp.int32)[:, None], (MT, 2))
    src = jnp.zeros((P_SLOTS,), jnp.int32).at[pos.reshape(-1)].set(tok.reshape(-1))
    n_tiles = (ends[-1] // TM_E).astype(jnp.int32)
    tile_start = jnp.arange(N_ETILES, dtype=jnp.int32) * TM_E
    te = jnp.sum((tile_start[:, None] >= ends[None, :]).astype(jnp.int32), axis=1)
    last = jnp.sum(jnp.where(jnp.arange(N_ETILES) == n_tiles - 1, te, 0))
    te = jnp.minimum(jnp.where(jnp.arange(N_ETILES) < n_tiles, te, last), MOE_E - 1).astype(jnp.int32)
    pos_tiles = pos.reshape(MT // TM_LN, TM_LN, 2).transpose(0, 2, 1).reshape(-1)
    return src, te, n_tiles.reshape(1), pos_tiles


def hier_moe_block(xf, i, moe_w_group, moe_b_group, moe_w_egate, moe_b_egate,
                   moe_w1, moe_w3, moe_w2, ln_g, ln_b):
    wr = jnp.concatenate([moe_w_group[i], moe_w_egate[i].transpose(1, 0, 2).reshape(D, MOE_E)], axis=1)
    wr = jnp.pad(wr, ((0, 0), (0, LANES - wr.shape[1])))
    br = jnp.concatenate([moe_b_group[i], moe_b_egate[i].reshape(MOE_E)])
    br = jnp.pad(br, (0, LANES - br.shape[0]))[None, :]
    rec, counts = moe_router(xf, wr, br)
    src, te, n_tiles, pos_tiles = _dispatch_plan(rec, counts)
    ys = moe_ffn(i, te, n_tiles, src, xf, moe_w1, moe_w3, moe_w2)
    return combine_ln(pos_tiles, xf, rec, ln_g[i][None, :], ln_b[i][None, :], ys)


def _sample_rows(a):
    return a[NP:].reshape(DEC_BATCH, TS, a.shape[-1])


def _prompt_tail(a, n):
    return jnp.stack([a[(b + 1) * SEQ - n:(b + 1) * SEQ] for b in range(BATCH)])


def conv_layer(xf, xb, state, w_in, b_in, w_dw, b_dw, g, b, w_out, b_out, ln_g, ln_b):
    u = mm_glu(xb, w_in, b_in[None, :], F32)
    tt_p = 256
    zbuf = jnp.zeros((BATCH, HALO, D), F32)
    z_p = conv_ln_silu(u, 0, BATCH, SEQ, tt_p, zbuf, w_dw, b_dw[None, :], g[None, :], b[None, :], BF16)
    pad = HALO - (CONV_WIDTH - 1)
    sbuf = jnp.pad(state, ((0, 0), (pad, 0), (0, 0)))
    z_s = conv_ln_silu(u, NP // TS, DEC_BATCH, TS, TS, sbuf, w_dw, b_dw[None, :], g[None, :], b[None, :], F32)
    z = jnp.concatenate([z_p, z_s.astype(BF16)], axis=0)
    xf, xb = mm_res_ln(z, w_out.astype(BF16), b_out[None, :], xf, ln_g[None, :], ln_b[None, :])
    new_p = _prompt_tail(u, CONV_WIDTH - 1)
    new_s = jnp.concatenate([state[:, DEC_SEQ:], _sample_rows(u)[:, :DEC_SEQ]], axis=1)
    return xf, xb, new_p, new_s


def hgrn_layer(xf, xb, state, w_in, lb, norm_g, w_out, ln_g, ln_b):
    hk = HG_H * HG_DK
    lb_row = jnp.concatenate([jnp.zeros((hk,), F32), lb, jnp.zeros((2 * hk,), F32)])[None, :]
    p = mm_hgrn(xb, w_in, lb_row)
    s0_p = jnp.zeros((BATCH, HG_H, HG_DK, HG_DV), F32)
    o_p, sf_p = hgrn_scan(p, 0, BATCH, SEQ, 128, 16, 16, s0_p, norm_g[None, :], BF16)
    o_s, sf_s = hgrn_scan(p, NP // TS, DEC_BATCH, TS, TS, TS, DEC_SEQ, state, norm_g[None, :], F32)
    o = jnp.concatenate([o_p, o_s.astype(BF16)], axis=0)
    zero_b = jnp.zeros((1, D), F32)
    xf, xb = mm_res_ln(o, w_out.astype(BF16), zero_b, xf, ln_g[None, :], ln_b[None, :])
    return xf, xb, sf_p, sf_s


def s5_layer(xf, xb, h_re, h_im, lam_re, lam_im, log_dt, b_re, b_im, c_re, c_im, d_skip, w_glu,
             ln_g, ln_b):
    nj, sw, lc = S5_NJ, S5_SW, S5_LC
    ops = _s5_operators(lam_re, lam_im, log_dt, b_re, b_im, c_re, c_im, d_skip)
    nc = SEQ // lc
    up = xb[:NP].reshape(BATCH, nc, lc, nj, LANES).transpose(3, 1, 0, 2, 4).reshape(nj, nc * BATCH, lc * LANES)
    y_p, hf_p = s5_lti(up, jnp.zeros((nj, 2 * BATCH, 2 * sw), F32), ops, nc, lc)
    y_p = y_p.reshape(nj, nc, BATCH, lc, LANES).transpose(2, 1, 3, 0, 4).reshape(NP, D)
    hf_p = hf_p[:, :BATCH].transpose(1, 0, 2)
    us = _sample_rows(xb).reshape(DEC_BATCH, TS, nj, LANES).transpose(2, 0, 1, 3).reshape(nj, DEC_BATCH, TS * LANES)
    h0 = jnp.concatenate([h_re.reshape(DEC_BATCH, nj, sw), h_im.reshape(DEC_BATCH, nj, sw)], -1).transpose(1, 0, 2)
    y_s, hf_s = s5_lti(us, h0, ops, 1, DEC_SEQ)
    y_s = y_s.reshape(nj, DEC_BATCH, TS, LANES).transpose(1, 2, 0, 3).reshape(NS, D)
    hf_s = hf_s.transpose(1, 0, 2)
    y = jnp.concatenate([y_p, y_s], axis=0)
    m = mm_glu(y, w_glu, jnp.zeros((1, 2 * D), F32), BF16)
    xf, xb = res_ln(xf, m, ln_g[None, :], ln_b[None, :])
    st = lambda h, n: (h[..., :sw].reshape(n, SSM_GROUPS, SSM_STATE), h[..., sw:].reshape(n, SSM_GROUPS, SSM_STATE))
    return (xf, xb) + st(hf_p, BATCH) + st(hf_s, DEC_BATCH)


def _rope_tables(pos):
    half = HD // 2
    inv = ROPE_THETA ** (-jnp.arange(half, dtype=F32) / half)
    ang = pos.astype(F32)[:, None] * inv[None, :]
    cos, sin = jnp.cos(ang), jnp.sin(ang)
    cos_t = jnp.tile(jnp.concatenate([cos, cos], -1), (1, LANES // HD))
    sin_t = jnp.tile(jnp.concatenate([-sin, sin], -1), (1, LANES // HD))
    return cos_t, sin_t


def swa_layer(xf, xb, cache_k, cache_v, w_qkv, sinks, w_out, ln_g, ln_b):
    hq = NH * HD
    kvw = NKV * HD
    pos = jnp.concatenate([jnp.tile(jnp.arange(SEQ), BATCH),
                           jnp.tile(PAST_LEN + jnp.arange(TS), DEC_BATCH)])
    cos_t, sin_t = _rope_tables(pos)
    q = mm_rope(xb, w_qkv, 0, hq, cos_t, sin_t, TN_MM // LANES, HD ** -0.5)
    kv = mm_rope(xb, w_qkv, hq, 2 * kvw, cos_t, sin_t, kvw // LANES, 1.0)
    nblk = SEQ // WINDOW
    qw = q.reshape(MT // WINDOW, WINDOW, D)
    kvwin = kv.reshape(MT // WINDOW, WINDOW, 2 * kvw)
    prev = lambda i: (jnp.maximum(i - 1, 0), 0, 0)
    prev_v = lambda i: (jnp.maximum(i - 1, 0), 0, 1)
    o_p = swa_attention(qw, kvwin, kvwin, kvwin, BATCH * nblk, 0, prev, prev_v, sinks, nblk, False).reshape(NP, D)
    kvs = _sample_rows(kv)
    kvs_pad = jnp.pad(kvs, ((0, 0), (0, WINDOW - TS), (0, 0)))
    ck = cache_k.reshape(DEC_BATCH, WINDOW, kvw)
    cv = cache_v.reshape(DEC_BATCH, WINDOW, kvw)
    same = lambda i: (i, 0, 0)
    o_s = swa_attention(q.reshape(MT // TS, TS, D), kvs_pad, ck, cv, DEC_BATCH, NP // TS, same, same,
                        sinks, 1, True).reshape(NS, D)
    o = jnp.concatenate([o_p, o_s], axis=0)
    xf, xb = mm_res_ln(o, w_out.astype(BF16), jnp.zeros((1, D), F32), xf, ln_g[None, :], ln_b[None, :])
    kvp = _prompt_tail(kv, WINDOW)
    nk_p = kvp[..., :kvw].reshape(BATCH, WINDOW, NKV, HD)
    nv_p = kvp[..., kvw:].reshape(BATCH, WINDOW, NKV, HD)
    nk_s = jnp.concatenate([ck[:, DEC_SEQ:], kvs[:, :DEC_SEQ, :kvw]], axis=1).reshape(DEC_BATCH, WINDOW, NKV, HD)
    nv_s = jnp.concatenate([cv[:, DEC_SEQ:], kvs[:, :DEC_SEQ, kvw:]], axis=1).reshape(DEC_BATCH, WINDOW, NKV, HD)
    return xf, xb, nk_p, nv_p, nk_s, nv_s


def kernel(x_prompt, x_sample, state_conv, state_hgrn, state_ssm_re, state_ssm_im, cache_swa_k, cache_swa_v, conv_w_in, conv_b_in, conv_w_dw, conv_b_dw, conv_ln_g, conv_ln_b, conv_w_out, conv_b_out, hgrn_w_in, hgrn_lb_logits, hgrn_norm_g, hgrn_w_out, ssm_lam_re, ssm_lam_im, ssm_log_dt, ssm_b_re, ssm_b_im, ssm_c_re, ssm_c_im, ssm_d, ssm_w_glu, swa_w_qkv, swa_sinks, swa_w_out, ln1_g, ln1_b, ln2_g, ln2_b, moe_w_group, moe_b_group, moe_w_egate, moe_b_egate, moe_w1, moe_w3, moe_w2):
    lbs = jnp.cumsum(jax.nn.softmax(hgrn_lb_logits.astype(F32), axis=0), axis=0)
    lbs = lbs - lbs[0]
    xs_pad = jnp.pad(x_sample, ((0, 0), (0, TS - DEC_SEQ), (0, 0)))
    xf = jnp.concatenate([x_prompt.reshape(NP, D), xs_pad.reshape(NS, D)], axis=0)
    xb = xf.astype(BF16)
    moe = (moe_w_group, moe_b_group, moe_w_egate, moe_b_egate, moe_w1, moe_w3, moe_w2, ln2_g, ln2_b)

    xf, xb, conv_p, conv_s = conv_layer(
        xf, xb, state_conv[0], conv_w_in[0], conv_b_in[0], conv_w_dw[0], conv_b_dw[0],
        conv_ln_g[0], conv_ln_b[0], conv_w_out[0], conv_b_out[0], ln1_g[0], ln1_b[0])
    xf, xb = hier_moe_block(xf, 0, *moe)

    xf, xb, hg_p, hg_s = hgrn_layer(xf, xb, state_hgrn[0], hgrn_w_in[0], lbs[1], hgrn_norm_g[0],
                                    hgrn_w_out[0], ln1_g[1], ln1_b[1])
    xf, xb = hier_moe_block(xf, 1, *moe)

    xf, xb, hr_p, hi_p, hr_s, hi_s = s5_layer(
        xf, xb, state_ssm_re[0], state_ssm_im[0], ssm_lam_re[0], ssm_lam_im[0], ssm_log_dt[0],
        ssm_b_re[0], ssm_b_im[0], ssm_c_re[0], ssm_c_im[0], ssm_d[0], ssm_w_glu[0], ln1_g[2], ln1_b[2])
    xf, xb = hier_moe_block(xf, 2, *moe)

    xf, xb, k_p, v_p, k_s, v_s = swa_layer(xf, xb, cache_swa_k[0], cache_swa_v[0], swa_w_qkv[0],
                                            swa_sinks[0], swa_w_out[0], ln1_g[3], ln1_b[3])
    xf, xb = hier_moe_block(xf, 3, *moe)

    y_p = xf[:NP].reshape(BATCH, SEQ, D)
    y_s = _sample_rows(xf)[:, :DEC_SEQ]
    return (y_p, y_s, conv_p[None], conv_s[None], hg_p[None], hg_s[None],
            hr_p[None], hi_p[None], hr_s[None], hi_s[None],
            k_p[None], v_p[None], k_s[None], v_s[None])
```

```python
import functools

import jax
import jax.numpy as jnp
from jax import lax
from jax.experimental import pallas as pl
from jax.experimental.pallas import tpu as pltpu

F32 = jnp.float32
BF16 = jnp.bfloat16

D = 2048
BATCH = 4
SEQ = 2048
DEPTH = 4
DEC_BATCH = 32
DEC_SEQ = 4
PAST_LEN = 16384
CONV_WIDTH = 31
HG_H = 16
HG_DK = 128
HG_DV = 128
SSM_GROUP = 16
SSM_GROUPS = 128
SSM_STATE = 64
HD = 64
NH = 32
NKV = 4
WINDOW = 128
ROPE_THETA = 10000.0
MOE_E = 16
MOE_FF = 512
ALPHA = (2.0 * DEPTH) ** 0.25
LN_EPS = 1e-5
RMS_EPS = 1e-6

TS = 8
NP = BATCH * SEQ
NS = DEC_BATCH * TS
MT = NP + NS
LANES = 128
SUBLANES = 8
VMEM_LIMIT = 56 * 1024 * 1024

TM_MM = 768
TN_MM = 512
TM_LN = 384
TM_E = 512
P_SLOTS = ((2 * MT + MOE_E * (TM_E - 1) + TM_E - 1) // TM_E) * TM_E
N_ETILES = P_SLOTS // TM_E
DMA_UNROLL = 8
FFN_CH = 64


def _cp(*sem):
    return pltpu.CompilerParams(dimension_semantics=sem, vmem_limit_bytes=VMEM_LIMIT)


def _sigmoid(x):
    return 1.0 / (1.0 + jnp.exp(-x))


def _silu(x):
    return x * _sigmoid(x)


def _ln_rows(z, g, b):
    mu = jnp.mean(z, -1, keepdims=True)
    zc = z - mu
    var = jnp.mean(zc * zc, -1, keepdims=True)
    return zc * lax.rsqrt(var + LN_EPS) * g + b


def _cast_weights_once(pairs):
    @pl.when(pl.program_id(1) == 0)
    def _():
        for src, dst in pairs:
            dst[...] = src[...].astype(BF16)


def _mm_glu_kernel(x_ref, wa_ref, wg_ref, ba_ref, bg_ref, o_ref, wab, wgb):
    _cast_weights_once(((wa_ref, wab), (wg_ref, wgb)))
    x = x_ref[...]
    a = jnp.dot(x, wab[...], preferred_element_type=F32) + ba_ref[...]
    g = jnp.dot(x, wgb[...], preferred_element_type=F32) + bg_ref[...]
    o_ref[...] = (a * _sigmoid(g)).astype(o_ref.dtype)


def mm_glu(x, w, b, out_dtype):
    m, k = x.shape
    n = w.shape[1] // 2
    nj = n // TN_MM
    return pl.pallas_call(
        _mm_glu_kernel,
        grid=(nj, m // TM_MM),
        in_specs=[
            pl.BlockSpec((TM_MM, k), lambda j, i: (i, 0)),
            pl.BlockSpec((k, TN_MM), lambda j, i: (0, j)),
            pl.BlockSpec((k, TN_MM), lambda j, i: (0, j + nj)),
            pl.BlockSpec((1, TN_MM), lambda j, i: (0, j)),
            pl.BlockSpec((1, TN_MM), lambda j, i: (0, j + nj)),
        ],
        out_specs=pl.BlockSpec((TM_MM, TN_MM), lambda j, i: (i, j)),
        out_shape=jax.ShapeDtypeStruct((m, n), out_dtype),
        scratch_shapes=[pltpu.VMEM((k, TN_MM), BF16), pltpu.VMEM((k, TN_MM), BF16)],
        compiler_params=_cp("parallel", "arbitrary"),
        name="mm_glu",
    )(x, w, w, b, b)


def _mm_hgrn_kernel(x_ref, w_ref, lb_ref, o_ref, wb, *, nsec):
    j = pl.program_id(0)
    _cast_weights_once(((w_ref, wb),))
    acc = jnp.dot(x_ref[...], wb[...], preferred_element_type=F32)

    @pl.when(j < nsec)
    def _():
        o_ref[...] = _silu(acc)

    @pl.when((j >= nsec) & (j < 2 * nsec))
    def _():
        lb = lb_ref[...]
        o_ref[...] = jnp.log(lb + (1.0 - lb) * _sigmoid(acc))

    @pl.when((j >= 2 * nsec) & (j < 3 * nsec))
    def _():
        o_ref[...] = acc

    @pl.when(j >= 3 * nsec)
    def _():
        o_ref[...] = _silu(acc)


def mm_hgrn(x, w, lb_row):
    m, k = x.shape
    n = w.shape[1]
    tn = 2 * TN_MM
    nsec = (n // 4) // tn
    return pl.pallas_call(
        functools.partial(_mm_hgrn_kernel, nsec=nsec),
        grid=(n // tn, m // TM_MM),
        in_specs=[
            pl.BlockSpec((TM_MM, k), lambda j, i: (i, 0)),
            pl.BlockSpec((k, tn), lambda j, i: (0, j)),
            pl.BlockSpec((1, tn), lambda j, i: (0, j)),
        ],
        out_specs=pl.BlockSpec((TM_MM, tn), lambda j, i: (i, j)),
        out_shape=jax.ShapeDtypeStruct((m, n), F32),
        scratch_shapes=[pltpu.VMEM((k, tn), BF16)],
        compiler_params=_cp("parallel", "arbitrary"),
        name="mm_hgrn",
    )(x, w, lb_row)


def _swap_half_heads(x, lane):
    return jnp.where((lane % HD) < HD // 2, pltpu.roll(x, LANES - HD // 2, 1),
                     pltpu.roll(x, HD // 2, 1))


def _mm_rope_kernel(x_ref, w_ref, cos_ref, sin_ref, o_ref, wb, *, n_rope, scale):
    _cast_weights_once(((w_ref, wb),))
    acc = jnp.dot(x_ref[...], wb[...], preferred_element_type=F32)
    cos = cos_ref[...]
    sin = sin_ref[...]
    lane = lax.broadcasted_iota(jnp.int32, cos.shape, 1)
    for c in range(acc.shape[1] // LANES):
        xc = acc[:, c * LANES:(c + 1) * LANES]
        if c < n_rope:
            xc = (xc * cos + _swap_half_heads(xc, lane) * sin) * scale
        o_ref[:, c * LANES:(c + 1) * LANES] = xc


def mm_rope(x, w, col0, n, cos, sin, n_rope, scale):
    m, k = x.shape
    j0 = col0 // TN_MM
    return pl.pallas_call(
        functools.partial(_mm_rope_kernel, n_rope=n_rope, scale=scale),
        grid=(n // TN_MM, m // TM_MM),
        in_specs=[
            pl.BlockSpec((TM_MM, k), lambda j, i: (i, 0)),
            pl.BlockSpec((k, TN_MM), lambda j, i: (0, j + j0)),
            pl.BlockSpec((TM_MM, LANES), lambda j, i: (i, 0)),
            pl.BlockSpec((TM_MM, LANES), lambda j, i: (i, 0)),
        ],
        out_specs=pl.BlockSpec((TM_MM, TN_MM), lambda j, i: (i, j)),
        out_shape=jax.ShapeDtypeStruct((m, n), F32),
        scratch_shapes=[pltpu.VMEM((k, TN_MM), BF16)],
        compiler_params=_cp("parallel", "arbitrary"),
        name="mm_rope",
    )(x, w, cos, sin)


def _mm_res_ln_kernel(x_ref, w_ref, b_ref, res_ref, g_ref, beta_ref, of_ref, ob_ref):
    y = jnp.dot(x_ref[...], w_ref[...], preferred_element_type=F32) + b_ref[...]
    out = _ln_rows(ALPHA * res_ref[...] + y, g_ref[...], beta_ref[...])
    of_ref[...] = out
    ob_ref[...] = out.astype(BF16)


def mm_res_ln(x, w, b, res, g, beta):
    m, k = x.shape
    row = lambda i: (i, 0)
    fix = lambda i: (0, 0)
    return pl.pallas_call(
        _mm_res_ln_kernel,
        grid=(m // TM_LN,),
        in_specs=[
            pl.BlockSpec((TM_LN, k), row),
            pl.BlockSpec((k, D), fix),
            pl.BlockSpec((1, D), fix),
            pl.BlockSpec((TM_LN, D), row),
            pl.BlockSpec((1, D), fix),
            pl.BlockSpec((1, D), fix),
        ],
        out_specs=[pl.BlockSpec((TM_LN, D), row), pl.BlockSpec((TM_LN, D), row)],
        out_shape=[jax.ShapeDtypeStruct((m, D), F32), jax.ShapeDtypeStruct((m, D), BF16)],
        compiler_params=_cp("parallel"),
        name="mm_res_ln",
    )(x, w, b, res, g, beta)


def _res_ln_kernel(res_ref, y_ref, g_ref, beta_ref, of_ref, ob_ref):
    out = _ln_rows(ALPHA * res_ref[...] + y_ref[...].astype(F32), g_ref[...], beta_ref[...])
    of_ref[...] = out
    ob_ref[...] = out.astype(BF16)


def res_ln(res, y, g, beta):
    m = res.shape[0]
    row = lambda i: (i, 0)
    fix = lambda i: (0, 0)
    return pl.pallas_call(
        _res_ln_kernel,
        grid=(m // TM_LN,),
        in_specs=[pl.BlockSpec((TM_LN, D), row), pl.BlockSpec((TM_LN, D), row),
                  pl.BlockSpec((1, D), fix), pl.BlockSpec((1, D), fix)],
        out_specs=[pl.BlockSpec((TM_LN, D), row), pl.BlockSpec((TM_LN, D), row)],
        out_shape=[jax.ShapeDtypeStruct((m, D), F32), jax.ShapeDtypeStruct((m, D), BF16)],
        compiler_params=_cp("parallel"),
        name="res_ln",
    )(res, y, g, beta)


HALO = 32
CONV_RC = 64
CONV_LC = 256


def _conv_kernel(u_ref, buf_ref, wdw_ref, bdw_ref, g_ref, b_ref, z_ref, ext_ref, sh_ref, c_ref, *, tt):
    i = pl.program_id(1)

    @pl.when(i == 0)
    def _():
        ext_ref[0:HALO, :] = buf_ref[0]

    @pl.when(i > 0)
    def _():
        ext_ref[0:HALO, :] = ext_ref[tt:tt + HALO, :]

    ext_ref[HALO:HALO + tt, :] = u_ref[...]
    nsh = tt + HALO - SUBLANES
    for b in range(1, SUBLANES):
        sh_ref[b - 1] = ext_ref[b:b + nsh, :]
    pad = HALO - (CONV_WIDTH - 1)
    rc = min(CONV_RC, tt)
    for r0 in range(0, tt, rc):
        for c0 in range(0, D, CONV_LC):
            acc = jnp.zeros((rc, CONV_LC), F32) + bdw_ref[:, c0:c0 + CONV_LC]
            for w in range(CONV_WIDTH):
                a, b = divmod(pad + w, SUBLANES)
                rows = slice(r0 + a * SUBLANES, r0 + a * SUBLANES + rc)
                if b == 0:
                    tap = ext_ref[rows, c0:c0 + CONV_LC]
                else:
                    tap = sh_ref[b - 1, rows, c0:c0 + CONV_LC]
                acc = acc + tap * wdw_ref[w:w + 1, c0:c0 + CONV_LC]
            c_ref[r0:r0 + rc, c0:c0 + CONV_LC] = acc
    z_ref[...] = _silu(_ln_rows(c_ref[...], g_ref[...], b_ref[...])).astype(z_ref.dtype)


def conv_ln_silu(u, row0_blocks, nseq, t, tt, buf, wdw, bdw, g, b, out_dtype):
    nt = t // tt
    fix = lambda s, i: (0, 0)
    return pl.pallas_call(
        functools.partial(_conv_kernel, tt=tt),
        grid=(nseq, nt),
        in_specs=[
            pl.BlockSpec((tt, D), lambda s, i: (row0_blocks + s * nt + i, 0)),
            pl.BlockSpec((1, HALO, D), lambda s, i: (s, 0, 0)),
            pl.BlockSpec((CONV_WIDTH, D), fix),
            pl.BlockSpec((1, D), fix),
            pl.BlockSpec((1, D), fix),
            pl.BlockSpec((1, D), fix),
        ],
        out_specs=pl.BlockSpec((tt, D), lambda s, i: (s * nt + i, 0)),
        out_shape=jax.ShapeDtypeStruct((nseq * t, D), out_dtype),
        scratch_shapes=[pltpu.VMEM((HALO + tt, D), F32),
                        pltpu.VMEM((SUBLANES - 1, tt + HALO - SUBLANES, D), F32),
                        pltpu.VMEM((tt, D), F32)],
        compiler_params=_cp("parallel", "arbitrary"),
        name="conv_ln_silu",
    )(u, buf, wdw, bdw, g, b)


def _hgrn_kernel(p_ref, s0_ref, ng_ref, o_ref, sf_ref, st_ref, pst_ref, ob_ref, bl_ref, *, tt, sc, nvalid):
    i = pl.program_id(1)
    hk = HG_H * HG_DK

    @pl.when(i == 0)
    def _():
        for h in range(HG_H):
            st_ref[h * HG_DV:(h + 1) * HG_DV, :] = s0_ref[0, h].T

    ones = jnp.ones((LANES, LANES), BF16)
    row = lax.broadcasted_iota(jnp.int32, (sc, hk), 0)
    first = [(s // SUBLANES) * SUBLANES for s in range(nvalid)]
    offs = [sum(sc - f for f in first[:s]) for s in range(nvalid + 1)]

    if nvalid == sc:
        rr = lax.broadcasted_iota(jnp.int32, (tt, tt), 0)
        cc = lax.broadcasted_iota(jnp.int32, (tt, tt), 1)
        tril = ((cc <= rr) & (cc // sc == rr // sc)).astype(F32)
        for h in range(HG_H):
            sl = slice(h * HG_DK, (h + 1) * HG_DK)
            bl_ref[:, sl] = jnp.dot(tril, p_ref[:, hk + h * HG_DK:hk + (h + 1) * HG_DK],
                                    preferred_element_type=F32, precision=lax.Precision.HIGHEST)

    def sub_chunk(ci, carry):
        r0 = pl.multiple_of(ci * sc, sc)
        q = p_ref[pl.ds(r0, sc), 0:hk]
        lf = p_ref[pl.ds(r0, sc), hk:2 * hk]
        v = p_ref[pl.ds(r0, sc), 2 * hk:3 * hk]
        if nvalid < sc:
            lf = jnp.where(row < nvalid, lf, 0.0)
            bl = jnp.zeros((sc, hk), F32)
            for s in range(nvalid):
                bl = bl + jnp.where(row >= s, lf[s:s + 1, :], 0.0)
        else:
            bl = bl_ref[pl.ds(r0, sc), :]
        kk = 1.0 - jnp.exp(lf)
        btot = bl[sc - 1:sc, :]
        qd = q * jnp.exp(bl)
        kd = kk * jnp.exp(btot - bl)
        for s in range(nvalid):
            f = first[s]
            rowf = lax.broadcasted_iota(jnp.int32, (sc - f, hk), 0) + f
            e = jnp.exp(jnp.where(rowf >= s, bl[f:] - bl[s:s + 1, :], -jnp.inf))
            pst_ref[offs[s]:offs[s + 1], :] = q[f:] * e * kk[s:s + 1, :]
        o_acc = []
        for h in range(HG_H):
            sl = slice(h * HG_DK, (h + 1) * HG_DK)
            rs = jnp.dot(pst_ref[0:offs[nvalid], sl].astype(BF16), ones,
                         preferred_element_type=F32)
            vh = v[:, sl]
            st_h = st_ref[h * HG_DV:(h + 1) * HG_DV, :]
            oh = lax.dot_general(qd[:, sl].astype(BF16), st_h.astype(BF16),
                                 (((1,), (1,)), ((), ())), preferred_element_type=F32)
            tiles = [oh[f:f + SUBLANES] for f in range(0, sc, SUBLANES)]
            for s in range(nvalid):
                for ti, f in enumerate(range(first[s], sc, SUBLANES)):
                    piece = rs[offs[s] + ti * SUBLANES:offs[s] + (ti + 1) * SUBLANES, :]
                    tiles[f // SUBLANES] = tiles[f // SUBLANES] + piece * vh[s:s + 1, :]
            o_acc.append(jnp.concatenate(tiles, axis=0) if len(tiles) > 1 else tiles[0])
            upd = jnp.dot(vh.T.astype(BF16), kd[:, sl].astype(BF16), preferred_element_type=F32)
            st_ref[h * HG_DV:(h + 1) * HG_DV, :] = jnp.exp(btot[:, sl]) * st_h + upd
        ob_ref[pl.ds(r0, sc), :] = jnp.concatenate(o_acc, axis=1)
        return carry

    lax.fori_loop(0, tt // sc, sub_chunk, 0)

    o = ob_ref[...]
    o2 = o * o
    parts = []
    for h in range(HG_H):
        sl = slice(h * HG_DV, (h + 1) * HG_DV)
        ms = jnp.dot(o2[:, sl].astype(BF16), ones, preferred_element_type=F32) * (1.0 / HG_DV)
        parts.append(o[:, sl] * lax.rsqrt(ms + RMS_EPS))
    on = jnp.concatenate(parts, axis=1) * ng_ref[...] * p_ref[:, 3 * hk:4 * hk]
    o_ref[...] = on.astype(o_ref.dtype)

    @pl.when(i == pl.num_programs(1) - 1)
    def _():
        for h in range(HG_H):
            sf_ref[0, h] = st_ref[h * HG_DV:(h + 1) * HG_DV, :].T


def hgrn_scan(p, row0_blocks, nseq, t, tt, sc, nvalid, s0, norm_g, out_dtype):
    nt = t // tt
    return pl.pallas_call(
        functools.partial(_hgrn_kernel, tt=tt, sc=sc, nvalid=nvalid),
        grid=(nseq, nt),
        in_specs=[
            pl.BlockSpec((tt, 4 * D), lambda s, i: (row0_blocks + s * nt + i, 0)),
            pl.BlockSpec((1, HG_H, HG_DK, HG_DV), lambda s, i: (s, 0, 0, 0)),
            pl.BlockSpec((1, D), lambda s, i: (0, 0)),
        ],
        out_specs=[
            pl.BlockSpec((tt, D), lambda s, i: (s * nt + i, 0)),
            pl.BlockSpec((1, HG_H, HG_DK, HG_DV), lambda s, i: (s, 0, 0, 0)),
        ],
        out_shape=[jax.ShapeDtypeStruct((nseq * t, D), out_dtype),
                   jax.ShapeDtypeStruct((nseq, HG_H, HG_DK, HG_DV), F32)],
        scratch_shapes=[pltpu.VMEM((HG_H * HG_DV, HG_DK), F32),
                        pltpu.VMEM((sc * sc, D), F32),
                        pltpu.VMEM((tt, D), F32),
                        pltpu.VMEM((tt, D), F32)],
        compiler_params=_cp("parallel", "arbitrary"),
        name="hgrn_scan",
    )(p, s0, norm_g)


S5_GPB = LANES // SSM_GROUP
S5_NJ = D // LANES
S5_SW = S5_GPB * SSM_STATE
S5_LC = 8


def _spread_groups(blk, rep, mask):
    full = jnp.dot(blk.astype(BF16), rep, preferred_element_type=F32)
    return jnp.where(mask, full, 0.0).astype(BF16)


def _group_mask(rows, row_w, cols, col_w):
    r = lax.broadcasted_iota(jnp.int32, (rows, cols), 0) // row_w
    c = lax.broadcasted_iota(jnp.int32, (rows, cols), 1) // col_w
    return r == c


def _rep_matrix(w, cols):
    r = lax.broadcasted_iota(jnp.int32, (w, cols), 0)
    c = lax.broadcasted_iota(jnp.int32, (w, cols), 1) % w
    return (r == c).astype(BF16)


def _s5_kernel(u_ref, h0_ref, klt_ref, wtr_ref, wti_ref, car_ref, cai_ref, ar_ref, ai_ref, y_ref, hf_ref,
               t_s, bp_s, cp_s, con_ref, hp_ref, *, nc, nv):
    sw = S5_SW
    lc = S5_LC
    c, p = SSM_GROUP, SSM_STATE
    rep_c = _rep_matrix(c, LANES)
    rep_p = _rep_matrix(p, sw)
    t_s[...] = jnp.zeros(t_s.shape, BF16)
    mask_t = _group_mask(LANES, c, LANES, c)
    for l in range(lc):
        blk = _spread_groups(klt_ref[0, l], rep_c, mask_t)
        for s in range(lc - l):
            t_s[s * LANES:(s + 1) * LANES, (s + l) * LANES:(s + l + 1) * LANES] = blk
    mask_b = _group_mask(LANES, c, sw, p)
    for s in range(lc):
        if s < nv:
            bp_s[s * LANES:(s + 1) * LANES, 0:sw] = _spread_groups(wtr_ref[0, nv - 1 - s], rep_p, mask_b)
            bp_s[s * LANES:(s + 1) * LANES, sw:2 * sw] = _spread_groups(wti_ref[0, nv - 1 - s], rep_p, mask_b)
        else:
            bp_s[s * LANES:(s + 1) * LANES, :] = jnp.zeros((LANES, 2 * sw), BF16)
    mask_c = _group_mask(sw, p, LANES, c)
    for t in range(lc):
        cp_s[0:sw, t * LANES:(t + 1) * LANES] = _spread_groups(car_ref[0, t], rep_c, mask_c)
        cp_s[sw:2 * sw, t * LANES:(t + 1) * LANES] = _spread_groups(cai_ref[0, t], rep_c, mask_c)

    u = u_ref[0]
    con_ref[...] = jnp.dot(u, bp_s[...], preferred_element_type=F32)
    ar = ar_ref[0]
    ai = ai_ref[0]

    def step(hr, hi, cr, ci):
        return ar * hr - ai * hi + cr, ar * hi + ai * hr + ci

    if nc == 1:
        hr, hi = h0_ref[0, :, 0:sw], h0_ref[0, :, sw:2 * sw]
        hp_ref[:, 0:sw] = hr
        hp_ref[:, sw:2 * sw] = hi
        hr, hi = step(hr, hi, con_ref[:, 0:sw], con_ref[:, sw:2 * sw])
    else:
        half = 4
        top = lax.broadcasted_iota(jnp.int32, (2 * half, sw), 0) < half

        def pair(k, carry):
            hr, hi = carry
            r0 = pl.multiple_of(k * 2 * half, 2 * half)
            cr = con_ref[pl.ds(r0, 2 * half), 0:sw]
            ci = con_ref[pl.ds(r0, 2 * half), sw:2 * sw]
            ar1, ai1 = step(hr, hi, cr, ci)
            ar1 = jnp.where(top, ar1, pltpu.roll(ar1, half, 0))
            ai1 = jnp.where(top, ai1, pltpu.roll(ai1, half, 0))
            hp_ref[pl.ds(r0, 2 * half), 0:sw] = jnp.where(top, hr, ar1)
            hp_ref[pl.ds(r0, 2 * half), sw:2 * sw] = jnp.where(top, hi, ai1)
            br, bi = step(ar1, ai1, cr, ci)
            br = jnp.where(top, pltpu.roll(br, half, 0), br)
            bi = jnp.where(top, pltpu.roll(bi, half, 0), bi)
            return br, bi

        hr, hi = lax.fori_loop(0, nc // 2, pair, (h0_ref[0, :, 0:sw], h0_ref[0, :, sw:2 * sw]))
    hf_ref[0, :, 0:sw] = hr
    hf_ref[0, :, sw:2 * sw] = hi
    y = (jnp.dot(u, t_s[...], preferred_element_type=F32)
         + jnp.dot(hp_ref[...].astype(BF16), cp_s[...], preferred_element_type=F32))
    y_ref[0] = jax.nn.gelu(y).astype(y_ref.dtype)


def s5_lti(u, h0, ops, nc, nv):
    klt, wtr, wti, car, cai, pw_re, pw_im = ops
    nj, rows, lw = u.shape
    hrows = h0.shape[1]
    sw, lc, c, p = S5_SW, S5_LC, SSM_GROUP, SSM_STATE
    ar = pw_re[:, nv].reshape(nj, 1, sw)
    ai = pw_im[:, nv].reshape(nj, 1, sw)
    blk = lambda j: (j, 0, 0)
    blk4 = lambda j: (j, 0, 0, 0)
    return pl.pallas_call(
        functools.partial(_s5_kernel, nc=nc, nv=nv),
        grid=(nj,),
        in_specs=[
            pl.BlockSpec((1, rows, lw), blk),
            pl.BlockSpec((1, hrows, 2 * sw), blk),
            pl.BlockSpec((1, lc, LANES, c), blk4),
            pl.BlockSpec((1, lc, LANES, p), blk4),
            pl.BlockSpec((1, lc, LANES, p), blk4),
            pl.BlockSpec((1, lc, sw, c), blk4),
            pl.BlockSpec((1, lc, sw, c), blk4),
            pl.BlockSpec((1, 1, sw), blk),
            pl.BlockSpec((1, 1, sw), blk),
        ],
        out_specs=[pl.BlockSpec((1, rows, lw), blk), pl.BlockSpec((1, hrows, 2 * sw), blk)],
        out_shape=[jax.ShapeDtypeStruct((nj, rows, lw), BF16),
                   jax.ShapeDtypeStruct((nj, hrows, 2 * sw), F32)],
        scratch_shapes=[pltpu.VMEM((lw, lw), BF16), pltpu.VMEM((lw, 2 * sw), BF16),
                        pltpu.VMEM((2 * sw, lw), BF16),
                        pltpu.VMEM((rows, 2 * sw), F32), pltpu.VMEM((rows, 2 * sw), F32)],
        compiler_params=_cp("parallel"),
        name="s5_lti",
    )(u, h0, klt, wtr, wti, car, cai, ar, ai)


def _s5_operators(lam_re, lam_im, log_dt, b_re, b_im, c_re, c_im, d_skip):
    hp = lax.Precision.HIGHEST
    gpb, nj, c, p, lc = S5_GPB, S5_NJ, SSM_GROUP, SSM_STATE, S5_LC
    dt = jnp.exp(log_dt)[:, None]
    lr, li = lam_re, lam_im
    mag = jnp.exp(lr * dt)
    ab_re, ab_im = mag * jnp.cos(li * dt), mag * jnp.sin(li * dt)
    nr, ni = ab_re - 1.0, ab_im
    den = lr * lr + li * li
    z_re, z_im = (nr * lr + ni * li) / den, (ni * lr - nr * li) / den
    bbt_re = (z_re[..., None] * b_re - z_im[..., None] * b_im).transpose(0, 2, 1)
    bbt_im = (z_re[..., None] * b_im + z_im[..., None] * b_re).transpose(0, 2, 1)
    l = jnp.arange(lc + 1, dtype=F32)[None, :, None]
    pmag = jnp.exp(l * (lr * dt)[:, None, :])
    pw_re = pmag * jnp.cos(l * (li * dt)[:, None, :])
    pw_im = pmag * jnp.sin(l * (li * dt)[:, None, :])
    wt_re = pw_re[:, :lc, None, :] * bbt_re[:, None] - pw_im[:, :lc, None, :] * bbt_im[:, None]
    wt_im = pw_re[:, :lc, None, :] * bbt_im[:, None] + pw_im[:, :lc, None, :] * bbt_re[:, None]
    klt = (jnp.einsum('glcp,gop->glco', wt_re, c_re, precision=hp)
           - jnp.einsum('glcp,gop->glco', wt_im, c_im, precision=hp))
    klt = klt.at[:, 0].add(d_skip.reshape(SSM_GROUPS, c)[:, :, None] * jnp.eye(c, dtype=F32)[None])
    ct_re, ct_im = c_re.transpose(0, 2, 1)[:, None], c_im.transpose(0, 2, 1)[:, None]
    ca_re = ct_re * pw_re[:, 1:, :, None] - ct_im * pw_im[:, 1:, :, None]
    ca_im = ct_re * pw_im[:, 1:, :, None] + ct_im * pw_re[:, 1:, :, None]

    def by_block(a):
        _, ll, r, w = a.shape
        return a.reshape(nj, gpb, ll, r, w).transpose(0, 2, 1, 3, 4).reshape(nj, ll, gpb * r, w)

    def pw_block(a):
        return a.reshape(nj, gpb, lc + 1, p).transpose(0, 2, 1, 3).reshape(nj, lc + 1, gpb * p)

    return (by_block(klt), by_block(wt_re), by_block(wt_im), by_block(ca_re), by_block(-ca_im),
            pw_block(pw_re), pw_block(pw_im))


def _attn_kernel(q_ref, kv_ref, pk_ref, pv_ref, sink_ref, o_ref, *, tq, nblk, always_prev):
    idx = pl.program_id(0)
    kvw = NKV * HD
    group = NH // NKV
    pkn = pk_ref.shape[1]
    to = kv_ref.shape[1]
    nk = pkn + to
    rows = group * tq
    a = lax.broadcasted_iota(jnp.int32, (rows, nk), 0) % tq
    col = lax.broadcasted_iota(jnp.int32, (rows, nk), 1)
    prev_ok = col >= a
    if not always_prev:
        prev_ok = prev_ok & ((idx % nblk) > 0)
    mask = ((col < pkn) & prev_ok) | ((col >= pkn) & ((col - pkn) <= a))
    hrow = lax.broadcasted_iota(jnp.int32, (rows, 1), 0) // tq
    low = lax.broadcasted_iota(jnp.int32, (tq, LANES), 1) < HD
    nt = (((1,), (1,)), ((), ()))
    for kb in range(NKV // 2):
        lanes = slice(kb * LANES, (kb + 1) * LANES)
        kcat = jnp.concatenate([pk_ref[0, :, lanes], kv_ref[0, :, lanes]], axis=0).astype(BF16)
        vcat = jnp.concatenate([pv_ref[0, :, lanes],
                                kv_ref[0, :, kvw + kb * LANES:kvw + (kb + 1) * LANES]], axis=0).astype(BF16)
        for hi in range(2):
            j = 2 * kb + hi
            parts = []
            sk = jnp.zeros((rows, 1), F32)
            for h8 in range(group):
                qb, e = (group * j + h8) // 2, h8 % 2
                q2 = q_ref[0, :, qb * LANES:(qb + 1) * LANES]
                qe = jnp.where(low if e == 0 else jnp.logical_not(low), q2, 0.0)
                if e != hi:
                    qe = pltpu.roll(qe, HD, 1)
                parts.append(qe)
                sk = jnp.where(hrow == h8, sink_ref[group * j + h8], sk)
            q8 = jnp.concatenate(parts, axis=0).astype(BF16)
            s = lax.dot_general(q8, kcat, nt, preferred_element_type=F32)
            s = jnp.where(mask, s, -jnp.inf)
            m = jnp.maximum(jnp.max(s, -1, keepdims=True), sk)
            p = jnp.exp(s - m)
            inv = 1.0 / (jnp.sum(p, -1, keepdims=True) + jnp.exp(sk - m))
            o8 = jnp.dot((p * inv).astype(BF16), vcat, preferred_element_type=F32)
            for pr in range(group // 2):
                qb = (group * j) // 2 + pr
                pair = []
                for e in range(2):
                    oe = o8[(2 * pr + e) * tq:(2 * pr + e + 1) * tq, :]
                    if e != hi:
                        oe = pltpu.roll(oe, HD, 1)
                    pair.append(oe)
                o_ref[0, :, qb * LANES:(qb + 1) * LANES] = jnp.where(low, pair[0], pair[1]).astype(o_ref.dtype)


def swa_attention(q3, kv3, pk3, pv3, nb, blk0, pk_map, pv_map, sinks, nblk, always_prev):
    tq = q3.shape[1]
    to = kv3.shape[1]
    kvw = NKV * HD
    return pl.pallas_call(
        functools.partial(_attn_kernel, tq=tq, nblk=nblk, always_prev=always_prev),
        grid=(nb,),
        in_specs=[
            pl.BlockSpec((1, tq, D), lambda i: (i + blk0, 0, 0)),
            pl.BlockSpec((1, to, 2 * kvw), lambda i: (i, 0, 0)),
            pl.BlockSpec((1, pk3.shape[1], kvw), pk_map),
            pl.BlockSpec((1, pv3.shape[1], kvw), pv_map),
            pl.BlockSpec(memory_space=pltpu.SMEM),
        ],
        out_specs=pl.BlockSpec((1, tq, D), lambda i: (i, 0, 0)),
        out_shape=jax.ShapeDtypeStruct((nb, tq, D), BF16),
        compiler_params=_cp("parallel"),
        name="swa_attention",
    )(q3, kv3, pk3, pv3, sinks)


R_E1, R_E2, R_W1, R_W2, R_R1, R_R2 = 0, 1, 2, 3, 4, 5


def _router_kernel(x_ref, w_ref, b_ref, o_ref, cnt_ref, carry_ref):
    i = pl.program_id(0)

    @pl.when(i == 0)
    def _():
        carry_ref[...] = jnp.zeros(carry_ref.shape, F32)

    l = jnp.dot(x_ref[...], w_ref[...], preferred_element_type=F32,
                precision=lax.Precision.HIGHEST) + b_ref[...]
    tm = l.shape[0]
    lane = lax.broadcasted_iota(jnp.int32, l.shape, 1).astype(F32)
    big = 1e9
    ninf = -jnp.inf
    gmask = lane < 4.0
    gl = jnp.where(gmask, l, ninf)
    gm = jnp.max(gl, -1, keepdims=True)
    gsel = jnp.min(jnp.where(gl == gm, lane, big), -1, keepdims=True)
    gsum = jnp.sum(jnp.where(gmask, jnp.exp(gl - gm), 0.0), -1, keepdims=True)
    gval = 1.0 / gsum
    lo = 4.0 + 4.0 * gsel
    emask = (lane >= lo) & (lane < lo + 4.0)
    el = jnp.where(emask, l, ninf)
    m1 = jnp.max(el, -1, keepdims=True)
    i1 = jnp.min(jnp.where(emask & (el == m1), lane, big), -1, keepdims=True)
    emask2 = emask & (lane != i1)
    el2 = jnp.where(emask2, l, ninf)
    m2 = jnp.max(el2, -1, keepdims=True)
    i2 = jnp.min(jnp.where(emask2 & (el2 == m2), lane, big), -1, keepdims=True)
    r = jnp.exp(m2 - m1)
    w1 = gval / (1.0 + r)
    w2 = gval * r / (1.0 + r)
    e1 = i1 - 4.0
    e2 = i2 - 4.0
    oh = ((lane == e1) | (lane == e2)).astype(F32)
    tri = (lax.broadcasted_iota(jnp.int32, (tm, tm), 1)
           < lax.broadcasted_iota(jnp.int32, (tm, tm), 0)).astype(BF16)
    before = jnp.dot(tri, oh.astype(BF16), preferred_element_type=F32) + carry_ref[...]
    r1 = jnp.sum(jnp.where(lane == e1, before, 0.0), -1, keepdims=True)
    r2 = jnp.sum(jnp.where(lane == e2, before, 0.0), -1, keepdims=True)
    carry_ref[...] = carry_ref[...] + jnp.sum(oh, axis=0, keepdims=True)
    cnt_ref[...] = carry_ref[...]
    rec = jnp.zeros(l.shape, F32)
    for ln, val in ((R_E1, e1), (R_E2, e2), (R_W1, w1), (R_W2, w2), (R_R1, r1), (R_R2, r2)):
        rec = jnp.where(lane == float(ln), val, rec)
    o_ref[...] = rec


def moe_router(x, w, b):
    m = x.shape[0]
    return pl.pallas_call(
        _router_kernel,
        grid=(m // TM_LN,),
        in_specs=[pl.BlockSpec((TM_LN, D), lambda i: (i, 0)),
                  pl.BlockSpec((D, LANES), lambda i: (0, 0)),
                  pl.BlockSpec((1, LANES), lambda i: (0, 0))],
        out_specs=[pl.BlockSpec((TM_LN, LANES), lambda i: (i, 0)),
                   pl.BlockSpec((1, LANES), lambda i: (0, 0))],
        out_shape=[jax.ShapeDtypeStruct((m, LANES), F32), jax.ShapeDtypeStruct((1, LANES), F32)],
        scratch_shapes=[pltpu.VMEM((1, LANES), F32)],
        compiler_params=_cp("arbitrary"),
        name="moe_router",
    )(x, w, b)


def _slots_kernel(rec_ref, starts_ref, o_ref):
    rec = rec_ref[...]
    lane = lax.broadcasted_iota(jnp.int32, rec.shape, 1).astype(F32)
    starts = starts_ref[...]
    out = jnp.zeros(rec.shape, F32)
    for k, (le, lr) in enumerate(((R_E1, R_R1), (R_E2, R_R2))):
        e = rec[:, le:le + 1]
        s = jnp.sum(jnp.where(lane == e, starts, 0.0), -1, keepdims=True) + rec[:, lr:lr + 1]
        out = jnp.where(lane == float(k), s, out)
    o_ref[...] = out


def moe_slots(rec, starts_row):
    m = rec.shape[0]
    return pl.pallas_call(
        _slots_kernel,
        grid=(m // TM_MM,),
        in_specs=[pl.BlockSpec((TM_MM, LANES), lambda i: (i, 0)), pl.BlockSpec((1, LANES), lambda i: (0, 0))],
        out_specs=pl.BlockSpec((TM_MM, LANES), lambda i: (i, 0)),
        out_shape=jax.ShapeDtypeStruct((m, LANES), F32),
        compiler_params=_cp("parallel"),
        name="moe_slots",
    )(rec, starts_row)


def _row_gather_start(idx_ref, base, n, src_hbm, dst, sem):
    def body(it, carry):
        for u in range(DMA_UNROLL):
            r = it * DMA_UNROLL + u
            pltpu.make_async_copy(src_hbm.at[pl.ds(idx_ref[base + r], 1)], dst.at[pl.ds(r, 1)], sem).start()
        return carry
    lax.fori_loop(0, n // DMA_UNROLL, body, 0)


def _row_gather_wait(n, src_hbm, dst, sem):
    pltpu.make_async_copy(src_hbm.at[pl.ds(0, n)], dst.at[pl.ds(0, n)], sem).wait()


def _ffn_kernel(te_ref, nt_ref, nch_ref, src_ref, x_hbm, w1_ref, w3_ref, w2_ref, y_ref, xbuf, sem,
                w1b, w3b, w2b):
    i = pl.program_id(0)
    nt = nt_ref[0]
    slot = i % 2

    @pl.when(i == 0)
    def _():
        xbuf[...] = jnp.zeros(xbuf.shape, F32)
        _row_gather_start(src_ref, 0, nch_ref[0] * FFN_CH, x_hbm, xbuf.at[0], sem.at[0])

    @pl.when(i + 1 < nt)
    def _():
        _row_gather_start(src_ref, (i + 1) * TM_E, nch_ref[i + 1] * FFN_CH, x_hbm,
                          xbuf.at[1 - slot], sem.at[1 - slot])

    @pl.when(i < nt)
    def _():
        prev = te_ref[jnp.maximum(i - 1, 0)]

        @pl.when((i == 0) | (te_ref[i] != prev))
        def _():
            w1b[...] = w1_ref[0, 0].astype(BF16)
            w3b[...] = w3_ref[0, 0].astype(BF16)
            w2b[...] = w2_ref[0, 0].astype(BF16)

        def wait_chunk(c, carry):
            _row_gather_wait(FFN_CH, x_hbm, xbuf.at[slot], sem.at[slot])
            return carry
        lax.fori_loop(0, nch_ref[i], wait_chunk, 0)
        x = xbuf[slot].astype(BF16)
        h1 = jnp.dot(x, w1b[...], preferred_element_type=F32)
        h3 = jnp.dot(x, w3b[...], preferred_element_type=F32)
        h = _silu(h1) * h3
        y_ref[...] = jnp.dot(h.astype(BF16), w2b[...], preferred_element_type=F32)

    @pl.when(i >= nt)
    def _():
        y_ref[...] = jnp.zeros(y_ref.shape, y_ref.dtype)


def moe_ffn(layer, tile_expert, n_tiles, n_chunks, src, x, w1, w3, w2):
    wmap = lambda i, te, nt, nch, src: (layer, te[i], 0, 0)
    grid_spec = pltpu.PrefetchScalarGridSpec(
        num_scalar_prefetch=4,
        grid=(N_ETILES,),
        in_specs=[
            pl.BlockSpec(memory_space=pl.ANY),
            pl.BlockSpec((1, 1, D, MOE_FF), wmap),
            pl.BlockSpec((1, 1, D, MOE_FF), wmap),
            pl.BlockSpec((1, 1, MOE_FF, D), wmap),
        ],
        out_specs=pl.BlockSpec((TM_E, D), lambda i, te, nt, nch, src: (i, 0)),
        scratch_shapes=[pltpu.VMEM((2, TM_E, D), F32), pltpu.SemaphoreType.DMA((2,)),
                        pltpu.VMEM((D, MOE_FF), BF16), pltpu.VMEM((D, MOE_FF), BF16),
                        pltpu.VMEM((MOE_FF, D), BF16)],
    )
    return pl.pallas_call(
        _ffn_kernel,
        grid_spec=grid_spec,
        out_shape=jax.ShapeDtypeStruct((P_SLOTS, D), F32),
        compiler_params=_cp("arbitrary"),
        name="moe_ffn",
    )(tile_expert, n_tiles, n_chunks, src, x, w1, w3, w2)


def _combine_ln_kernel(pos_ref, res_ref, rec_ref, g_ref, beta_ref, ys_hbm, of_ref, ob_ref, ybuf, sem):
    i = pl.program_id(0)
    slot = i % 2
    n = 2 * TM_LN

    @pl.when(i == 0)
    def _():
        _row_gather_start(pos_ref, 0, n, ys_hbm, ybuf.at[0], sem.at[0])

    @pl.when(i + 1 < pl.num_programs(0))
    def _():
        _row_gather_start(pos_ref, (i + 1) * n, n, ys_hbm, ybuf.at[1 - slot], sem.at[1 - slot])

    _row_gather_wait(n, ys_hbm, ybuf.at[slot], sem.at[slot])
    rec = rec_ref[...]
    y = (rec[:, R_W1:R_W1 + 1] * ybuf[slot, 0:TM_LN, :]
         + rec[:, R_W2:R_W2 + 1] * ybuf[slot, TM_LN:2 * TM_LN, :])
    out = _ln_rows(ALPHA * res_ref[...] + y, g_ref[...], beta_ref[...])
    of_ref[...] = out
    ob_ref[...] = out.astype(BF16)


def combine_ln(pos_tiles, res, rec, g, beta, ys):
    m = res.shape[0]
    row = lambda i, pos: (i, 0)
    fix = lambda i, pos: (0, 0)
    grid_spec = pltpu.PrefetchScalarGridSpec(
        num_scalar_prefetch=1,
        grid=(m // TM_LN,),
        in_specs=[pl.BlockSpec((TM_LN, D), row), pl.BlockSpec((TM_LN, LANES), row),
                  pl.BlockSpec((1, D), fix), pl.BlockSpec((1, D), fix),
                  pl.BlockSpec(memory_space=pl.ANY)],
        out_specs=[pl.BlockSpec((TM_LN, D), row), pl.BlockSpec((TM_LN, D), row)],
        scratch_shapes=[pltpu.VMEM((2, 2 * TM_LN, D), F32), pltpu.SemaphoreType.DMA((2,))],
    )
    return pl.pallas_call(
        _combine_ln_kernel,
        grid_spec=grid_spec,
        out_shape=[jax.ShapeDtypeStruct((m, D), F32), jax.ShapeDtypeStruct((m, D), BF16)],
        compiler_params=_cp("arbitrary"),
        name="combine_ln",
    )(pos_tiles, res, rec, g, beta, ys)


def _dispatch_plan(rec, counts_row):
    counts = counts_row[0, :MOE_E].astype(jnp.int32)
    padded = ((counts + TM_E - 1) // TM_E) * TM_E
    ends = jnp.cumsum(padded)
    starts = ends - padded
    starts_row = jnp.pad(starts.astype(F32), (0, LANES - MOE_E))[None, :]
    pos = moe_slots(rec, starts_row)[:, 0:2].astype(jnp.int32)
    tok = jnp.broadcast_to(jnp.arange(MT, dtype=jnp.int32)[:, None], (MT, 2))
    src = jnp.zeros((P_SLOTS,), jnp.int32).at[pos.reshape(-1)].set(tok.reshape(-1))
    n_tiles = (ends[-1] // TM_E).astype(jnp.int32)
    tile_start = jnp.arange(N_ETILES, dtype=jnp.int32) * TM_E
    te = jnp.sum((tile_start[:, None] >= ends[None, :]).astype(jnp.int32), axis=1)
    last = jnp.sum(jnp.where(jnp.arange(N_ETILES) == n_tiles - 1, te, 0))
    live = jnp.arange(N_ETILES) < n_tiles
    te = jnp.minimum(jnp.where(live, te, last), MOE_E - 1).astype(jnp.int32)
    ids = jnp.arange(MOE_E, dtype=jnp.int32)[None, :]
    seg_end = jnp.sum(jnp.where(te[:, None] == ids, (starts + counts)[None, :], 0), axis=1)
    n_rows = jnp.clip(seg_end - tile_start, 0, TM_E)
    n_chunks = jnp.where(live, (n_rows + FFN_CH - 1) // FFN_CH, 0).astype(jnp.int32)
    pos_tiles = pos.reshape(MT // TM_LN, TM_LN, 2).transpose(0, 2, 1).reshape(-1)
    return src, te, n_tiles.reshape(1), n_chunks, pos_tiles


def hier_moe_block(xf, i, moe_w_group, moe_b_group, moe_w_egate, moe_b_egate,
                   moe_w1, moe_w3, moe_w2, ln_g, ln_b):
    wr = jnp.concatenate([moe_w_group[i], moe_w_egate[i].transpose(1, 0, 2).reshape(D, MOE_E)], axis=1)
    wr = jnp.pad(wr, ((0, 0), (0, LANES - wr.shape[1])))
    br = jnp.concatenate([moe_b_group[i], moe_b_egate[i].reshape(MOE_E)])
    br = jnp.pad(br, (0, LANES - br.shape[0]))[None, :]
    rec, counts = moe_router(xf, wr, br)
    src, te, n_tiles, n_chunks, pos_tiles = _dispatch_plan(rec, counts)
    ys = moe_ffn(i, te, n_tiles, n_chunks, src, xf, moe_w1, moe_w3, moe_w2)
    return combine_ln(pos_tiles, xf, rec, ln_g[i][None, :], ln_b[i][None, :], ys)


def _sample_rows(a):
    return a[NP:].reshape(DEC_BATCH, TS, a.shape[-1])


def _prompt_tail(a, n):
    return jnp.stack([a[(b + 1) * SEQ - n:(b + 1) * SEQ] for b in range(BATCH)])


def conv_layer(xf, xb, state, w_in, b_in, w_dw, b_dw, g, b, w_out, b_out, ln_g, ln_b):
    u = mm_glu(xb, w_in, b_in[None, :], F32)
    tt_p = 256
    zbuf = jnp.zeros((BATCH, HALO, D), F32)
    z_p = conv_ln_silu(u, 0, BATCH, SEQ, tt_p, zbuf, w_dw, b_dw[None, :], g[None, :], b[None, :], BF16)
    pad = HALO - (CONV_WIDTH - 1)
    sbuf = jnp.pad(state, ((0, 0), (pad, 0), (0, 0)))
    z_s = conv_ln_silu(u, NP // TS, DEC_BATCH, TS, TS, sbuf, w_dw, b_dw[None, :], g[None, :], b[None, :], F32)
    z = jnp.concatenate([z_p, z_s.astype(BF16)], axis=0)
    xf, xb = mm_res_ln(z, w_out.astype(BF16), b_out[None, :], xf, ln_g[None, :], ln_b[None, :])
    new_p = _prompt_tail(u, CONV_WIDTH - 1)
    new_s = jnp.concatenate([state[:, DEC_SEQ:], _sample_rows(u)[:, :DEC_SEQ]], axis=1)
    return xf, xb, new_p, new_s


def hgrn_layer(xf, xb, state, w_in, lb, norm_g, w_out, ln_g, ln_b):
    hk = HG_H * HG_DK
    lb_row = jnp.concatenate([jnp.zeros((hk,), F32), lb, jnp.zeros((2 * hk,), F32)])[None, :]
    p = mm_hgrn(xb, w_in, lb_row)
    s0_p = jnp.zeros((BATCH, HG_H, HG_DK, HG_DV), F32)
    o_p, sf_p = hgrn_scan(p, 0, BATCH, SEQ, 128, 16, 16, s0_p, norm_g[None, :], BF16)
    o_s, sf_s = hgrn_scan(p, NP // TS, DEC_BATCH, TS, TS, TS, DEC_SEQ, state, norm_g[None, :], F32)
    o = jnp.concatenate([o_p, o_s.astype(BF16)], axis=0)
    zero_b = jnp.zeros((1, D), F32)
    xf, xb = mm_res_ln(o, w_out.astype(BF16), zero_b, xf, ln_g[None, :], ln_b[None, :])
    return xf, xb, sf_p, sf_s


def s5_layer(xf, xb, h_re, h_im, lam_re, lam_im, log_dt, b_re, b_im, c_re, c_im, d_skip, w_glu,
             ln_g, ln_b):
    nj, sw, lc = S5_NJ, S5_SW, S5_LC
    ops = _s5_operators(lam_re, lam_im, log_dt, b_re, b_im, c_re, c_im, d_skip)
    nc = SEQ // lc
    up = xb[:NP].reshape(BATCH, nc, lc, nj, LANES).transpose(3, 1, 0, 2, 4).reshape(nj, nc * BATCH, lc * LANES)
    y_p, hf_p = s5_lti(up, jnp.zeros((nj, 2 * BATCH, 2 * sw), F32), ops, nc, lc)
    y_p = y_p.reshape(nj, nc, BATCH, lc, LANES).transpose(2, 1, 3, 0, 4).reshape(NP, D)
    hf_p = hf_p[:, :BATCH].transpose(1, 0, 2)
    us = _sample_rows(xb).reshape(DEC_BATCH, TS, nj, LANES).transpose(2, 0, 1, 3).reshape(nj, DEC_BATCH, TS * LANES)
    h0 = jnp.concatenate([h_re.reshape(DEC_BATCH, nj, sw), h_im.reshape(DEC_BATCH, nj, sw)], -1).transpose(1, 0, 2)
    y_s, hf_s = s5_lti(us, h0, ops, 1, DEC_SEQ)
    y_s = y_s.reshape(nj, DEC_BATCH, TS, LANES).transpose(1, 2, 0, 3).reshape(NS, D)
    hf_s = hf_s.transpose(1, 0, 2)
    y = jnp.concatenate([y_p, y_s], axis=0)
    m = mm_glu(y, w_glu, jnp.zeros((1, 2 * D), F32), BF16)
    xf, xb = res_ln(xf, m, ln_g[None, :], ln_b[None, :])
    st = lambda h, n: (h[..., :sw].reshape(n, SSM_GROUPS, SSM_STATE), h[..., sw:].reshape(n, SSM_GROUPS, SSM_STATE))
    return (xf, xb) + st(hf_p, BATCH) + st(hf_s, DEC_BATCH)


def _rope_tables(pos):
    half = HD // 2
    inv = ROPE_THETA ** (-jnp.arange(half, dtype=F32) / half)
    ang = pos.astype(F32)[:, None] * inv[None, :]
    cos, sin = jnp.cos(ang), jnp.sin(ang)
    cos_t = jnp.tile(jnp.concatenate([cos, cos], -1), (1, LANES // HD))
    sin_t = jnp.tile(jnp.concatenate([-sin, sin], -1), (1, LANES // HD))
    return cos_t, sin_t


def swa_layer(xf, xb, cache_k, cache_v, w_qkv, sinks, w_out, ln_g, ln_b):
    hq = NH * HD
    kvw = NKV * HD
    pos = jnp.concatenate([jnp.tile(jnp.arange(SEQ), BATCH),
                           jnp.tile(PAST_LEN + jnp.arange(TS), DEC_BATCH)])
    cos_t, sin_t = _rope_tables(pos)
    q = mm_rope(xb, w_qkv, 0, hq, cos_t, sin_t, TN_MM // LANES, HD ** -0.5)
    kv = mm_rope(xb, w_qkv, hq, 2 * kvw, cos_t, sin_t, kvw // LANES, 1.0)
    nblk = SEQ // WINDOW
    qw = q.reshape(MT // WINDOW, WINDOW, D)
    kvwin = kv.reshape(MT // WINDOW, WINDOW, 2 * kvw)
    prev = lambda i: (jnp.maximum(i - 1, 0), 0, 0)
    prev_v = lambda i: (jnp.maximum(i - 1, 0), 0, 1)
    o_p = swa_attention(qw, kvwin, kvwin, kvwin, BATCH * nblk, 0, prev, prev_v, sinks, nblk, False).reshape(NP, D)
    kvs = _sample_rows(kv)
    kvs_pad = jnp.pad(kvs, ((0, 0), (0, WINDOW - TS), (0, 0)))
    ck = cache_k.reshape(DEC_BATCH, WINDOW, kvw)
    cv = cache_v.reshape(DEC_BATCH, WINDOW, kvw)
    same = lambda i: (i, 0, 0)
    o_s = swa_attention(q.reshape(MT // TS, TS, D), kvs_pad, ck, cv, DEC_BATCH, NP // TS, same, same,
                        sinks, 1, True).reshape(NS, D)
    o = jnp.concatenate([o_p, o_s], axis=0)
    xf, xb = mm_res_ln(o, w_out.astype(BF16), jnp.zeros((1, D), F32), xf, ln_g[None, :], ln_b[None, :])
    kvp = _prompt_tail(kv, WINDOW)
    nk_p = kvp[..., :kvw].reshape(BATCH, WINDOW, NKV, HD)
    nv_p = kvp[..., kvw:].reshape(BATCH, WINDOW, NKV, HD)
    nk_s = jnp.concatenate([ck[:, DEC_SEQ:], kvs[:, :DEC_SEQ, :kvw]], axis=1).reshape(DEC_BATCH, WINDOW, NKV, HD)
    nv_s = jnp.concatenate([cv[:, DEC_SEQ:], kvs[:, :DEC_SEQ, kvw:]], axis=1).reshape(DEC_BATCH, WINDOW, NKV, HD)
    return xf, xb, nk_p, nv_p, nk_s, nv_s


def kernel(x_prompt, x_sample, state_conv, state_hgrn, state_ssm_re, state_ssm_im, cache_swa_k, cache_swa_v, conv_w_in, conv_b_in, conv_w_dw, conv_b_dw, conv_ln_g, conv_ln_b, conv_w_out, conv_b_out, hgrn_w_in, hgrn_lb_logits, hgrn_norm_g, hgrn_w_out, ssm_lam_re, ssm_lam_im, ssm_log_dt, ssm_b_re, ssm_b_im, ssm_c_re, ssm_c_im, ssm_d, ssm_w_glu, swa_w_qkv, swa_sinks, swa_w_out, ln1_g, ln1_b, ln2_g, ln2_b, moe_w_group, moe_b_group, moe_w_egate, moe_b_egate, moe_w1, moe_w3, moe_w2):
    lbs = jnp.cumsum(jax.nn.softmax(hgrn_lb_logits.astype(F32), axis=0), axis=0)
    lbs = lbs - lbs[0]
    xs_pad = jnp.pad(x_sample, ((0, 0), (0, TS - DEC_SEQ), (0, 0)))
    xf = jnp.concatenate([x_prompt.reshape(NP, D), xs_pad.reshape(NS, D)], axis=0)
    xb = xf.astype(BF16)
    moe = (moe_w_group, moe_b_group, moe_w_egate, moe_b_egate, moe_w1, moe_w3, moe_w2, ln2_g, ln2_b)

    xf, xb, conv_p, conv_s = conv_layer(
        xf, xb, state_conv[0], conv_w_in[0], conv_b_in[0], conv_w_dw[0], conv_b_dw[0],
        conv_ln_g[0], conv_ln_b[0], conv_w_out[0], conv_b_out[0], ln1_g[0], ln1_b[0])
    xf, xb = hier_moe_block(xf, 0, *moe)

    xf, xb, hg_p, hg_s = hgrn_layer(xf, xb, state_hgrn[0], hgrn_w_in[0], lbs[1], hgrn_norm_g[0],
                                    hgrn_w_out[0], ln1_g[1], ln1_b[1])
    xf, xb = hier_moe_block(xf, 1, *moe)

    xf, xb, hr_p, hi_p, hr_s, hi_s = s5_layer(
        xf, xb, state_ssm_re[0], state_ssm_im[0], ssm_lam_re[0], ssm_lam_im[0], ssm_log_dt[0],
        ssm_b_re[0], ssm_b_im[0], ssm_c_re[0], ssm_c_im[0], ssm_d[0], ssm_w_glu[0], ln1_g[2], ln1_b[2])
    xf, xb = hier_moe_block(xf, 2, *moe)

    xf, xb, k_p, v_p, k_s, v_s = swa_layer(xf, xb, cache_swa_k[0], cache_swa_v[0], swa_w_qkv[0],
                                            swa_sinks[0], swa_w_out[0], ln1_g[3], ln1_b[3])
    xf, xb = hier_moe_block(xf, 3, *moe)

    y_p = xf[:NP].reshape(BATCH, SEQ, D)
    y_s = _sample_rows(xf)[:, :DEC_SEQ]
    return (y_p, y_s, conv_p[None], conv_s[None], hg_p[None], hg_s[None],
            hr_p[None], hi_p[None], hr_s[None], hi_s[None],
            k_p[None], v_p[None], k_s[None], v_s[None])
```

```python
import functools

import jax
import jax.numpy as jnp
from jax import lax
from jax.experimental import pallas as pl
from jax.experimental.pallas import tpu as pltpu

F32 = jnp.float32
BF16 = jnp.bfloat16

D = 2048
BATCH = 4
SEQ = 2048
DEPTH = 4
DEC_BATCH = 32
DEC_SEQ = 4
PAST_LEN = 16384
CONV_WIDTH = 31
HG_H = 16
HG_DK = 128
HG_DV = 128
SSM_GROUP = 16
SSM_GROUPS = 128
SSM_STATE = 64
HD = 64
NH = 32
NKV = 4
WINDOW = 128
ROPE_THETA = 10000.0
MOE_E = 16
MOE_FF = 512
ALPHA = (2.0 * DEPTH) ** 0.25
LN_EPS = 1e-5
RMS_EPS = 1e-6

TS = 8
NP = BATCH * SEQ
NS = DEC_BATCH * TS
MT = NP + NS
LANES = 128
SUBLANES = 8
VMEM_LIMIT = 56 * 1024 * 1024

TM_MM = 768
TN_MM = 512
TM_LN = 384
TM_E = 512
P_SLOTS = ((2 * MT + MOE_E * (TM_E - 1) + TM_E - 1) // TM_E) * TM_E
N_ETILES = P_SLOTS // TM_E
GATHER_CH = 64
SMEM_UNROLL = 8


def _cp(*sem):
    return pltpu.CompilerParams(dimension_semantics=sem, vmem_limit_bytes=VMEM_LIMIT)


def _sigmoid(x):
    return 1.0 / (1.0 + jnp.exp(-x))


def _silu(x):
    return x * _sigmoid(x)


def _ln_rows(z, g, b):
    mu = jnp.mean(z, -1, keepdims=True)
    zc = z - mu
    var = jnp.mean(zc * zc, -1, keepdims=True)
    return zc * lax.rsqrt(var + LN_EPS) * g + b


def _cast_weights_once(pairs):
    @pl.when(pl.program_id(1) == 0)
    def _():
        for src, dst in pairs:
            dst[...] = src[...].astype(BF16)


def _mm_glu_kernel(x_ref, wa_ref, wg_ref, ba_ref, bg_ref, o_ref, wab, wgb):
    _cast_weights_once(((wa_ref, wab), (wg_ref, wgb)))
    x = x_ref[...]
    a = jnp.dot(x, wab[...], preferred_element_type=F32) + ba_ref[...]
    g = jnp.dot(x, wgb[...], preferred_element_type=F32) + bg_ref[...]
    o_ref[...] = (a * _sigmoid(g)).astype(o_ref.dtype)


def mm_glu(x, w, b, out_dtype):
    m, k = x.shape
    n = w.shape[1] // 2
    nj = n // TN_MM
    return pl.pallas_call(
        _mm_glu_kernel,
        grid=(nj, m // TM_MM),
        in_specs=[
            pl.BlockSpec((TM_MM, k), lambda j, i: (i, 0)),
            pl.BlockSpec((k, TN_MM), lambda j, i: (0, j)),
            pl.BlockSpec((k, TN_MM), lambda j, i: (0, j + nj)),
            pl.BlockSpec((1, TN_MM), lambda j, i: (0, j)),
            pl.BlockSpec((1, TN_MM), lambda j, i: (0, j + nj)),
        ],
        out_specs=pl.BlockSpec((TM_MM, TN_MM), lambda j, i: (i, j)),
        out_shape=jax.ShapeDtypeStruct((m, n), out_dtype),
        scratch_shapes=[pltpu.VMEM((k, TN_MM), BF16), pltpu.VMEM((k, TN_MM), BF16)],
        compiler_params=_cp("parallel", "arbitrary"),
        name="mm_glu",
    )(x, w, w, b, b)


def _mm_hgrn_kernel(x_ref, w_ref, lb_ref, o_ref, wb, *, nsec):
    j = pl.program_id(0)
    _cast_weights_once(((w_ref, wb),))
    acc = jnp.dot(x_ref[...], wb[...], preferred_element_type=F32)

    @pl.when(j < nsec)
    def _():
        o_ref[...] = _silu(acc)

    @pl.when((j >= nsec) & (j < 2 * nsec))
    def _():
        lb = lb_ref[...]
        o_ref[...] = jnp.log(lb + (1.0 - lb) * _sigmoid(acc))

    @pl.when((j >= 2 * nsec) & (j < 3 * nsec))
    def _():
        o_ref[...] = acc

    @pl.when(j >= 3 * nsec)
    def _():
        o_ref[...] = _silu(acc)


def mm_hgrn(x, w, lb_row):
    m, k = x.shape
    n = w.shape[1]
    tn = 2 * TN_MM
    nsec = (n // 4) // tn
    return pl.pallas_call(
        functools.partial(_mm_hgrn_kernel, nsec=nsec),
        grid=(n // tn, m // TM_MM),
        in_specs=[
            pl.BlockSpec((TM_MM, k), lambda j, i: (i, 0)),
            pl.BlockSpec((k, tn), lambda j, i: (0, j)),
            pl.BlockSpec((1, tn), lambda j, i: (0, j)),
        ],
        out_specs=pl.BlockSpec((TM_MM, tn), lambda j, i: (i, j)),
        out_shape=jax.ShapeDtypeStruct((m, n), F32),
        scratch_shapes=[pltpu.VMEM((k, tn), BF16)],
        compiler_params=_cp("parallel", "arbitrary"),
        name="mm_hgrn",
    )(x, w, lb_row)


def _swap_half_heads(x, lane):
    return jnp.where((lane % HD) < HD // 2, pltpu.roll(x, LANES - HD // 2, 1),
                     pltpu.roll(x, HD // 2, 1))


def _mm_rope_kernel(x_ref, w_ref, cos_ref, sin_ref, o_ref, wb, *, n_rope, scale):
    _cast_weights_once(((w_ref, wb),))
    acc = jnp.dot(x_ref[...], wb[...], preferred_element_type=F32)
    cos = cos_ref[...]
    sin = sin_ref[...]
    lane = lax.broadcasted_iota(jnp.int32, cos.shape, 1)
    for c in range(acc.shape[1] // LANES):
        xc = acc[:, c * LANES:(c + 1) * LANES]
        if c < n_rope:
            xc = (xc * cos + _swap_half_heads(xc, lane) * sin) * scale
        o_ref[:, c * LANES:(c + 1) * LANES] = xc


def mm_rope(x, w, col0, n, cos, sin, n_rope, scale):
    m, k = x.shape
    j0 = col0 // TN_MM
    return pl.pallas_call(
        functools.partial(_mm_rope_kernel, n_rope=n_rope, scale=scale),
        grid=(n // TN_MM, m // TM_MM),
        in_specs=[
            pl.BlockSpec((TM_MM, k), lambda j, i: (i, 0)),
            pl.BlockSpec((k, TN_MM), lambda j, i: (0, j + j0)),
            pl.BlockSpec((TM_MM, LANES), lambda j, i: (i, 0)),
            pl.BlockSpec((TM_MM, LANES), lambda j, i: (i, 0)),
        ],
        out_specs=pl.BlockSpec((TM_MM, TN_MM), lambda j, i: (i, j)),
        out_shape=jax.ShapeDtypeStruct((m, n), F32),
        scratch_shapes=[pltpu.VMEM((k, TN_MM), BF16)],
        compiler_params=_cp("parallel", "arbitrary"),
        name="mm_rope",
    )(x, w, cos, sin)


def _mm_res_ln_kernel(x_ref, w_ref, b_ref, res_ref, g_ref, beta_ref, of_ref, ob_ref):
    y = jnp.dot(x_ref[...], w_ref[...], preferred_element_type=F32) + b_ref[...]
    out = _ln_rows(ALPHA * res_ref[...] + y, g_ref[...], beta_ref[...])
    of_ref[...] = out
    ob_ref[...] = out.astype(BF16)


def mm_res_ln(x, w, b, res, g, beta):
    m, k = x.shape
    row = lambda i: (i, 0)
    fix = lambda i: (0, 0)
    return pl.pallas_call(
        _mm_res_ln_kernel,
        grid=(m // TM_LN,),
        in_specs=[
            pl.BlockSpec((TM_LN, k), row),
            pl.BlockSpec((k, D), fix),
            pl.BlockSpec((1, D), fix),
            pl.BlockSpec((TM_LN, D), row),
            pl.BlockSpec((1, D), fix),
            pl.BlockSpec((1, D), fix),
        ],
        out_specs=[pl.BlockSpec((TM_LN, D), row), pl.BlockSpec((TM_LN, D), row)],
        out_shape=[jax.ShapeDtypeStruct((m, D), F32), jax.ShapeDtypeStruct((m, D), BF16)],
        compiler_params=_cp("parallel"),
        name="mm_res_ln",
    )(x, w, b, res, g, beta)


def _res_ln_kernel(res_ref, y_ref, g_ref, beta_ref, of_ref, ob_ref):
    out = _ln_rows(ALPHA * res_ref[...] + y_ref[...].astype(F32), g_ref[...], beta_ref[...])
    of_ref[...] = out
    ob_ref[...] = out.astype(BF16)


def res_ln(res, y, g, beta):
    m = res.shape[0]
    row = lambda i: (i, 0)
    fix = lambda i: (0, 0)
    return pl.pallas_call(
        _res_ln_kernel,
        grid=(m // TM_LN,),
        in_specs=[pl.BlockSpec((TM_LN, D), row), pl.BlockSpec((TM_LN, D), row),
                  pl.BlockSpec((1, D), fix), pl.BlockSpec((1, D), fix)],
        out_specs=[pl.BlockSpec((TM_LN, D), row), pl.BlockSpec((TM_LN, D), row)],
        out_shape=[jax.ShapeDtypeStruct((m, D), F32), jax.ShapeDtypeStruct((m, D), BF16)],
        compiler_params=_cp("parallel"),
        name="res_ln",
    )(res, y, g, beta)


HALO = 32
CONV_RC = 64
CONV_LC = 256


def _conv_kernel(u_ref, buf_ref, wdw_ref, bdw_ref, g_ref, b_ref, z_ref, ext_ref, sh_ref, c_ref, *, tt):
    i = pl.program_id(1)

    @pl.when(i == 0)
    def _():
        ext_ref[0:HALO, :] = buf_ref[0]

    @pl.when(i > 0)
    def _():
        ext_ref[0:HALO, :] = ext_ref[tt:tt + HALO, :]

    ext_ref[HALO:HALO + tt, :] = u_ref[...]
    nsh = tt + HALO - SUBLANES
    for b in range(1, SUBLANES):
        sh_ref[b - 1] = ext_ref[b:b + nsh, :]
    pad = HALO - (CONV_WIDTH - 1)
    rc = min(CONV_RC, tt)
    for r0 in range(0, tt, rc):
        for c0 in range(0, D, CONV_LC):
            acc = jnp.zeros((rc, CONV_LC), F32) + bdw_ref[:, c0:c0 + CONV_LC]
            for w in range(CONV_WIDTH):
                a, b = divmod(pad + w, SUBLANES)
                rows = slice(r0 + a * SUBLANES, r0 + a * SUBLANES + rc)
                if b == 0:
                    tap = ext_ref[rows, c0:c0 + CONV_LC]
                else:
                    tap = sh_ref[b - 1, rows, c0:c0 + CONV_LC]
                acc = acc + tap * wdw_ref[w:w + 1, c0:c0 + CONV_LC]
            c_ref[r0:r0 + rc, c0:c0 + CONV_LC] = acc
    z_ref[...] = _silu(_ln_rows(c_ref[...], g_ref[...], b_ref[...])).astype(z_ref.dtype)


def conv_ln_silu(u, row0_blocks, nseq, t, tt, buf, wdw, bdw, g, b, out_dtype):
    nt = t // tt
    fix = lambda s, i: (0, 0)
    return pl.pallas_call(
        functools.partial(_conv_kernel, tt=tt),
        grid=(nseq, nt),
        in_specs=[
            pl.BlockSpec((tt, D), lambda s, i: (row0_blocks + s * nt + i, 0)),
            pl.BlockSpec((1, HALO, D), lambda s, i: (s, 0, 0)),
            pl.BlockSpec((CONV_WIDTH, D), fix),
            pl.BlockSpec((1, D), fix),
            pl.BlockSpec((1, D), fix),
            pl.BlockSpec((1, D), fix),
        ],
        out_specs=pl.BlockSpec((tt, D), lambda s, i: (s * nt + i, 0)),
        out_shape=jax.ShapeDtypeStruct((nseq * t, D), out_dtype),
        scratch_shapes=[pltpu.VMEM((HALO + tt, D), F32),
                        pltpu.VMEM((SUBLANES - 1, tt + HALO - SUBLANES, D), F32),
                        pltpu.VMEM((tt, D), F32)],
        compiler_params=_cp("parallel", "arbitrary"),
        name="conv_ln_silu",
    )(u, buf, wdw, bdw, g, b)


def _hgrn_kernel(p_ref, s0_ref, ng_ref, o_ref, sf_ref, st_ref, pst_ref, ob_ref, bl_ref, *, tt, sc, nvalid):
    i = pl.program_id(1)
    hk = HG_H * HG_DK

    @pl.when(i == 0)
    def _():
        for h in range(HG_H):
            st_ref[h * HG_DV:(h + 1) * HG_DV, :] = s0_ref[0, h].T

    ones = jnp.ones((LANES, LANES), BF16)
    row = lax.broadcasted_iota(jnp.int32, (sc, hk), 0)
    first = [(s // SUBLANES) * SUBLANES for s in range(nvalid)]
    offs = [sum(sc - f for f in first[:s]) for s in range(nvalid + 1)]

    if nvalid == sc:
        rr = lax.broadcasted_iota(jnp.int32, (tt, tt), 0)
        cc = lax.broadcasted_iota(jnp.int32, (tt, tt), 1)
        tril = ((cc <= rr) & (cc // sc == rr // sc)).astype(F32)
        for h in range(HG_H):
            sl = slice(h * HG_DK, (h + 1) * HG_DK)
            bl_ref[:, sl] = jnp.dot(tril, p_ref[:, hk + h * HG_DK:hk + (h + 1) * HG_DK],
                                    preferred_element_type=F32, precision=lax.Precision.HIGHEST)

    def sub_chunk(ci, carry):
        r0 = pl.multiple_of(ci * sc, sc)
        q = p_ref[pl.ds(r0, sc), 0:hk]
        lf = p_ref[pl.ds(r0, sc), hk:2 * hk]
        v = p_ref[pl.ds(r0, sc), 2 * hk:3 * hk]
        if nvalid < sc:
            lf = jnp.where(row < nvalid, lf, 0.0)
            bl = jnp.zeros((sc, hk), F32)
            for s in range(nvalid):
                bl = bl + jnp.where(row >= s, lf[s:s + 1, :], 0.0)
        else:
            bl = bl_ref[pl.ds(r0, sc), :]
        kk = 1.0 - jnp.exp(lf)
        btot = bl[sc - 1:sc, :]
        qd = q * jnp.exp(bl)
        kd = kk * jnp.exp(btot - bl)
        for s in range(nvalid):
            f = first[s]
            rowf = lax.broadcasted_iota(jnp.int32, (sc - f, hk), 0) + f
            e = jnp.exp(jnp.where(rowf >= s, bl[f:] - bl[s:s + 1, :], -jnp.inf))
            pst_ref[offs[s]:offs[s + 1], :] = q[f:] * e * kk[s:s + 1, :]
        o_acc = []
        for h in range(HG_H):
            sl = slice(h * HG_DK, (h + 1) * HG_DK)
            rs = jnp.dot(pst_ref[0:offs[nvalid], sl].astype(BF16), ones,
                         preferred_element_type=F32)
            vh = v[:, sl]
            st_h = st_ref[h * HG_DV:(h + 1) * HG_DV, :]
            oh = lax.dot_general(qd[:, sl].astype(BF16), st_h.astype(BF16),
                                 (((1,), (1,)), ((), ())), preferred_element_type=F32)
            tiles = [oh[f:f + SUBLANES] for f in range(0, sc, SUBLANES)]
            for s in range(nvalid):
                for ti, f in enumerate(range(first[s], sc, SUBLANES)):
                    piece = rs[offs[s] + ti * SUBLANES:offs[s] + (ti + 1) * SUBLANES, :]
                    tiles[f // SUBLANES] = tiles[f // SUBLANES] + piece * vh[s:s + 1, :]
            o_acc.append(jnp.concatenate(tiles, axis=0) if len(tiles) > 1 else tiles[0])
            upd = jnp.dot(vh.T.astype(BF16), kd[:, sl].astype(BF16), preferred_element_type=F32)
            st_ref[h * HG_DV:(h + 1) * HG_DV, :] = jnp.exp(btot[:, sl]) * st_h + upd
        ob_ref[pl.ds(r0, sc), :] = jnp.concatenate(o_acc, axis=1)
        return carry

    lax.fori_loop(0, tt // sc, sub_chunk, 0)

    o = ob_ref[...]
    o2 = o * o
    parts = []
    for h in range(HG_H):
        sl = slice(h * HG_DV, (h + 1) * HG_DV)
        ms = jnp.dot(o2[:, sl].astype(BF16), ones, preferred_element_type=F32) * (1.0 / HG_DV)
        parts.append(o[:, sl] * lax.rsqrt(ms + RMS_EPS))
    on = jnp.concatenate(parts, axis=1) * ng_ref[...] * p_ref[:, 3 * hk:4 * hk]
    o_ref[...] = on.astype(o_ref.dtype)

    @pl.when(i == pl.num_programs(1) - 1)
    def _():
        for h in range(HG_H):
            sf_ref[0, h] = st_ref[h * HG_DV:(h + 1) * HG_DV, :].T


def hgrn_scan(p, row0_blocks, nseq, t, tt, sc, nvalid, s0, norm_g, out_dtype):
    nt = t // tt
    return pl.pallas_call(
        functools.partial(_hgrn_kernel, tt=tt, sc=sc, nvalid=nvalid),
        grid=(nseq, nt),
        in_specs=[
            pl.BlockSpec((tt, 4 * D), lambda s, i: (row0_blocks + s * nt + i, 0)),
            pl.BlockSpec((1, HG_H, HG_DK, HG_DV), lambda s, i: (s, 0, 0, 0)),
            pl.BlockSpec((1, D), lambda s, i: (0, 0)),
        ],
        out_specs=[
            pl.BlockSpec((tt, D), lambda s, i: (s * nt + i, 0)),
            pl.BlockSpec((1, HG_H, HG_DK, HG_DV), lambda s, i: (s, 0, 0, 0)),
        ],
        out_shape=[jax.ShapeDtypeStruct((nseq * t, D), out_dtype),
                   jax.ShapeDtypeStruct((nseq, HG_H, HG_DK, HG_DV), F32)],
        scratch_shapes=[pltpu.VMEM((HG_H * HG_DV, HG_DK), F32),
                        pltpu.VMEM((sc * sc, D), F32),
                        pltpu.VMEM((tt, D), F32),
                        pltpu.VMEM((tt, D), F32)],
        compiler_params=_cp("parallel", "arbitrary"),
        name="hgrn_scan",
    )(p, s0, norm_g)


S5_GPB = LANES // SSM_GROUP
S5_NJ = D // LANES
S5_SW = S5_GPB * SSM_STATE
S5_LC = 8


def _spread_groups(blk, rep, mask):
    full = jnp.dot(blk.astype(BF16), rep, preferred_element_type=F32)
    return jnp.where(mask, full, 0.0).astype(BF16)


def _group_mask(rows, row_w, cols, col_w):
    r = lax.broadcasted_iota(jnp.int32, (rows, cols), 0) // row_w
    c = lax.broadcasted_iota(jnp.int32, (rows, cols), 1) // col_w
    return r == c


def _rep_matrix(w, cols):
    r = lax.broadcasted_iota(jnp.int32, (w, cols), 0)
    c = lax.broadcasted_iota(jnp.int32, (w, cols), 1) % w
    return (r == c).astype(BF16)


def _s5_kernel(u_ref, h0_ref, klt_ref, wtr_ref, wti_ref, car_ref, cai_ref, ar_ref, ai_ref, y_ref, hf_ref,
               t_s, bp_s, cp_s, con_ref, hp_ref, *, nc, nv):
    sw = S5_SW
    lc = S5_LC
    c, p = SSM_GROUP, SSM_STATE
    rep_c = _rep_matrix(c, LANES)
    rep_p = _rep_matrix(p, sw)
    t_s[...] = jnp.zeros(t_s.shape, BF16)
    mask_t = _group_mask(LANES, c, LANES, c)
    for l in range(lc):
        blk = _spread_groups(klt_ref[0, l], rep_c, mask_t)
        for s in range(lc - l):
            t_s[s * LANES:(s + 1) * LANES, (s + l) * LANES:(s + l + 1) * LANES] = blk
    mask_b = _group_mask(LANES, c, sw, p)
    for s in range(lc):
        if s < nv:
            bp_s[s * LANES:(s + 1) * LANES, 0:sw] = _spread_groups(wtr_ref[0, nv - 1 - s], rep_p, mask_b)
            bp_s[s * LANES:(s + 1) * LANES, sw:2 * sw] = _spread_groups(wti_ref[0, nv - 1 - s], rep_p, mask_b)
        else:
            bp_s[s * LANES:(s + 1) * LANES, :] = jnp.zeros((LANES, 2 * sw), BF16)
    mask_c = _group_mask(sw, p, LANES, c)
    for t in range(lc):
        cp_s[0:sw, t * LANES:(t + 1) * LANES] = _spread_groups(car_ref[0, t], rep_c, mask_c)
        cp_s[sw:2 * sw, t * LANES:(t + 1) * LANES] = _spread_groups(cai_ref[0, t], rep_c, mask_c)

    u = u_ref[0]
    con_ref[...] = jnp.dot(u, bp_s[...], preferred_element_type=F32)
    ar = ar_ref[0]
    ai = ai_ref[0]

    def step(hr, hi, cr, ci):
        return ar * hr - ai * hi + cr, ar * hi + ai * hr + ci

    if nc == 1:
        hr, hi = h0_ref[0, :, 0:sw], h0_ref[0, :, sw:2 * sw]
        hp_ref[:, 0:sw] = hr
        hp_ref[:, sw:2 * sw] = hi
        hr, hi = step(hr, hi, con_ref[:, 0:sw], con_ref[:, sw:2 * sw])
    else:
        half = 4
        top = lax.broadcasted_iota(jnp.int32, (2 * half, sw), 0) < half

        def pair(k, carry):
            hr, hi = carry
            r0 = pl.multiple_of(k * 2 * half, 2 * half)
            cr = con_ref[pl.ds(r0, 2 * half), 0:sw]
            ci = con_ref[pl.ds(r0, 2 * half), sw:2 * sw]
            ar1, ai1 = step(hr, hi, cr, ci)
            ar1 = jnp.where(top, ar1, pltpu.roll(ar1, half, 0))
            ai1 = jnp.where(top, ai1, pltpu.roll(ai1, half, 0))
            hp_ref[pl.ds(r0, 2 * half), 0:sw] = jnp.where(top, hr, ar1)
            hp_ref[pl.ds(r0, 2 * half), sw:2 * sw] = jnp.where(top, hi, ai1)
            br, bi = step(ar1, ai1, cr, ci)
            br = jnp.where(top, pltpu.roll(br, half, 0), br)
            bi = jnp.where(top, pltpu.roll(bi, half, 0), bi)
            return br, bi

        hr, hi = lax.fori_loop(0, nc // 2, pair, (h0_ref[0, :, 0:sw], h0_ref[0, :, sw:2 * sw]))
    hf_ref[0, :, 0:sw] = hr
    hf_ref[0, :, sw:2 * sw] = hi
    y = (jnp.dot(u, t_s[...], preferred_element_type=F32)
         + jnp.dot(hp_ref[...].astype(BF16), cp_s[...], preferred_element_type=F32))
    y_ref[0] = jax.nn.gelu(y).astype(y_ref.dtype)


def s5_lti(u, h0, ops, nc, nv):
    klt, wtr, wti, car, cai, pw_re, pw_im = ops
    nj, rows, lw = u.shape
    hrows = h0.shape[1]
    sw, lc, c, p = S5_SW, S5_LC, SSM_GROUP, SSM_STATE
    ar = pw_re[:, nv].reshape(nj, 1, sw)
    ai = pw_im[:, nv].reshape(nj, 1, sw)
    blk = lambda j: (j, 0, 0)
    blk4 = lambda j: (j, 0, 0, 0)
    return pl.pallas_call(
        functools.partial(_s5_kernel, nc=nc, nv=nv),
        grid=(nj,),
        in_specs=[
            pl.BlockSpec((1, rows, lw), blk),
            pl.BlockSpec((1, hrows, 2 * sw), blk),
            pl.BlockSpec((1, lc, LANES, c), blk4),
            pl.BlockSpec((1, lc, LANES, p), blk4),
            pl.BlockSpec((1, lc, LANES, p), blk4),
            pl.BlockSpec((1, lc, sw, c), blk4),
            pl.BlockSpec((1, lc, sw, c), blk4),
            pl.BlockSpec((1, 1, sw), blk),
            pl.BlockSpec((1, 1, sw), blk),
        ],
        out_specs=[pl.BlockSpec((1, rows, lw), blk), pl.BlockSpec((1, hrows, 2 * sw), blk)],
        out_shape=[jax.ShapeDtypeStruct((nj, rows, lw), BF16),
                   jax.ShapeDtypeStruct((nj, hrows, 2 * sw), F32)],
        scratch_shapes=[pltpu.VMEM((lw, lw), BF16), pltpu.VMEM((lw, 2 * sw), BF16),
                        pltpu.VMEM((2 * sw, lw), BF16),
                        pltpu.VMEM((rows, 2 * sw), F32), pltpu.VMEM((rows, 2 * sw), F32)],
        compiler_params=_cp("parallel"),
        name="s5_lti",
    )(u, h0, klt, wtr, wti, car, cai, ar, ai)


def _s5_operators(lam_re, lam_im, log_dt, b_re, b_im, c_re, c_im, d_skip):
    hp = lax.Precision.HIGHEST
    gpb, nj, c, p, lc = S5_GPB, S5_NJ, SSM_GROUP, SSM_STATE, S5_LC
    dt = jnp.exp(log_dt)[:, None]
    lr, li = lam_re, lam_im
    mag = jnp.exp(lr * dt)
    ab_re, ab_im = mag * jnp.cos(li * dt), mag * jnp.sin(li * dt)
    nr, ni = ab_re - 1.0, ab_im
    den = lr * lr + li * li
    z_re, z_im = (nr * lr + ni * li) / den, (ni * lr - nr * li) / den
    bbt_re = (z_re[..., None] * b_re - z_im[..., None] * b_im).transpose(0, 2, 1)
    bbt_im = (z_re[..., None] * b_im + z_im[..., None] * b_re).transpose(0, 2, 1)
    l = jnp.arange(lc + 1, dtype=F32)[None, :, None]
    pmag = jnp.exp(l * (lr * dt)[:, None, :])
    pw_re = pmag * jnp.cos(l * (li * dt)[:, None, :])
    pw_im = pmag * jnp.sin(l * (li * dt)[:, None, :])
    wt_re = pw_re[:, :lc, None, :] * bbt_re[:, None] - pw_im[:, :lc, None, :] * bbt_im[:, None]
    wt_im = pw_re[:, :lc, None, :] * bbt_im[:, None] + pw_im[:, :lc, None, :] * bbt_re[:, None]
    klt = (jnp.einsum('glcp,gop->glco', wt_re, c_re, precision=hp)
           - jnp.einsum('glcp,gop->glco', wt_im, c_im, precision=hp))
    klt = klt.at[:, 0].add(d_skip.reshape(SSM_GROUPS, c)[:, :, None] * jnp.eye(c, dtype=F32)[None])
    ct_re, ct_im = c_re.transpose(0, 2, 1)[:, None], c_im.transpose(0, 2, 1)[:, None]
    ca_re = ct_re * pw_re[:, 1:, :, None] - ct_im * pw_im[:, 1:, :, None]
    ca_im = ct_re * pw_im[:, 1:, :, None] + ct_im * pw_re[:, 1:, :, None]

    def by_block(a):
        _, ll, r, w = a.shape
        return a.reshape(nj, gpb, ll, r, w).transpose(0, 2, 1, 3, 4).reshape(nj, ll, gpb * r, w)

    def pw_block(a):
        return a.reshape(nj, gpb, lc + 1, p).transpose(0, 2, 1, 3).reshape(nj, lc + 1, gpb * p)

    return (by_block(klt), by_block(wt_re), by_block(wt_im), by_block(ca_re), by_block(-ca_im),
            pw_block(pw_re), pw_block(pw_im))


def _attn_kernel(q_ref, kv_ref, pk_ref, pv_ref, sink_ref, o_ref, *, tq, nblk, always_prev):
    idx = pl.program_id(0)
    kvw = NKV * HD
    group = NH // NKV
    pkn = pk_ref.shape[1]
    to = kv_ref.shape[1]
    nk = pkn + to
    rows = group * tq
    a = lax.broadcasted_iota(jnp.int32, (rows, nk), 0) % tq
    col = lax.broadcasted_iota(jnp.int32, (rows, nk), 1)
    prev_ok = col >= a
    if not always_prev:
        prev_ok = prev_ok & ((idx % nblk) > 0)
    mask = ((col < pkn) & prev_ok) | ((col >= pkn) & ((col - pkn) <= a))
    hrow = lax.broadcasted_iota(jnp.int32, (rows, 1), 0) // tq
    low = lax.broadcasted_iota(jnp.int32, (tq, LANES), 1) < HD
    nt = (((1,), (1,)), ((), ()))
    for kb in range(NKV // 2):
        lanes = slice(kb * LANES, (kb + 1) * LANES)
        kcat = jnp.concatenate([pk_ref[0, :, lanes], kv_ref[0, :, lanes]], axis=0).astype(BF16)
        vcat = jnp.concatenate([pv_ref[0, :, lanes],
                                kv_ref[0, :, kvw + kb * LANES:kvw + (kb + 1) * LANES]], axis=0).astype(BF16)
        for hi in range(2):
            j = 2 * kb + hi
            parts = []
            sk = jnp.zeros((rows, 1), F32)
            for h8 in range(group):
                qb, e = (group * j + h8) // 2, h8 % 2
                q2 = q_ref[0, :, qb * LANES:(qb + 1) * LANES]
                qe = jnp.where(low if e == 0 else jnp.logical_not(low), q2, 0.0)
                if e != hi:
                    qe = pltpu.roll(qe, HD, 1)
                parts.append(qe)
                sk = jnp.where(hrow == h8, sink_ref[group * j + h8], sk)
            q8 = jnp.concatenate(parts, axis=0).astype(BF16)
            s = lax.dot_general(q8, kcat, nt, preferred_element_type=F32)
            s = jnp.where(mask, s, -jnp.inf)
            m = jnp.maximum(jnp.max(s, -1, keepdims=True), sk)
            p = jnp.exp(s - m)
            inv = 1.0 / (jnp.sum(p, -1, keepdims=True) + jnp.exp(sk - m))
            o8 = jnp.dot((p * inv).astype(BF16), vcat, preferred_element_type=F32)
            for pr in range(group // 2):
                qb = (group * j) // 2 + pr
                pair = []
                for e in range(2):
                    oe = o8[(2 * pr + e) * tq:(2 * pr + e + 1) * tq, :]
                    if e != hi:
                        oe = pltpu.roll(oe, HD, 1)
                    pair.append(oe)
                o_ref[0, :, qb * LANES:(qb + 1) * LANES] = jnp.where(low, pair[0], pair[1]).astype(o_ref.dtype)


def swa_attention(q3, kv3, pk3, pv3, nb, blk0, pk_map, pv_map, sinks, nblk, always_prev):
    tq = q3.shape[1]
    to = kv3.shape[1]
    kvw = NKV * HD
    return pl.pallas_call(
        functools.partial(_attn_kernel, tq=tq, nblk=nblk, always_prev=always_prev),
        grid=(nb,),
        in_specs=[
            pl.BlockSpec((1, tq, D), lambda i: (i + blk0, 0, 0)),
            pl.BlockSpec((1, to, 2 * kvw), lambda i: (i, 0, 0)),
            pl.BlockSpec((1, pk3.shape[1], kvw), pk_map),
            pl.BlockSpec((1, pv3.shape[1], kvw), pv_map),
            pl.BlockSpec(memory_space=pltpu.SMEM),
        ],
        out_specs=pl.BlockSpec((1, tq, D), lambda i: (i, 0, 0)),
        out_shape=jax.ShapeDtypeStruct((nb, tq, D), BF16),
        compiler_params=_cp("parallel"),
        name="swa_attention",
    )(q3, kv3, pk3, pv3, sinks)


R_E1, R_E2, R_W1, R_W2, R_R1, R_R2 = 0, 1, 2, 3, 4, 5


def _router_kernel(x_ref, wh_ref, wl_ref, b_ref, o_ref, cnt_ref, carry_ref):
    i = pl.program_id(0)

    @pl.when(i == 0)
    def _():
        carry_ref[...] = jnp.zeros(carry_ref.shape, F32)

    x = x_ref[...]
    xh = x.astype(BF16)
    xl = (x - xh.astype(F32)).astype(BF16)
    wh = wh_ref[...]
    l = (jnp.dot(xh, wh, preferred_element_type=F32) + jnp.dot(xl, wh, preferred_element_type=F32)
         + jnp.dot(xh, wl_ref[...], preferred_element_type=F32) + b_ref[...])
    tm = l.shape[0]
    lane = lax.broadcasted_iota(jnp.int32, l.shape, 1).astype(F32)
    big = 1e9
    ninf = -jnp.inf
    gmask = lane < 4.0
    gl = jnp.where(gmask, l, ninf)
    gm = jnp.max(gl, -1, keepdims=True)
    gsel = jnp.min(jnp.where(gl == gm, lane, big), -1, keepdims=True)
    gsum = jnp.sum(jnp.where(gmask, jnp.exp(gl - gm), 0.0), -1, keepdims=True)
    gval = 1.0 / gsum
    lo = 4.0 + 4.0 * gsel
    emask = (lane >= lo) & (lane < lo + 4.0)
    el = jnp.where(emask, l, ninf)
    m1 = jnp.max(el, -1, keepdims=True)
    i1 = jnp.min(jnp.where(emask & (el == m1), lane, big), -1, keepdims=True)
    emask2 = emask & (lane != i1)
    el2 = jnp.where(emask2, l, ninf)
    m2 = jnp.max(el2, -1, keepdims=True)
    i2 = jnp.min(jnp.where(emask2 & (el2 == m2), lane, big), -1, keepdims=True)
    r = jnp.exp(m2 - m1)
    w1 = gval / (1.0 + r)
    w2 = gval * r / (1.0 + r)
    e1 = i1 - 4.0
    e2 = i2 - 4.0
    oh = ((lane == e1) | (lane == e2)).astype(F32)
    tri = (lax.broadcasted_iota(jnp.int32, (tm, tm), 1)
           < lax.broadcasted_iota(jnp.int32, (tm, tm), 0)).astype(BF16)
    before = jnp.dot(tri, oh.astype(BF16), preferred_element_type=F32) + carry_ref[...]
    r1 = jnp.sum(jnp.where(lane == e1, before, 0.0), -1, keepdims=True)
    r2 = jnp.sum(jnp.where(lane == e2, before, 0.0), -1, keepdims=True)
    carry_ref[...] = carry_ref[...] + jnp.sum(oh, axis=0, keepdims=True)
    cnt_ref[...] = carry_ref[...]
    rec = jnp.zeros(l.shape, F32)
    for ln, val in ((R_E1, e1), (R_E2, e2), (R_W1, w1), (R_W2, w2), (R_R1, r1), (R_R2, r2)):
        rec = jnp.where(lane == float(ln), val, rec)
    o_ref[...] = rec


def moe_router(x, w, b):
    m = x.shape[0]
    wh = w.astype(BF16)
    wl = (w - wh.astype(F32)).astype(BF16)
    return pl.pallas_call(
        _router_kernel,
        grid=(m // TM_LN,),
        in_specs=[pl.BlockSpec((TM_LN, D), lambda i: (i, 0)),
                  pl.BlockSpec((D, LANES), lambda i: (0, 0)),
                  pl.BlockSpec((D, LANES), lambda i: (0, 0)),
                  pl.BlockSpec((1, LANES), lambda i: (0, 0))],
        out_specs=[pl.BlockSpec((TM_LN, LANES), lambda i: (i, 0)),
                   pl.BlockSpec((1, LANES), lambda i: (0, 0))],
        out_shape=[jax.ShapeDtypeStruct((m, LANES), F32), jax.ShapeDtypeStruct((1, LANES), F32)],
        scratch_shapes=[pltpu.VMEM((1, LANES), F32)],
        compiler_params=_cp("arbitrary"),
        name="moe_router",
    )(x, wh, wl, b)


def _slots_kernel(rec_ref, starts_ref, o_ref):
    rec = rec_ref[...]
    lane = lax.broadcasted_iota(jnp.int32, rec.shape, 1).astype(F32)
    starts = starts_ref[...]
    out = jnp.zeros(rec.shape, F32)
    for k, (le, lr) in enumerate(((R_E1, R_R1), (R_E2, R_R2))):
        e = rec[:, le:le + 1]
        s = jnp.sum(jnp.where(lane == e, starts, 0.0), -1, keepdims=True) + rec[:, lr:lr + 1]
        out = jnp.where(lane == float(k), s, out)
    o_ref[...] = out


def moe_slots(rec, starts_row):
    m = rec.shape[0]
    return pl.pallas_call(
        _slots_kernel,
        grid=(m // TM_MM,),
        in_specs=[pl.BlockSpec((TM_MM, LANES), lambda i: (i, 0)), pl.BlockSpec((1, LANES), lambda i: (0, 0))],
        out_specs=pl.BlockSpec((TM_MM, LANES), lambda i: (i, 0)),
        out_shape=jax.ShapeDtypeStruct((m, LANES), F32),
        compiler_params=_cp("parallel"),
        name="moe_slots",
    )(rec, starts_row)


def _row_gather_start(idx_ref, base, nchunks, src_hbm, dst, sem):
    def body(c, carry):
        off = pl.multiple_of(c * GATHER_CH, GATHER_CH)
        dst_c = dst.at[pl.ds(off, GATHER_CH)]
        for u in range(GATHER_CH):
            pltpu.make_async_copy(src_hbm.at[pl.ds(idx_ref[base + off + u], 1)],
                                  dst_c.at[pl.ds(u, 1)], sem).start()
        return carry
    lax.fori_loop(0, nchunks, body, 0)


def _row_gather_wait(nchunks, src_hbm, dst, sem):
    def body(c, carry):
        pltpu.make_async_copy(src_hbm.at[pl.ds(0, GATHER_CH)], dst.at[pl.ds(0, GATHER_CH)], sem).wait()
        return carry
    lax.fori_loop(0, nchunks, body, 0)


def _ffn_kernel(te_ref, nt_ref, nch_ref, pos_ref, x_hbm, w1_ref, w3_ref, w2_ref, y_ref, xbuf, sem,
                w1b, w3b, w2b, src_ref):
    i = pl.program_id(0)
    nt = nt_ref[0]
    slot = i % 2

    @pl.when(i == 0)
    def _():
        xbuf[...] = jnp.zeros(xbuf.shape, F32)
        def clear(k, carry):
            for u in range(SMEM_UNROLL):
                src_ref[k * SMEM_UNROLL + u] = 0
            return carry
        lax.fori_loop(0, P_SLOTS // SMEM_UNROLL, clear, 0)

        def scatter(k, carry):
            t, rem = k // (TM_LN // SMEM_UNROLL), k % (TM_LN // SMEM_UNROLL)
            for half in range(2):
                for u in range(SMEM_UNROLL):
                    r = rem * SMEM_UNROLL + u
                    src_ref[pos_ref[t * 2 * TM_LN + half * TM_LN + r]] = t * TM_LN + r
            return carry
        lax.fori_loop(0, MT // SMEM_UNROLL, scatter, 0)
        _row_gather_start(src_ref, 0, nch_ref[0], x_hbm, xbuf.at[0], sem.at[0])

    @pl.when(i + 1 < nt)
    def _():
        _row_gather_start(src_ref, (i + 1) * TM_E, nch_ref[i + 1], x_hbm,
                          xbuf.at[1 - slot], sem.at[1 - slot])

    @pl.when(i < nt)
    def _():
        prev = te_ref[jnp.maximum(i - 1, 0)]

        @pl.when((i == 0) | (te_ref[i] != prev))
        def _():
            w1b[...] = w1_ref[0, 0].astype(BF16)
            w3b[...] = w3_ref[0, 0].astype(BF16)
            w2b[...] = w2_ref[0, 0].astype(BF16)

        _row_gather_wait(nch_ref[i], x_hbm, xbuf.at[slot], sem.at[slot])
        x = xbuf[slot].astype(BF16)
        h1 = jnp.dot(x, w1b[...], preferred_element_type=F32)
        h3 = jnp.dot(x, w3b[...], preferred_element_type=F32)
        h = _silu(h1) * h3
        y_ref[...] = jnp.dot(h.astype(BF16), w2b[...], preferred_element_type=F32)

    @pl.when(i >= nt)
    def _():
        y_ref[...] = jnp.zeros(y_ref.shape, y_ref.dtype)


def moe_ffn(layer, tile_expert, n_tiles, n_chunks, pos_tiles, x, w1, w3, w2):
    wmap = lambda i, te, nt, nch, pos: (layer, te[i], 0, 0)
    grid_spec = pltpu.PrefetchScalarGridSpec(
        num_scalar_prefetch=4,
        grid=(N_ETILES,),
        in_specs=[
            pl.BlockSpec(memory_space=pl.ANY),
            pl.BlockSpec((1, 1, D, MOE_FF), wmap),
            pl.BlockSpec((1, 1, D, MOE_FF), wmap),
            pl.BlockSpec((1, 1, MOE_FF, D), wmap),
        ],
        out_specs=pl.BlockSpec((TM_E, D), lambda i, te, nt, nch, pos: (i, 0)),
        scratch_shapes=[pltpu.VMEM((2, TM_E, D), F32), pltpu.SemaphoreType.DMA((2,)),
                        pltpu.VMEM((D, MOE_FF), BF16), pltpu.VMEM((D, MOE_FF), BF16),
                        pltpu.VMEM((MOE_FF, D), BF16), pltpu.SMEM((P_SLOTS,), jnp.int32)],
    )
    return pl.pallas_call(
        _ffn_kernel,
        grid_spec=grid_spec,
        out_shape=jax.ShapeDtypeStruct((P_SLOTS, D), F32),
        compiler_params=_cp("arbitrary"),
        name="moe_ffn",
    )(tile_expert, n_tiles, n_chunks, pos_tiles, x, w1, w3, w2)


def _combine_ln_kernel(pos_ref, res_ref, rec_ref, g_ref, beta_ref, ys_hbm, of_ref, ob_ref, ybuf, sem):
    i = pl.program_id(0)
    slot = i % 2
    n = 2 * TM_LN
    nch = n // GATHER_CH

    @pl.when(i == 0)
    def _():
        _row_gather_start(pos_ref, 0, nch, ys_hbm, ybuf.at[0], sem.at[0])

    @pl.when(i + 1 < pl.num_programs(0))
    def _():
        _row_gather_start(pos_ref, (i + 1) * n, nch, ys_hbm, ybuf.at[1 - slot], sem.at[1 - slot])

    _row_gather_wait(nch, ys_hbm, ybuf.at[slot], sem.at[slot])
    rec = rec_ref[...]
    y = (rec[:, R_W1:R_W1 + 1] * ybuf[slot, 0:TM_LN, :]
         + rec[:, R_W2:R_W2 + 1] * ybuf[slot, TM_LN:2 * TM_LN, :])
    out = _ln_rows(ALPHA * res_ref[...] + y, g_ref[...], beta_ref[...])
    of_ref[...] = out
    ob_ref[...] = out.astype(BF16)


def combine_ln(pos_tiles, res, rec, g, beta, ys):
    m = res.shape[0]
    row = lambda i, pos: (i, 0)
    fix = lambda i, pos: (0, 0)
    grid_spec = pltpu.PrefetchScalarGridSpec(
        num_scalar_prefetch=1,
        grid=(m // TM_LN,),
        in_specs=[pl.BlockSpec((TM_LN, D), row), pl.BlockSpec((TM_LN, LANES), row),
                  pl.BlockSpec((1, D), fix), pl.BlockSpec((1, D), fix),
                  pl.BlockSpec(memory_space=pl.ANY)],
        out_specs=[pl.BlockSpec((TM_LN, D), row), pl.BlockSpec((TM_LN, D), row)],
        scratch_shapes=[pltpu.VMEM((2, 2 * TM_LN, D), F32), pltpu.SemaphoreType.DMA((2,))],
    )
    return pl.pallas_call(
        _combine_ln_kernel,
        grid_spec=grid_spec,
        out_shape=[jax.ShapeDtypeStruct((m, D), F32), jax.ShapeDtypeStruct((m, D), BF16)],
        compiler_params=_cp("arbitrary"),
        name="combine_ln",
    )(pos_tiles, res, rec, g, beta, ys)


def _dispatch_plan(rec, counts_row):
    counts = counts_row[0, :MOE_E].astype(jnp.int32)
    padded = ((counts + TM_E - 1) // TM_E) * TM_E
    ends = jnp.cumsum(padded)
    starts = ends - padded
    starts_row = jnp.pad(starts.astype(F32), (0, LANES - MOE_E))[None, :]
    pos = moe_slots(rec, starts_row)[:, 0:2].astype(jnp.int32)
    n_tiles = (ends[-1] // TM_E).astype(jnp.int32)
    tile_start = jnp.arange(N_ETILES, dtype=jnp.int32) * TM_E
    te = jnp.sum((tile_start[:, None] >= ends[None, :]).astype(jnp.int32), axis=1)
    last = jnp.sum(jnp.where(jnp.arange(N_ETILES) == n_tiles - 1, te, 0))
    live = jnp.arange(N_ETILES) < n_tiles
    te = jnp.minimum(jnp.where(live, te, last), MOE_E - 1).astype(jnp.int32)
    ids = jnp.arange(MOE_E, dtype=jnp.int32)[None, :]
    seg_end = jnp.sum(jnp.where(te[:, None] == ids, (starts + counts)[None, :], 0), axis=1)
    n_rows = jnp.clip(seg_end - tile_start, 0, TM_E)
    n_chunks = jnp.where(live, (n_rows + GATHER_CH - 1) // GATHER_CH, 0).astype(jnp.int32)
    pos_tiles = pos.reshape(MT // TM_LN, TM_LN, 2).transpose(0, 2, 1).reshape(-1)
    return te, n_tiles.reshape(1), n_chunks, pos_tiles


def hier_moe_block(xf, i, moe_w_group, moe_b_group, moe_w_egate, moe_b_egate,
                   moe_w1, moe_w3, moe_w2, ln_g, ln_b):
    wr = jnp.concatenate([moe_w_group[i], moe_w_egate[i].transpose(1, 0, 2).reshape(D, MOE_E)], axis=1)
    wr = jnp.pad(wr, ((0, 0), (0, LANES - wr.shape[1])))
    br = jnp.concatenate([moe_b_group[i], moe_b_egate[i].reshape(MOE_E)])
    br = jnp.pad(br, (0, LANES - br.shape[0]))[None, :]
    rec, counts = moe_router(xf, wr, br)
    te, n_tiles, n_chunks, pos_tiles = _dispatch_plan(rec, counts)
    ys = moe_ffn(i, te, n_tiles, n_chunks, pos_tiles, xf, moe_w1, moe_w3, moe_w2)
    return combine_ln(pos_tiles, xf, rec, ln_g[i][None, :], ln_b[i][None, :], ys)


def _sample_rows(a):
    return a[NP:].reshape(DEC_BATCH, TS, a.shape[-1])


def _prompt_tail(a, n):
    return jnp.stack([a[(b + 1) * SEQ - n:(b + 1) * SEQ] for b in range(BATCH)])


def conv_layer(xf, xb, state, w_in, b_in, w_dw, b_dw, g, b, w_out, b_out, ln_g, ln_b):
    u = mm_glu(xb, w_in, b_in[None, :], F32)
    tt_p = 256
    zbuf = jnp.zeros((BATCH, HALO, D), F32)
    z_p = conv_ln_silu(u, 0, BATCH, SEQ, tt_p, zbuf, w_dw, b_dw[None, :], g[None, :], b[None, :], BF16)
    pad = HALO - (CONV_WIDTH - 1)
    sbuf = jnp.pad(state, ((0, 0), (pad, 0), (0, 0)))
    z_s = conv_ln_silu(u, NP // TS, DEC_BATCH, TS, TS, sbuf, w_dw, b_dw[None, :], g[None, :], b[None, :], F32)
    z = jnp.concatenate([z_p, z_s.astype(BF16)], axis=0)
    xf, xb = mm_res_ln(z, w_out.astype(BF16), b_out[None, :], xf, ln_g[None, :], ln_b[None, :])
    new_p = _prompt_tail(u, CONV_WIDTH - 1)
    new_s = jnp.concatenate([state[:, DEC_SEQ:], _sample_rows(u)[:, :DEC_SEQ]], axis=1)
    return xf, xb, new_p, new_s


def hgrn_layer(xf, xb, state, w_in, lb, norm_g, w_out, ln_g, ln_b):
    hk = HG_H * HG_DK
    lb_row = jnp.concatenate([jnp.zeros((hk,), F32), lb, jnp.zeros((2 * hk,), F32)])[None, :]
    p = mm_hgrn(xb, w_in, lb_row)
    s0_p = jnp.zeros((BATCH, HG_H, HG_DK, HG_DV), F32)
    o_p, sf_p = hgrn_scan(p, 0, BATCH, SEQ, 128, 16, 16, s0_p, norm_g[None, :], BF16)
    o_s, sf_s = hgrn_scan(p, NP // TS, DEC_BATCH, TS, TS, TS, DEC_SEQ, state, norm_g[None, :], F32)
    o = jnp.concatenate([o_p, o_s.astype(BF16)], axis=0)
    zero_b = jnp.zeros((1, D), F32)
    xf, xb = mm_res_ln(o, w_out.astype(BF16), zero_b, xf, ln_g[None, :], ln_b[None, :])
    return xf, xb, sf_p, sf_s


def s5_layer(xf, xb, h_re, h_im, lam_re, lam_im, log_dt, b_re, b_im, c_re, c_im, d_skip, w_glu,
             ln_g, ln_b):
    nj, sw, lc = S5_NJ, S5_SW, S5_LC
    ops = _s5_operators(lam_re, lam_im, log_dt, b_re, b_im, c_re, c_im, d_skip)
    nc = SEQ // lc
    up = xb[:NP].reshape(BATCH, nc, lc, nj, LANES).transpose(3, 1, 0, 2, 4).reshape(nj, nc * BATCH, lc * LANES)
    y_p, hf_p = s5_lti(up, jnp.zeros((nj, 2 * BATCH, 2 * sw), F32), ops, nc, lc)
    y_p = y_p.reshape(nj, nc, BATCH, lc, LANES).transpose(2, 1, 3, 0, 4).reshape(NP, D)
    hf_p = hf_p[:, :BATCH].transpose(1, 0, 2)
    us = _sample_rows(xb).reshape(DEC_BATCH, TS, nj, LANES).transpose(2, 0, 1, 3).reshape(nj, DEC_BATCH, TS * LANES)
    h0 = jnp.concatenate([h_re.reshape(DEC_BATCH, nj, sw), h_im.reshape(DEC_BATCH, nj, sw)], -1).transpose(1, 0, 2)
    y_s, hf_s = s5_lti(us, h0, ops, 1, DEC_SEQ)
    y_s = y_s.reshape(nj, DEC_BATCH, TS, LANES).transpose(1, 2, 0, 3).reshape(NS, D)
    hf_s = hf_s.transpose(1, 0, 2)
    y = jnp.concatenate([y_p, y_s], axis=0)
    m = mm_glu(y, w_glu, jnp.zeros((1, 2 * D), F32), BF16)
    xf, xb = res_ln(xf, m, ln_g[None, :], ln_b[None, :])
    st = lambda h, n: (h[..., :sw].reshape(n, SSM_GROUPS, SSM_STATE), h[..., sw:].reshape(n, SSM_GROUPS, SSM_STATE))
    return (xf, xb) + st(hf_p, BATCH) + st(hf_s, DEC_BATCH)


def _rope_tables(pos):
    half = HD // 2
    inv = ROPE_THETA ** (-jnp.arange(half, dtype=F32) / half)
    ang = pos.astype(F32)[:, None] * inv[None, :]
    cos, sin = jnp.cos(ang), jnp.sin(ang)
    cos_t = jnp.tile(jnp.concatenate([cos, cos], -1), (1, LANES // HD))
    sin_t = jnp.tile(jnp.concatenate([-sin, sin], -1), (1, LANES // HD))
    return cos_t, sin_t


def swa_layer(xf, xb, cache_k, cache_v, w_qkv, sinks, w_out, ln_g, ln_b):
    hq = NH * HD
    kvw = NKV * HD
    pos = jnp.concatenate([jnp.tile(jnp.arange(SEQ), BATCH),
                           jnp.tile(PAST_LEN + jnp.arange(TS), DEC_BATCH)])
    cos_t, sin_t = _rope_tables(pos)
    q = mm_rope(xb, w_qkv, 0, hq, cos_t, sin_t, TN_MM // LANES, HD ** -0.5)
    kv = mm_rope(xb, w_qkv, hq, 2 * kvw, cos_t, sin_t, kvw // LANES, 1.0)
    nblk = SEQ // WINDOW
    qw = q.reshape(MT // WINDOW, WINDOW, D)
    kvwin = kv.reshape(MT // WINDOW, WINDOW, 2 * kvw)
    prev = lambda i: (jnp.maximum(i - 1, 0), 0, 0)
    prev_v = lambda i: (jnp.maximum(i - 1, 0), 0, 1)
    o_p = swa_attention(qw, kvwin, kvwin, kvwin, BATCH * nblk, 0, prev, prev_v, sinks, nblk, False).reshape(NP, D)
    kvs = _sample_rows(kv)
    kvs_pad = jnp.pad(kvs, ((0, 0), (0, WINDOW - TS), (0, 0)))
    ck = cache_k.reshape(DEC_BATCH, WINDOW, kvw)
    cv = cache_v.reshape(DEC_BATCH, WINDOW, kvw)
    same = lambda i: (i, 0, 0)
    o_s = swa_attention(q.reshape(MT // TS, TS, D), kvs_pad, ck, cv, DEC_BATCH, NP // TS, same, same,
                        sinks, 1, True).reshape(NS, D)
    o = jnp.concatenate([o_p, o_s], axis=0)
    xf, xb = mm_res_ln(o, w_out.astype(BF16), jnp.zeros((1, D), F32), xf, ln_g[None, :], ln_b[None, :])
    kvp = _prompt_tail(kv, WINDOW)
    nk_p = kvp[..., :kvw].reshape(BATCH, WINDOW, NKV, HD)
    nv_p = kvp[..., kvw:].reshape(BATCH, WINDOW, NKV, HD)
    nk_s = jnp.concatenate([ck[:, DEC_SEQ:], kvs[:, :DEC_SEQ, :kvw]], axis=1).reshape(DEC_BATCH, WINDOW, NKV, HD)
    nv_s = jnp.concatenate([cv[:, DEC_SEQ:], kvs[:, :DEC_SEQ, kvw:]], axis=1).reshape(DEC_BATCH, WINDOW, NKV, HD)
    return xf, xb, nk_p, nv_p, nk_s, nv_s


def kernel(x_prompt, x_sample, state_conv, state_hgrn, state_ssm_re, state_ssm_im, cache_swa_k, cache_swa_v, conv_w_in, conv_b_in, conv_w_dw, conv_b_dw, conv_ln_g, conv_ln_b, conv_w_out, conv_b_out, hgrn_w_in, hgrn_lb_logits, hgrn_norm_g, hgrn_w_out, ssm_lam_re, ssm_lam_im, ssm_log_dt, ssm_b_re, ssm_b_im, ssm_c_re, ssm_c_im, ssm_d, ssm_w_glu, swa_w_qkv, swa_sinks, swa_w_out, ln1_g, ln1_b, ln2_g, ln2_b, moe_w_group, moe_b_group, moe_w_egate, moe_b_egate, moe_w1, moe_w3, moe_w2):
    lbs = jnp.cumsum(jax.nn.softmax(hgrn_lb_logits.astype(F32), axis=0), axis=0)
    lbs = lbs - lbs[0]
    xs_pad = jnp.pad(x_sample, ((0, 0), (0, TS - DEC_SEQ), (0, 0)))
    xf = jnp.concatenate([x_prompt.reshape(NP, D), xs_pad.reshape(NS, D)], axis=0)
    xb = xf.astype(BF16)
    moe = (moe_w_group, moe_b_group, moe_w_egate, moe_b_egate, moe_w1, moe_w3, moe_w2, ln2_g, ln2_b)

    xf, xb, conv_p, conv_s = conv_layer(
        xf, xb, state_conv[0], conv_w_in[0], conv_b_in[0], conv_w_dw[0], conv_b_dw[0],
        conv_ln_g[0], conv_ln_b[0], conv_w_out[0], conv_b_out[0], ln1_g[0], ln1_b[0])
    xf, xb = hier_moe_block(xf, 0, *moe)

    xf, xb, hg_p, hg_s = hgrn_layer(xf, xb, state_hgrn[0], hgrn_w_in[0], lbs[1], hgrn_norm_g[0],
                                    hgrn_w_out[0], ln1_g[1], ln1_b[1])
    xf, xb = hier_moe_block(xf, 1, *moe)

    xf, xb, hr_p, hi_p, hr_s, hi_s = s5_layer(
        xf, xb, state_ssm_re[0], state_ssm_im[0], ssm_lam_re[0], ssm_lam_im[0], ssm_log_dt[0],
        ssm_b_re[0], ssm_b_im[0], ssm_c_re[0], ssm_c_im[0], ssm_d[0], ssm_w_glu[0], ln1_g[2], ln1_b[2])
    xf, xb = hier_moe_block(xf, 2, *moe)

    xf, xb, k_p, v_p, k_s, v_s = swa_layer(xf, xb, cache_swa_k[0], cache_swa_v[0], swa_w_qkv[0],
                                            swa_sinks[0], swa_w_out[0], ln1_g[3], ln1_b[3])
    xf, xb = hier_moe_block(xf, 3, *moe)

    y_p = xf[:NP].reshape(BATCH, SEQ, D)
    y_s = _sample_rows(xf)[:, :DEC_SEQ]
    return (y_p, y_s, conv_p[None], conv_s[None], hg_p[None], hg_s[None],
            hr_p[None], hi_p[None], hr_s[None], hi_s[None],
            k_p[None], v_p[None], k_s[None], v_s[None])
```

```python
import functools

import jax
import jax.numpy as jnp
from jax import lax
from jax.experimental import pallas as pl
from jax.experimental.pallas import tpu as pltpu

F32 = jnp.float32
BF16 = jnp.bfloat16

D = 2048
BATCH = 4
SEQ = 2048
DEPTH = 4
DEC_BATCH = 32
DEC_SEQ = 4
PAST_LEN = 16384
CONV_WIDTH = 31
HG_H = 16
HG_DK = 128
HG_DV = 128
SSM_GROUP = 16
SSM_GROUPS = 128
SSM_STATE = 64
HD = 64
NH = 32
NKV = 4
WINDOW = 128
ROPE_THETA = 10000.0
MOE_E = 16
MOE_FF = 512
ALPHA = (2.0 * DEPTH) ** 0.25
LN_EPS = 1e-5
RMS_EPS = 1e-6

TS = 8
NP = BATCH * SEQ
NS = DEC_BATCH * TS
MT = NP + NS
LANES = 128
SUBLANES = 8
VMEM_LIMIT = 56 * 1024 * 1024

TM_MM = 768
TN_MM = 512
TM_LN = 384
TM_E = 512
P_SLOTS = ((2 * MT + MOE_E * (TM_E - 1) + TM_E - 1) // TM_E) * TM_E
N_ETILES = P_SLOTS // TM_E
GATHER_CH = 64
SMEM_UNROLL = 8


def _cp(*sem):
    return pltpu.CompilerParams(dimension_semantics=sem, vmem_limit_bytes=VMEM_LIMIT)


def _sigmoid(x):
    return 1.0 / (1.0 + jnp.exp(-x))


def _silu(x):
    return x * _sigmoid(x)


def _ln_rows(z, g, b):
    mu = jnp.mean(z, -1, keepdims=True)
    zc = z - mu
    var = jnp.mean(zc * zc, -1, keepdims=True)
    return zc * lax.rsqrt(var + LN_EPS) * g + b


def _cast_weights_once(pairs):
    @pl.when(pl.program_id(1) == 0)
    def _():
        for src, dst in pairs:
            dst[...] = src[...].astype(BF16)


def _mm_glu_kernel(x_ref, wa_ref, wg_ref, ba_ref, bg_ref, o_ref, wab, wgb):
    _cast_weights_once(((wa_ref, wab), (wg_ref, wgb)))
    x = x_ref[...]
    a = jnp.dot(x, wab[...], preferred_element_type=F32) + ba_ref[...]
    g = jnp.dot(x, wgb[...], preferred_element_type=F32) + bg_ref[...]
    o_ref[...] = (a * _sigmoid(g)).astype(o_ref.dtype)


def mm_glu(x, w, b, out_dtype):
    m, k = x.shape
    n = w.shape[1] // 2
    nj = n // TN_MM
    return pl.pallas_call(
        _mm_glu_kernel,
        grid=(nj, m // TM_MM),
        in_specs=[
            pl.BlockSpec((TM_MM, k), lambda j, i: (i, 0)),
            pl.BlockSpec((k, TN_MM), lambda j, i: (0, j)),
            pl.BlockSpec((k, TN_MM), lambda j, i: (0, j + nj)),
            pl.BlockSpec((1, TN_MM), lambda j, i: (0, j)),
            pl.BlockSpec((1, TN_MM), lambda j, i: (0, j + nj)),
        ],
        out_specs=pl.BlockSpec((TM_MM, TN_MM), lambda j, i: (i, j)),
        out_shape=jax.ShapeDtypeStruct((m, n), out_dtype),
        scratch_shapes=[pltpu.VMEM((k, TN_MM), BF16), pltpu.VMEM((k, TN_MM), BF16)],
        compiler_params=_cp("parallel", "arbitrary"),
        name="mm_glu",
    )(x, w, w, b, b)


def _mm_hgrn_kernel(x_ref, w_ref, lb_ref, o_ref, wb, *, nsec):
    j = pl.program_id(0)
    _cast_weights_once(((w_ref, wb),))
    acc = jnp.dot(x_ref[...], wb[...], preferred_element_type=F32)

    @pl.when(j < nsec)
    def _():
        o_ref[...] = _silu(acc)

    @pl.when((j >= nsec) & (j < 2 * nsec))
    def _():
        lb = lb_ref[...]
        o_ref[...] = jnp.log(lb + (1.0 - lb) * _sigmoid(acc))

    @pl.when((j >= 2 * nsec) & (j < 3 * nsec))
    def _():
        o_ref[...] = acc

    @pl.when(j >= 3 * nsec)
    def _():
        o_ref[...] = _silu(acc)


def mm_hgrn(x, w, lb_row):
    m, k = x.shape
    n = w.shape[1]
    tn = 2 * TN_MM
    nsec = (n // 4) // tn
    return pl.pallas_call(
        functools.partial(_mm_hgrn_kernel, nsec=nsec),
        grid=(n // tn, m // TM_MM),
        in_specs=[
            pl.BlockSpec((TM_MM, k), lambda j, i: (i, 0)),
            pl.BlockSpec((k, tn), lambda j, i: (0, j)),
            pl.BlockSpec((1, tn), lambda j, i: (0, j)),
        ],
        out_specs=pl.BlockSpec((TM_MM, tn), lambda j, i: (i, j)),
        out_shape=jax.ShapeDtypeStruct((m, n), F32),
        scratch_shapes=[pltpu.VMEM((k, tn), BF16)],
        compiler_params=_cp("parallel", "arbitrary"),
        name="mm_hgrn",
    )(x, w, lb_row)


def _swap_half_heads(x, lane):
    return jnp.where((lane % HD) < HD // 2, pltpu.roll(x, LANES - HD // 2, 1),
                     pltpu.roll(x, HD // 2, 1))


def _mm_rope_kernel(x_ref, w_ref, cos_ref, sin_ref, o_ref, wb, *, n_rope, scale):
    _cast_weights_once(((w_ref, wb),))
    acc = jnp.dot(x_ref[...], wb[...], preferred_element_type=F32)
    cos = cos_ref[...]
    sin = sin_ref[...]
    lane = lax.broadcasted_iota(jnp.int32, cos.shape, 1)
    for c in range(acc.shape[1] // LANES):
        xc = acc[:, c * LANES:(c + 1) * LANES]
        if c < n_rope:
            xc = (xc * cos + _swap_half_heads(xc, lane) * sin) * scale
        o_ref[:, c * LANES:(c + 1) * LANES] = xc


def mm_rope(x, w, col0, n, cos, sin, n_rope, scale):
    m, k = x.shape
    j0 = col0 // TN_MM
    return pl.pallas_call(
        functools.partial(_mm_rope_kernel, n_rope=n_rope, scale=scale),
        grid=(n // TN_MM, m // TM_MM),
        in_specs=[
            pl.BlockSpec((TM_MM, k), lambda j, i: (i, 0)),
            pl.BlockSpec((k, TN_MM), lambda j, i: (0, j + j0)),
            pl.BlockSpec((TM_MM, LANES), lambda j, i: (i, 0)),
            pl.BlockSpec((TM_MM, LANES), lambda j, i: (i, 0)),
        ],
        out_specs=pl.BlockSpec((TM_MM, TN_MM), lambda j, i: (i, j)),
        out_shape=jax.ShapeDtypeStruct((m, n), F32),
        scratch_shapes=[pltpu.VMEM((k, TN_MM), BF16)],
        compiler_params=_cp("parallel", "arbitrary"),
        name="mm_rope",
    )(x, w, cos, sin)


def _pack_bf16_pairs(x):
    h = x.shape[1] // 2
    lo = lax.bitcast_convert_type(x[:, :h].astype(BF16).astype(F32), jnp.uint32) >> 16
    hi = lax.bitcast_convert_type(x[:, h:].astype(BF16).astype(F32), jnp.uint32) & jnp.uint32(0xFFFF0000)
    return lo | hi


def _unpack_bf16_pairs(u):
    lo = lax.bitcast_convert_type(u << 16, F32).astype(BF16)
    hi = lax.bitcast_convert_type(u & jnp.uint32(0xFFFF0000), F32).astype(BF16)
    return jnp.concatenate([lo, hi], axis=1)


def _mm_res_ln_kernel(x_ref, w_ref, b_ref, res_ref, g_ref, beta_ref, of_ref, op_ref):
    y = jnp.dot(x_ref[...], w_ref[...], preferred_element_type=F32) + b_ref[...]
    out = _ln_rows(ALPHA * res_ref[...] + y, g_ref[...], beta_ref[...])
    of_ref[...] = out
    op_ref[...] = _pack_bf16_pairs(out)


def mm_res_ln(x, w, b, res, g, beta):
    m, k = x.shape
    row = lambda i: (i, 0)
    fix = lambda i: (0, 0)
    return pl.pallas_call(
        _mm_res_ln_kernel,
        grid=(m // TM_LN,),
        in_specs=[
            pl.BlockSpec((TM_LN, k), row),
            pl.BlockSpec((k, D), fix),
            pl.BlockSpec((1, D), fix),
            pl.BlockSpec((TM_LN, D), row),
            pl.BlockSpec((1, D), fix),
            pl.BlockSpec((1, D), fix),
        ],
        out_specs=[pl.BlockSpec((TM_LN, D), row), pl.BlockSpec((TM_LN, D // 2), row)],
        out_shape=[jax.ShapeDtypeStruct((m, D), F32), jax.ShapeDtypeStruct((m, D // 2), jnp.uint32)],
        compiler_params=_cp("parallel"),
        name="mm_res_ln",
    )(x, w, b, res, g, beta)


def _res_ln_kernel(res_ref, y_ref, g_ref, beta_ref, of_ref, op_ref):
    out = _ln_rows(ALPHA * res_ref[...] + y_ref[...].astype(F32), g_ref[...], beta_ref[...])
    of_ref[...] = out
    op_ref[...] = _pack_bf16_pairs(out)


def res_ln(res, y, g, beta):
    m = res.shape[0]
    row = lambda i: (i, 0)
    fix = lambda i: (0, 0)
    return pl.pallas_call(
        _res_ln_kernel,
        grid=(m // TM_LN,),
        in_specs=[pl.BlockSpec((TM_LN, D), row), pl.BlockSpec((TM_LN, D), row),
                  pl.BlockSpec((1, D), fix), pl.BlockSpec((1, D), fix)],
        out_specs=[pl.BlockSpec((TM_LN, D), row), pl.BlockSpec((TM_LN, D // 2), row)],
        out_shape=[jax.ShapeDtypeStruct((m, D), F32), jax.ShapeDtypeStruct((m, D // 2), jnp.uint32)],
        compiler_params=_cp("parallel"),
        name="res_ln",
    )(res, y, g, beta)


HALO = 32
CONV_RC = 64
CONV_LC = 256


def _conv_kernel(u_ref, buf_ref, wdw_ref, bdw_ref, g_ref, b_ref, z_ref, ext_ref, sh_ref, c_ref, *, tt):
    i = pl.program_id(1)

    @pl.when(i == 0)
    def _():
        ext_ref[0:HALO, :] = buf_ref[0]

    @pl.when(i > 0)
    def _():
        ext_ref[0:HALO, :] = ext_ref[tt:tt + HALO, :]

    ext_ref[HALO:HALO + tt, :] = u_ref[...]
    nsh = tt + HALO - SUBLANES
    for b in range(1, SUBLANES):
        sh_ref[b - 1] = ext_ref[b:b + nsh, :]
    pad = HALO - (CONV_WIDTH - 1)
    rc = min(CONV_RC, tt)
    for r0 in range(0, tt, rc):
        for c0 in range(0, D, CONV_LC):
            acc = jnp.zeros((rc, CONV_LC), F32) + bdw_ref[:, c0:c0 + CONV_LC]
            for w in range(CONV_WIDTH):
                a, b = divmod(pad + w, SUBLANES)
                rows = slice(r0 + a * SUBLANES, r0 + a * SUBLANES + rc)
                if b == 0:
                    tap = ext_ref[rows, c0:c0 + CONV_LC]
                else:
                    tap = sh_ref[b - 1, rows, c0:c0 + CONV_LC]
                acc = acc + tap * wdw_ref[w:w + 1, c0:c0 + CONV_LC]
            c_ref[r0:r0 + rc, c0:c0 + CONV_LC] = acc
    z_ref[...] = _silu(_ln_rows(c_ref[...], g_ref[...], b_ref[...])).astype(z_ref.dtype)


def conv_ln_silu(u, row0_blocks, nseq, t, tt, buf, wdw, bdw, g, b, out_dtype):
    nt = t // tt
    fix = lambda s, i: (0, 0)
    return pl.pallas_call(
        functools.partial(_conv_kernel, tt=tt),
        grid=(nseq, nt),
        in_specs=[
            pl.BlockSpec((tt, D), lambda s, i: (row0_blocks + s * nt + i, 0)),
            pl.BlockSpec((1, HALO, D), lambda s, i: (s, 0, 0)),
            pl.BlockSpec((CONV_WIDTH, D), fix),
            pl.BlockSpec((1, D), fix),
            pl.BlockSpec((1, D), fix),
            pl.BlockSpec((1, D), fix),
        ],
        out_specs=pl.BlockSpec((tt, D), lambda s, i: (s * nt + i, 0)),
        out_shape=jax.ShapeDtypeStruct((nseq * t, D), out_dtype),
        scratch_shapes=[pltpu.VMEM((HALO + tt, D), F32),
                        pltpu.VMEM((SUBLANES - 1, tt + HALO - SUBLANES, D), F32),
                        pltpu.VMEM((tt, D), F32)],
        compiler_params=_cp("parallel", "arbitrary"),
        name="conv_ln_silu",
    )(u, buf, wdw, bdw, g, b)


def _hgrn_kernel(p_ref, s0_ref, ng_ref, o_ref, sf_ref, st_ref, pst_ref, ob_ref, bl_ref, *, tt, sc, nvalid):
    i = pl.program_id(1)
    hk = HG_H * HG_DK

    @pl.when(i == 0)
    def _():
        for h in range(HG_H):
            st_ref[h * HG_DV:(h + 1) * HG_DV, :] = s0_ref[0, h].T

    ones = jnp.ones((LANES, LANES), BF16)
    row = lax.broadcasted_iota(jnp.int32, (sc, hk), 0)
    first = [(s // SUBLANES) * SUBLANES for s in range(nvalid)]
    offs = [sum(sc - f for f in first[:s]) for s in range(nvalid + 1)]

    if nvalid == sc:
        rr = lax.broadcasted_iota(jnp.int32, (tt, tt), 0)
        cc = lax.broadcasted_iota(jnp.int32, (tt, tt), 1)
        tril = ((cc <= rr) & (cc // sc == rr // sc)).astype(F32)
        for h in range(HG_H):
            sl = slice(h * HG_DK, (h + 1) * HG_DK)
            bl_ref[:, sl] = jnp.dot(tril, p_ref[:, hk + h * HG_DK:hk + (h + 1) * HG_DK],
                                    preferred_element_type=F32, precision=lax.Precision.HIGHEST)

    def sub_chunk(ci, carry):
        r0 = pl.multiple_of(ci * sc, sc)
        q = p_ref[pl.ds(r0, sc), 0:hk]
        lf = p_ref[pl.ds(r0, sc), hk:2 * hk]
        v = p_ref[pl.ds(r0, sc), 2 * hk:3 * hk]
        if nvalid < sc:
            lf = jnp.where(row < nvalid, lf, 0.0)
            bl = jnp.zeros((sc, hk), F32)
            for s in range(nvalid):
                bl = bl + jnp.where(row >= s, lf[s:s + 1, :], 0.0)
        else:
            bl = bl_ref[pl.ds(r0, sc), :]
        kk = 1.0 - jnp.exp(lf)
        btot = bl[sc - 1:sc, :]
        qd = q * jnp.exp(bl)
        kd = kk * jnp.exp(btot - bl)
        for s in range(nvalid):
            f = first[s]
            rowf = lax.broadcasted_iota(jnp.int32, (sc - f, hk), 0) + f
            e = jnp.exp(jnp.where(rowf >= s, bl[f:] - bl[s:s + 1, :], -jnp.inf))
            pst_ref[offs[s]:offs[s + 1], :] = q[f:] * e * kk[s:s + 1, :]
        o_acc = []
        for h in range(HG_H):
            sl = slice(h * HG_DK, (h + 1) * HG_DK)
            rs = jnp.dot(pst_ref[0:offs[nvalid], sl].astype(BF16), ones,
                         preferred_element_type=F32)
            vh = v[:, sl]
            st_h = st_ref[h * HG_DV:(h + 1) * HG_DV, :]
            oh = lax.dot_general(qd[:, sl].astype(BF16), st_h.astype(BF16),
                                 (((1,), (1,)), ((), ())), preferred_element_type=F32)
            tiles = [oh[f:f + SUBLANES] for f in range(0, sc, SUBLANES)]
            for s in range(nvalid):
                for ti, f in enumerate(range(first[s], sc, SUBLANES)):
                    piece = rs[offs[s] + ti * SUBLANES:offs[s] + (ti + 1) * SUBLANES, :]
                    tiles[f // SUBLANES] = tiles[f // SUBLANES] + piece * vh[s:s + 1, :]
            o_acc.append(jnp.concatenate(tiles, axis=0) if len(tiles) > 1 else tiles[0])
            upd = jnp.dot(vh.T.astype(BF16), kd[:, sl].astype(BF16), preferred_element_type=F32)
            st_ref[h * HG_DV:(h + 1) * HG_DV, :] = jnp.exp(btot[:, sl]) * st_h + upd
        ob_ref[pl.ds(r0, sc), :] = jnp.concatenate(o_acc, axis=1)
        return carry

    lax.fori_loop(0, tt // sc, sub_chunk, 0)

    o = ob_ref[...]
    o2 = o * o
    parts = []
    for h in range(HG_H):
        sl = slice(h * HG_DV, (h + 1) * HG_DV)
        ms = jnp.dot(o2[:, sl].astype(BF16), ones, preferred_element_type=F32) * (1.0 / HG_DV)
        parts.append(o[:, sl] * lax.rsqrt(ms + RMS_EPS))
    on = jnp.concatenate(parts, axis=1) * ng_ref[...] * p_ref[:, 3 * hk:4 * hk]
    o_ref[...] = on.astype(o_ref.dtype)

    @pl.when(i == pl.num_programs(1) - 1)
    def _():
        for h in range(HG_H):
            sf_ref[0, h] = st_ref[h * HG_DV:(h + 1) * HG_DV, :].T


def hgrn_scan(p, row0_blocks, nseq, t, tt, sc, nvalid, s0, norm_g, out_dtype):
    nt = t // tt
    return pl.pallas_call(
        functools.partial(_hgrn_kernel, tt=tt, sc=sc, nvalid=nvalid),
        grid=(nseq, nt),
        in_specs=[
            pl.BlockSpec((tt, 4 * D), lambda s, i: (row0_blocks + s * nt + i, 0)),
            pl.BlockSpec((1, HG_H, HG_DK, HG_DV), lambda s, i: (s, 0, 0, 0)),
            pl.BlockSpec((1, D), lambda s, i: (0, 0)),
        ],
        out_specs=[
            pl.BlockSpec((tt, D), lambda s, i: (s * nt + i, 0)),
            pl.BlockSpec((1, HG_H, HG_DK, HG_DV), lambda s, i: (s, 0, 0, 0)),
        ],
        out_shape=[jax.ShapeDtypeStruct((nseq * t, D), out_dtype),
                   jax.ShapeDtypeStruct((nseq, HG_H, HG_DK, HG_DV), F32)],
        scratch_shapes=[pltpu.VMEM((HG_H * HG_DV, HG_DK), F32),
                        pltpu.VMEM((sc * sc, D), F32),
                        pltpu.VMEM((tt, D), F32),
                        pltpu.VMEM((tt, D), F32)],
        compiler_params=_cp("parallel", "arbitrary"),
        name="hgrn_scan",
    )(p, s0, norm_g)


S5_GPB = LANES // SSM_GROUP
S5_NJ = D // LANES
S5_SW = S5_GPB * SSM_STATE
S5_LC = 8


def _spread_groups(blk, rep, mask):
    full = jnp.dot(blk.astype(BF16), rep, preferred_element_type=F32)
    return jnp.where(mask, full, 0.0).astype(BF16)


def _group_mask(rows, row_w, cols, col_w):
    r = lax.broadcasted_iota(jnp.int32, (rows, cols), 0) // row_w
    c = lax.broadcasted_iota(jnp.int32, (rows, cols), 1) // col_w
    return r == c


def _rep_matrix(w, cols):
    r = lax.broadcasted_iota(jnp.int32, (w, cols), 0)
    c = lax.broadcasted_iota(jnp.int32, (w, cols), 1) % w
    return (r == c).astype(BF16)


def _s5_kernel(u_ref, h0_ref, klt_ref, wtr_ref, wti_ref, car_ref, cai_ref, ar_ref, ai_ref, y_ref, hf_ref,
               t_s, bp_s, cp_s, con_ref, hp_ref, *, nc, nv):
    sw = S5_SW
    lc = S5_LC
    c, p = SSM_GROUP, SSM_STATE
    rep_c = _rep_matrix(c, LANES)
    rep_p = _rep_matrix(p, sw)
    t_s[...] = jnp.zeros(t_s.shape, BF16)
    mask_t = _group_mask(LANES, c, LANES, c)
    for l in range(lc):
        blk = _spread_groups(klt_ref[0, l], rep_c, mask_t)
        for s in range(lc - l):
            t_s[s * LANES:(s + 1) * LANES, (s + l) * LANES:(s + l + 1) * LANES] = blk
    mask_b = _group_mask(LANES, c, sw, p)
    for s in range(lc):
        if s < nv:
            bp_s[s * LANES:(s + 1) * LANES, 0:sw] = _spread_groups(wtr_ref[0, nv - 1 - s], rep_p, mask_b)
            bp_s[s * LANES:(s + 1) * LANES, sw:2 * sw] = _spread_groups(wti_ref[0, nv - 1 - s], rep_p, mask_b)
        else:
            bp_s[s * LANES:(s + 1) * LANES, :] = jnp.zeros((LANES, 2 * sw), BF16)
    mask_c = _group_mask(sw, p, LANES, c)
    for t in range(lc):
        cp_s[0:sw, t * LANES:(t + 1) * LANES] = _spread_groups(car_ref[0, t], rep_c, mask_c)
        cp_s[sw:2 * sw, t * LANES:(t + 1) * LANES] = _spread_groups(cai_ref[0, t], rep_c, mask_c)

    u = u_ref[0]
    con_ref[...] = jnp.dot(u, bp_s[...], preferred_element_type=F32)
    ar = ar_ref[0]
    ai = ai_ref[0]

    def step(hr, hi, cr, ci):
        return ar * hr - ai * hi + cr, ar * hi + ai * hr + ci

    if nc == 1:
        hr, hi = h0_ref[0, :, 0:sw], h0_ref[0, :, sw:2 * sw]
        hp_ref[:, 0:sw] = hr
        hp_ref[:, sw:2 * sw] = hi
        hr, hi = step(hr, hi, con_ref[:, 0:sw], con_ref[:, sw:2 * sw])
    else:
        half = 4
        top = lax.broadcasted_iota(jnp.int32, (2 * half, sw), 0) < half

        def pair(k, carry):
            hr, hi = carry
            r0 = pl.multiple_of(k * 2 * half, 2 * half)
            cr = con_ref[pl.ds(r0, 2 * half), 0:sw]
            ci = con_ref[pl.ds(r0, 2 * half), sw:2 * sw]
            ar1, ai1 = step(hr, hi, cr, ci)
            ar1 = jnp.where(top, ar1, pltpu.roll(ar1, half, 0))
            ai1 = jnp.where(top, ai1, pltpu.roll(ai1, half, 0))
            hp_ref[pl.ds(r0, 2 * half), 0:sw] = jnp.where(top, hr, ar1)
            hp_ref[pl.ds(r0, 2 * half), sw:2 * sw] = jnp.where(top, hi, ai1)
            br, bi = step(ar1, ai1, cr, ci)
            br = jnp.where(top, pltpu.roll(br, half, 0), br)
            bi = jnp.where(top, pltpu.roll(bi, half, 0), bi)
            return br, bi

        hr, hi = lax.fori_loop(0, nc // 2, pair, (h0_ref[0, :, 0:sw], h0_ref[0, :, sw:2 * sw]))
    hf_ref[0, :, 0:sw] = hr
    hf_ref[0, :, sw:2 * sw] = hi
    y = (jnp.dot(u, t_s[...], preferred_element_type=F32)
         + jnp.dot(hp_ref[...].astype(BF16), cp_s[...], preferred_element_type=F32))
    y_ref[0] = jax.nn.gelu(y).astype(y_ref.dtype)


def s5_lti(u, h0, ops, nc, nv):
    klt, wtr, wti, car, cai, pw_re, pw_im = ops
    nj, rows, lw = u.shape
    hrows = h0.shape[1]
    sw, lc, c, p = S5_SW, S5_LC, SSM_GROUP, SSM_STATE
    ar = pw_re[:, nv].reshape(nj, 1, sw)
    ai = pw_im[:, nv].reshape(nj, 1, sw)
    blk = lambda j: (j, 0, 0)
    blk4 = lambda j: (j, 0, 0, 0)
    return pl.pallas_call(
        functools.partial(_s5_kernel, nc=nc, nv=nv),
        grid=(nj,),
        in_specs=[
            pl.BlockSpec((1, rows, lw), blk),
            pl.BlockSpec((1, hrows, 2 * sw), blk),
            pl.BlockSpec((1, lc, LANES, c), blk4),
            pl.BlockSpec((1, lc, LANES, p), blk4),
            pl.BlockSpec((1, lc, LANES, p), blk4),
            pl.BlockSpec((1, lc, sw, c), blk4),
            pl.BlockSpec((1, lc, sw, c), blk4),
            pl.BlockSpec((1, 1, sw), blk),
            pl.BlockSpec((1, 1, sw), blk),
        ],
        out_specs=[pl.BlockSpec((1, rows, lw), blk), pl.BlockSpec((1, hrows, 2 * sw), blk)],
        out_shape=[jax.ShapeDtypeStruct((nj, rows, lw), BF16),
                   jax.ShapeDtypeStruct((nj, hrows, 2 * sw), F32)],
        scratch_shapes=[pltpu.VMEM((lw, lw), BF16), pltpu.VMEM((lw, 2 * sw), BF16),
                        pltpu.VMEM((2 * sw, lw), BF16),
                        pltpu.VMEM((rows, 2 * sw), F32), pltpu.VMEM((rows, 2 * sw), F32)],
        compiler_params=_cp("parallel"),
        name="s5_lti",
    )(u, h0, klt, wtr, wti, car, cai, ar, ai)


def _s5_operators(lam_re, lam_im, log_dt, b_re, b_im, c_re, c_im, d_skip):
    hp = lax.Precision.HIGHEST
    gpb, nj, c, p, lc = S5_GPB, S5_NJ, SSM_GROUP, SSM_STATE, S5_LC
    dt = jnp.exp(log_dt)[:, None]
    lr, li = lam_re, lam_im
    mag = jnp.exp(lr * dt)
    ab_re, ab_im = mag * jnp.cos(li * dt), mag * jnp.sin(li * dt)
    nr, ni = ab_re - 1.0, ab_im
    den = lr * lr + li * li
    z_re, z_im = (nr * lr + ni * li) / den, (ni * lr - nr * li) / den
    bbt_re = (z_re[..., None] * b_re - z_im[..., None] * b_im).transpose(0, 2, 1)
    bbt_im = (z_re[..., None] * b_im + z_im[..., None] * b_re).transpose(0, 2, 1)
    l = jnp.arange(lc + 1, dtype=F32)[None, :, None]
    pmag = jnp.exp(l * (lr * dt)[:, None, :])
    pw_re = pmag * jnp.cos(l * (li * dt)[:, None, :])
    pw_im = pmag * jnp.sin(l * (li * dt)[:, None, :])
    wt_re = pw_re[:, :lc, None, :] * bbt_re[:, None] - pw_im[:, :lc, None, :] * bbt_im[:, None]
    wt_im = pw_re[:, :lc, None, :] * bbt_im[:, None] + pw_im[:, :lc, None, :] * bbt_re[:, None]
    klt = (jnp.einsum('glcp,gop->glco', wt_re, c_re, precision=hp)
           - jnp.einsum('glcp,gop->glco', wt_im, c_im, precision=hp))
    klt = klt.at[:, 0].add(d_skip.reshape(SSM_GROUPS, c)[:, :, None] * jnp.eye(c, dtype=F32)[None])
    ct_re, ct_im = c_re.transpose(0, 2, 1)[:, None], c_im.transpose(0, 2, 1)[:, None]
    ca_re = ct_re * pw_re[:, 1:, :, None] - ct_im * pw_im[:, 1:, :, None]
    ca_im = ct_re * pw_im[:, 1:, :, None] + ct_im * pw_re[:, 1:, :, None]

    def by_block(a):
        _, ll, r, w = a.shape
        return a.reshape(nj, gpb, ll, r, w).transpose(0, 2, 1, 3, 4).reshape(nj, ll, gpb * r, w)

    def pw_block(a):
        return a.reshape(nj, gpb, lc + 1, p).transpose(0, 2, 1, 3).reshape(nj, lc + 1, gpb * p)

    return (by_block(klt), by_block(wt_re), by_block(wt_im), by_block(ca_re), by_block(-ca_im),
            pw_block(pw_re), pw_block(pw_im))


def _attn_kernel(q_ref, kv_ref, pk_ref, pv_ref, sink_ref, o_ref, *, tq, nblk, always_prev):
    idx = pl.program_id(0)
    kvw = NKV * HD
    group = NH // NKV
    pkn = pk_ref.shape[1]
    to = kv_ref.shape[1]
    nk = pkn + to
    rows = group * tq
    a = lax.broadcasted_iota(jnp.int32, (rows, nk), 0) % tq
    col = lax.broadcasted_iota(jnp.int32, (rows, nk), 1)
    prev_ok = col >= a
    if not always_prev:
        prev_ok = prev_ok & ((idx % nblk) > 0)
    mask = ((col < pkn) & prev_ok) | ((col >= pkn) & ((col - pkn) <= a))
    hrow = lax.broadcasted_iota(jnp.int32, (rows, 1), 0) // tq
    low = lax.broadcasted_iota(jnp.int32, (tq, LANES), 1) < HD
    nt = (((1,), (1,)), ((), ()))
    for kb in range(NKV // 2):
        lanes = slice(kb * LANES, (kb + 1) * LANES)
        kcat = jnp.concatenate([pk_ref[0, :, lanes], kv_ref[0, :, lanes]], axis=0).astype(BF16)
        vcat = jnp.concatenate([pv_ref[0, :, lanes],
                                kv_ref[0, :, kvw + kb * LANES:kvw + (kb + 1) * LANES]], axis=0).astype(BF16)
        for hi in range(2):
            j = 2 * kb + hi
            parts = []
            sk = jnp.zeros((rows, 1), F32)
            for h8 in range(group):
                qb, e = (group * j + h8) // 2, h8 % 2
                q2 = q_ref[0, :, qb * LANES:(qb + 1) * LANES]
                qe = jnp.where(low if e == 0 else jnp.logical_not(low), q2, 0.0)
                if e != hi:
                    qe = pltpu.roll(qe, HD, 1)
                parts.append(qe)
                sk = jnp.where(hrow == h8, sink_ref[group * j + h8], sk)
            q8 = jnp.concatenate(parts, axis=0).astype(BF16)
            s = lax.dot_general(q8, kcat, nt, preferred_element_type=F32)
            s = jnp.where(mask, s, -jnp.inf)
            m = jnp.maximum(jnp.max(s, -1, keepdims=True), sk)
            p = jnp.exp(s - m)
            inv = 1.0 / (jnp.sum(p, -1, keepdims=True) + jnp.exp(sk - m))
            o8 = jnp.dot((p * inv).astype(BF16), vcat, preferred_element_type=F32)
            for pr in range(group // 2):
                qb = (group * j) // 2 + pr
                pair = []
                for e in range(2):
                    oe = o8[(2 * pr + e) * tq:(2 * pr + e + 1) * tq, :]
                    if e != hi:
                        oe = pltpu.roll(oe, HD, 1)
                    pair.append(oe)
                o_ref[0, :, qb * LANES:(qb + 1) * LANES] = jnp.where(low, pair[0], pair[1]).astype(o_ref.dtype)


def swa_attention(q3, kv3, pk3, pv3, nb, blk0, pk_map, pv_map, sinks, nblk, always_prev):
    tq = q3.shape[1]
    to = kv3.shape[1]
    kvw = NKV * HD
    return pl.pallas_call(
        functools.partial(_attn_kernel, tq=tq, nblk=nblk, always_prev=always_prev),
        grid=(nb,),
        in_specs=[
            pl.BlockSpec((1, tq, D), lambda i: (i + blk0, 0, 0)),
            pl.BlockSpec((1, to, 2 * kvw), lambda i: (i, 0, 0)),
            pl.BlockSpec((1, pk3.shape[1], kvw), pk_map),
            pl.BlockSpec((1, pv3.shape[1], kvw), pv_map),
            pl.BlockSpec(memory_space=pltpu.SMEM),
        ],
        out_specs=pl.BlockSpec((1, tq, D), lambda i: (i, 0, 0)),
        out_shape=jax.ShapeDtypeStruct((nb, tq, D), BF16),
        compiler_params=_cp("parallel"),
        name="swa_attention",
    )(q3, kv3, pk3, pv3, sinks)


R_E1, R_E2, R_W1, R_W2, R_R1, R_R2 = 0, 1, 2, 3, 4, 5


def _router_kernel(x_ref, wh_ref, wl_ref, b_ref, o_ref, cnt_ref, carry_ref):
    i = pl.program_id(0)

    @pl.when(i == 0)
    def _():
        carry_ref[...] = jnp.zeros(carry_ref.shape, F32)

    x = x_ref[...]
    xh = x.astype(BF16)
    xl = (x - xh.astype(F32)).astype(BF16)
    wh = wh_ref[...]
    l = (jnp.dot(xh, wh, preferred_element_type=F32) + jnp.dot(xl, wh, preferred_element_type=F32)
         + jnp.dot(xh, wl_ref[...], preferred_element_type=F32) + b_ref[...])
    tm = l.shape[0]
    lane = lax.broadcasted_iota(jnp.int32, l.shape, 1).astype(F32)
    big = 1e9
    ninf = -jnp.inf
    gmask = lane < 4.0
    gl = jnp.where(gmask, l, ninf)
    gm = jnp.max(gl, -1, keepdims=True)
    gsel = jnp.min(jnp.where(gl == gm, lane, big), -1, keepdims=True)
    gsum = jnp.sum(jnp.where(gmask, jnp.exp(gl - gm), 0.0), -1, keepdims=True)
    gval = 1.0 / gsum
    lo = 4.0 + 4.0 * gsel
    emask = (lane >= lo) & (lane < lo + 4.0)
    el = jnp.where(emask, l, ninf)
    m1 = jnp.max(el, -1, keepdims=True)
    i1 = jnp.min(jnp.where(emask & (el == m1), lane, big), -1, keepdims=True)
    emask2 = emask & (lane != i1)
    el2 = jnp.where(emask2, l, ninf)
    m2 = jnp.max(el2, -1, keepdims=True)
    i2 = jnp.min(jnp.where(emask2 & (el2 == m2), lane, big), -1, keepdims=True)
    r = jnp.exp(m2 - m1)
    w1 = gval / (1.0 + r)
    w2 = gval * r / (1.0 + r)
    e1 = i1 - 4.0
    e2 = i2 - 4.0
    oh = ((lane == e1) | (lane == e2)).astype(F32)
    tri = (lax.broadcasted_iota(jnp.int32, (tm, tm), 1)
           < lax.broadcasted_iota(jnp.int32, (tm, tm), 0)).astype(BF16)
    before = jnp.dot(tri, oh.astype(BF16), preferred_element_type=F32) + carry_ref[...]
    r1 = jnp.sum(jnp.where(lane == e1, before, 0.0), -1, keepdims=True)
    r2 = jnp.sum(jnp.where(lane == e2, before, 0.0), -1, keepdims=True)
    carry_ref[...] = carry_ref[...] + jnp.sum(oh, axis=0, keepdims=True)
    cnt_ref[...] = carry_ref[...]
    rec = jnp.zeros(l.shape, F32)
    for ln, val in ((R_E1, e1), (R_E2, e2), (R_W1, w1), (R_W2, w2), (R_R1, r1), (R_R2, r2)):
        rec = jnp.where(lane == float(ln), val, rec)
    o_ref[...] = rec


def moe_router(x, w, b):
    m = x.shape[0]
    wh = w.astype(BF16)
    wl = (w - wh.astype(F32)).astype(BF16)
    return pl.pallas_call(
        _router_kernel,
        grid=(m // TM_LN,),
        in_specs=[pl.BlockSpec((TM_LN, D), lambda i: (i, 0)),
                  pl.BlockSpec((D, LANES), lambda i: (0, 0)),
                  pl.BlockSpec((D, LANES), lambda i: (0, 0)),
                  pl.BlockSpec((1, LANES), lambda i: (0, 0))],
        out_specs=[pl.BlockSpec((TM_LN, LANES), lambda i: (i, 0)),
                   pl.BlockSpec((1, LANES), lambda i: (0, 0))],
        out_shape=[jax.ShapeDtypeStruct((m, LANES), F32), jax.ShapeDtypeStruct((1, LANES), F32)],
        scratch_shapes=[pltpu.VMEM((1, LANES), F32)],
        compiler_params=_cp("arbitrary"),
        name="moe_router",
    )(x, wh, wl, b)


def _slots_kernel(rec_ref, starts_ref, o_ref):
    rec = rec_ref[...]
    lane = lax.broadcasted_iota(jnp.int32, rec.shape, 1).astype(F32)
    starts = starts_ref[...]
    out = jnp.zeros(rec.shape, F32)
    for k, (le, lr) in enumerate(((R_E1, R_R1), (R_E2, R_R2))):
        e = rec[:, le:le + 1]
        s = jnp.sum(jnp.where(lane == e, starts, 0.0), -1, keepdims=True) + rec[:, lr:lr + 1]
        out = jnp.where(lane == float(k), s, out)
    o_ref[...] = out


def moe_slots(rec, starts_row):
    m = rec.shape[0]
    return pl.pallas_call(
        _slots_kernel,
        grid=(m // TM_MM,),
        in_specs=[pl.BlockSpec((TM_MM, LANES), lambda i: (i, 0)), pl.BlockSpec((1, LANES), lambda i: (0, 0))],
        out_specs=pl.BlockSpec((TM_MM, LANES), lambda i: (i, 0)),
        out_shape=jax.ShapeDtypeStruct((m, LANES), F32),
        compiler_params=_cp("parallel"),
        name="moe_slots",
    )(rec, starts_row)


def _row_gather_start(idx_ref, base, nchunks, src_hbm, dst, sem):
    def body(c, carry):
        off = pl.multiple_of(c * GATHER_CH, GATHER_CH)
        dst_c = dst.at[pl.ds(off, GATHER_CH)]
        for u in range(GATHER_CH):
            pltpu.make_async_copy(src_hbm.at[pl.ds(idx_ref[base + off + u], 1)],
                                  dst_c.at[pl.ds(u, 1)], sem).start()
        return carry
    lax.fori_loop(0, nchunks, body, 0)


def _row_gather_wait(nchunks, src_hbm, dst, sem):
    def body(c, carry):
        pltpu.make_async_copy(src_hbm.at[pl.ds(0, GATHER_CH)], dst.at[pl.ds(0, GATHER_CH)], sem).wait()
        return carry
    lax.fori_loop(0, nchunks, body, 0)


def _ffn_kernel(te_ref, nt_ref, nch_ref, pos_ref, x_hbm, w1_ref, w3_ref, w2_ref, y_ref, xbuf, sem,
                w1b, w3b, w2b, src_ref):
    i = pl.program_id(0)
    nt = nt_ref[0]
    slot = i % 2

    @pl.when(i == 0)
    def _():
        xbuf[...] = jnp.zeros(xbuf.shape, xbuf.dtype)
        def clear(k, carry):
            for u in range(SMEM_UNROLL):
                src_ref[k * SMEM_UNROLL + u] = 0
            return carry
        lax.fori_loop(0, P_SLOTS // SMEM_UNROLL, clear, 0)

        def scatter(k, carry):
            t, rem = k // (TM_LN // SMEM_UNROLL), k % (TM_LN // SMEM_UNROLL)
            for half in range(2):
                for u in range(SMEM_UNROLL):
                    r = rem * SMEM_UNROLL + u
                    src_ref[pos_ref[t * 2 * TM_LN + half * TM_LN + r]] = t * TM_LN + r
            return carry
        lax.fori_loop(0, MT // SMEM_UNROLL, scatter, 0)
        _row_gather_start(src_ref, 0, nch_ref[0], x_hbm, xbuf.at[0], sem.at[0])

    @pl.when(i + 1 < nt)
    def _():
        _row_gather_start(src_ref, (i + 1) * TM_E, nch_ref[i + 1], x_hbm,
                          xbuf.at[1 - slot], sem.at[1 - slot])

    @pl.when(i < nt)
    def _():
        prev = te_ref[jnp.maximum(i - 1, 0)]

        @pl.when((i == 0) | (te_ref[i] != prev))
        def _():
            w1b[...] = w1_ref[0, 0].astype(BF16)
            w3b[...] = w3_ref[0, 0].astype(BF16)
            w2b[...] = w2_ref[0, 0].astype(BF16)

        _row_gather_wait(nch_ref[i], x_hbm, xbuf.at[slot], sem.at[slot])
        x = _unpack_bf16_pairs(xbuf[slot])
        h1 = jnp.dot(x, w1b[...], preferred_element_type=F32)
        h3 = jnp.dot(x, w3b[...], preferred_element_type=F32)
        h = _silu(h1) * h3
        y_ref[...] = jnp.dot(h.astype(BF16), w2b[...], preferred_element_type=F32)

    @pl.when(i >= nt)
    def _():
        y_ref[...] = jnp.zeros(y_ref.shape, y_ref.dtype)


def moe_ffn(layer, tile_expert, n_tiles, n_chunks, pos_tiles, x, w1, w3, w2):
    wmap = lambda i, te, nt, nch, pos: (layer, te[i], 0, 0)
    grid_spec = pltpu.PrefetchScalarGridSpec(
        num_scalar_prefetch=4,
        grid=(N_ETILES,),
        in_specs=[
            pl.BlockSpec(memory_space=pl.ANY),
            pl.BlockSpec((1, 1, D, MOE_FF), wmap),
            pl.BlockSpec((1, 1, D, MOE_FF), wmap),
            pl.BlockSpec((1, 1, MOE_FF, D), wmap),
        ],
        out_specs=pl.BlockSpec((TM_E, D), lambda i, te, nt, nch, pos: (i, 0)),
        scratch_shapes=[pltpu.VMEM((2, TM_E, D // 2), jnp.uint32), pltpu.SemaphoreType.DMA((2,)),
                        pltpu.VMEM((D, MOE_FF), BF16), pltpu.VMEM((D, MOE_FF), BF16),
                        pltpu.VMEM((MOE_FF, D), BF16), pltpu.SMEM((P_SLOTS,), jnp.int32)],
    )
    return pl.pallas_call(
        _ffn_kernel,
        grid_spec=grid_spec,
        out_shape=jax.ShapeDtypeStruct((P_SLOTS, D), F32),
        compiler_params=_cp("arbitrary"),
        name="moe_ffn",
    )(tile_expert, n_tiles, n_chunks, pos_tiles, x, w1, w3, w2)


def _combine_ln_kernel(pos_ref, res_ref, rec_ref, g_ref, beta_ref, ys_hbm, of_ref, ob_ref, ybuf, sem):
    i = pl.program_id(0)
    slot = i % 2
    n = 2 * TM_LN
    nch = n // GATHER_CH

    @pl.when(i == 0)
    def _():
        _row_gather_start(pos_ref, 0, nch, ys_hbm, ybuf.at[0], sem.at[0])

    @pl.when(i + 1 < pl.num_programs(0))
    def _():
        _row_gather_start(pos_ref, (i + 1) * n, nch, ys_hbm, ybuf.at[1 - slot], sem.at[1 - slot])

    _row_gather_wait(nch, ys_hbm, ybuf.at[slot], sem.at[slot])
    rec = rec_ref[...]
    y = (rec[:, R_W1:R_W1 + 1] * ybuf[slot, 0:TM_LN, :]
         + rec[:, R_W2:R_W2 + 1] * ybuf[slot, TM_LN:2 * TM_LN, :])
    out = _ln_rows(ALPHA * res_ref[...] + y, g_ref[...], beta_ref[...])
    of_ref[...] = out
    ob_ref[...] = out.astype(BF16)


def combine_ln(pos_tiles, res, rec, g, beta, ys):
    m = res.shape[0]
    row = lambda i, pos: (i, 0)
    fix = lambda i, pos: (0, 0)
    grid_spec = pltpu.PrefetchScalarGridSpec(
        num_scalar_prefetch=1,
        grid=(m // TM_LN,),
        in_specs=[pl.BlockSpec((TM_LN, D), row), pl.BlockSpec((TM_LN, LANES), row),
                  pl.BlockSpec((1, D), fix), pl.BlockSpec((1, D), fix),
                  pl.BlockSpec(memory_space=pl.ANY)],
        out_specs=[pl.BlockSpec((TM_LN, D), row), pl.BlockSpec((TM_LN, D), row)],
        scratch_shapes=[pltpu.VMEM((2, 2 * TM_LN, D), F32), pltpu.SemaphoreType.DMA((2,))],
    )
    return pl.pallas_call(
        _combine_ln_kernel,
        grid_spec=grid_spec,
        out_shape=[jax.ShapeDtypeStruct((m, D), F32), jax.ShapeDtypeStruct((m, D), BF16)],
        compiler_params=_cp("arbitrary"),
        name="combine_ln",
    )(pos_tiles, res, rec, g, beta, ys)


def _dispatch_plan(rec, counts_row):
    counts = counts_row[0, :MOE_E].astype(jnp.int32)
    padded = ((counts + TM_E - 1) // TM_E) * TM_E
    ends = jnp.cumsum(padded)
    starts = ends - padded
    starts_row = jnp.pad(starts.astype(F32), (0, LANES - MOE_E))[None, :]
    pos = moe_slots(rec, starts_row)[:, 0:2].astype(jnp.int32)
    n_tiles = (ends[-1] // TM_E).astype(jnp.int32)
    tile_start = jnp.arange(N_ETILES, dtype=jnp.int32) * TM_E
    te = jnp.sum((tile_start[:, None] >= ends[None, :]).astype(jnp.int32), axis=1)
    last = jnp.sum(jnp.where(jnp.arange(N_ETILES) == n_tiles - 1, te, 0))
    live = jnp.arange(N_ETILES) < n_tiles
    te = jnp.minimum(jnp.where(live, te, last), MOE_E - 1).astype(jnp.int32)
    ids = jnp.arange(MOE_E, dtype=jnp.int32)[None, :]
    seg_end = jnp.sum(jnp.where(te[:, None] == ids, (starts + counts)[None, :], 0), axis=1)
    n_rows = jnp.clip(seg_end - tile_start, 0, TM_E)
    n_chunks = jnp.where(live, (n_rows + GATHER_CH - 1) // GATHER_CH, 0).astype(jnp.int32)
    pos_tiles = pos.reshape(MT // TM_LN, TM_LN, 2).transpose(0, 2, 1).reshape(-1)
    return te, n_tiles.reshape(1), n_chunks, pos_tiles


def hier_moe_block(xf, xp, i, moe_w_group, moe_b_group, moe_w_egate, moe_b_egate,
                   moe_w1, moe_w3, moe_w2, ln_g, ln_b):
    wr = jnp.concatenate([moe_w_group[i], moe_w_egate[i].transpose(1, 0, 2).reshape(D, MOE_E)], axis=1)
    wr = jnp.pad(wr, ((0, 0), (0, LANES - wr.shape[1])))
    br = jnp.concatenate([moe_b_group[i], moe_b_egate[i].reshape(MOE_E)])
    br = jnp.pad(br, (0, LANES - br.shape[0]))[None, :]
    rec, counts = moe_router(xf, wr, br)
    te, n_tiles, n_chunks, pos_tiles = _dispatch_plan(rec, counts)
    ys = moe_ffn(i, te, n_tiles, n_chunks, pos_tiles, xp, moe_w1, moe_w3, moe_w2)
    return combine_ln(pos_tiles, xf, rec, ln_g[i][None, :], ln_b[i][None, :], ys)


def _sample_rows(a):
    return a[NP:].reshape(DEC_BATCH, TS, a.shape[-1])


def _prompt_tail(a, n):
    return jnp.stack([a[(b + 1) * SEQ - n:(b + 1) * SEQ] for b in range(BATCH)])


def conv_layer(xf, xb, state, w_in, b_in, w_dw, b_dw, g, b, w_out, b_out, ln_g, ln_b):
    u = mm_glu(xb, w_in, b_in[None, :], F32)
    tt_p = 256
    zbuf = jnp.zeros((BATCH, HALO, D), F32)
    z_p = conv_ln_silu(u, 0, BATCH, SEQ, tt_p, zbuf, w_dw, b_dw[None, :], g[None, :], b[None, :], BF16)
    pad = HALO - (CONV_WIDTH - 1)
    sbuf = jnp.pad(state, ((0, 0), (pad, 0), (0, 0)))
    z_s = conv_ln_silu(u, NP // TS, DEC_BATCH, TS, TS, sbuf, w_dw, b_dw[None, :], g[None, :], b[None, :], F32)
    z = jnp.concatenate([z_p, z_s.astype(BF16)], axis=0)
    xf, xp = mm_res_ln(z, w_out.astype(BF16), b_out[None, :], xf, ln_g[None, :], ln_b[None, :])
    new_p = _prompt_tail(u, CONV_WIDTH - 1)
    new_s = jnp.concatenate([state[:, DEC_SEQ:], _sample_rows(u)[:, :DEC_SEQ]], axis=1)
    return xf, xp, new_p, new_s


def hgrn_layer(xf, xb, state, w_in, lb, norm_g, w_out, ln_g, ln_b):
    hk = HG_H * HG_DK
    lb_row = jnp.concatenate([jnp.zeros((hk,), F32), lb, jnp.zeros((2 * hk,), F32)])[None, :]
    p = mm_hgrn(xb, w_in, lb_row)
    s0_p = jnp.zeros((BATCH, HG_H, HG_DK, HG_DV), F32)
    o_p, sf_p = hgrn_scan(p, 0, BATCH, SEQ, 128, 16, 16, s0_p, norm_g[None, :], BF16)
    o_s, sf_s = hgrn_scan(p, NP // TS, DEC_BATCH, TS, TS, TS, DEC_SEQ, state, norm_g[None, :], F32)
    o = jnp.concatenate([o_p, o_s.astype(BF16)], axis=0)
    zero_b = jnp.zeros((1, D), F32)
    xf, xp = mm_res_ln(o, w_out.astype(BF16), zero_b, xf, ln_g[None, :], ln_b[None, :])
    return xf, xp, sf_p, sf_s


def s5_layer(xf, xb, h_re, h_im, lam_re, lam_im, log_dt, b_re, b_im, c_re, c_im, d_skip, w_glu,
             ln_g, ln_b):
    nj, sw, lc = S5_NJ, S5_SW, S5_LC
    ops = _s5_operators(lam_re, lam_im, log_dt, b_re, b_im, c_re, c_im, d_skip)
    nc = SEQ // lc
    up = xb[:NP].reshape(BATCH, nc, lc, nj, LANES).transpose(3, 1, 0, 2, 4).reshape(nj, nc * BATCH, lc * LANES)
    y_p, hf_p = s5_lti(up, jnp.zeros((nj, 2 * BATCH, 2 * sw), F32), ops, nc, lc)
    y_p = y_p.reshape(nj, nc, BATCH, lc, LANES).transpose(2, 1, 3, 0, 4).reshape(NP, D)
    hf_p = hf_p[:, :BATCH].transpose(1, 0, 2)
    us = _sample_rows(xb).reshape(DEC_BATCH, TS, nj, LANES).transpose(2, 0, 1, 3).reshape(nj, DEC_BATCH, TS * LANES)
    h0 = jnp.concatenate([h_re.reshape(DEC_BATCH, nj, sw), h_im.reshape(DEC_BATCH, nj, sw)], -1).transpose(1, 0, 2)
    y_s, hf_s = s5_lti(us, h0, ops, 1, DEC_SEQ)
    y_s = y_s.reshape(nj, DEC_BATCH, TS, LANES).transpose(1, 2, 0, 3).reshape(NS, D)
    hf_s = hf_s.transpose(1, 0, 2)
    y = jnp.concatenate([y_p, y_s], axis=0)
    m = mm_glu(y, w_glu, jnp.zeros((1, 2 * D), F32), BF16)
    xf, xp = res_ln(xf, m, ln_g[None, :], ln_b[None, :])
    st = lambda h, n: (h[..., :sw].reshape(n, SSM_GROUPS, SSM_STATE), h[..., sw:].reshape(n, SSM_GROUPS, SSM_STATE))
    return (xf, xp) + st(hf_p, BATCH) + st(hf_s, DEC_BATCH)


def _rope_tables(pos):
    half = HD // 2
    inv = ROPE_THETA ** (-jnp.arange(half, dtype=F32) / half)
    ang = pos.astype(F32)[:, None] * inv[None, :]
    cos, sin = jnp.cos(ang), jnp.sin(ang)
    cos_t = jnp.tile(jnp.concatenate([cos, cos], -1), (1, LANES // HD))
    sin_t = jnp.tile(jnp.concatenate([-sin, sin], -1), (1, LANES // HD))
    return cos_t, sin_t


def swa_layer(xf, xb, cache_k, cache_v, w_qkv, sinks, w_out, ln_g, ln_b):
    hq = NH * HD
    kvw = NKV * HD
    pos = jnp.concatenate([jnp.tile(jnp.arange(SEQ), BATCH),
                           jnp.tile(PAST_LEN + jnp.arange(TS), DEC_BATCH)])
    cos_t, sin_t = _rope_tables(pos)
    q = mm_rope(xb, w_qkv, 0, hq, cos_t, sin_t, TN_MM // LANES, HD ** -0.5)
    kv = mm_rope(xb, w_qkv, hq, 2 * kvw, cos_t, sin_t, kvw // LANES, 1.0)
    nblk = SEQ // WINDOW
    qw = q.reshape(MT // WINDOW, WINDOW, D)
    kvwin = kv.reshape(MT // WINDOW, WINDOW, 2 * kvw)
    prev = lambda i: (jnp.maximum(i - 1, 0), 0, 0)
    prev_v = lambda i: (jnp.maximum(i - 1, 0), 0, 1)
    o_p = swa_attention(qw, kvwin, kvwin, kvwin, BATCH * nblk, 0, prev, prev_v, sinks, nblk, False).reshape(NP, D)
    kvs = _sample_rows(kv)
    kvs_pad = jnp.pad(kvs, ((0, 0), (0, WINDOW - TS), (0, 0)))
    ck = cache_k.reshape(DEC_BATCH, WINDOW, kvw)
    cv = cache_v.reshape(DEC_BATCH, WINDOW, kvw)
    same = lambda i: (i, 0, 0)
    o_s = swa_attention(q.reshape(MT // TS, TS, D), kvs_pad, ck, cv, DEC_BATCH, NP // TS, same, same,
                        sinks, 1, True).reshape(NS, D)
    o = jnp.concatenate([o_p, o_s], axis=0)
    xf, xp = mm_res_ln(o, w_out.astype(BF16), jnp.zeros((1, D), F32), xf, ln_g[None, :], ln_b[None, :])
    kvp = _prompt_tail(kv, WINDOW)
    nk_p = kvp[..., :kvw].reshape(BATCH, WINDOW, NKV, HD)
    nv_p = kvp[..., kvw:].reshape(BATCH, WINDOW, NKV, HD)
    nk_s = jnp.concatenate([ck[:, DEC_SEQ:], kvs[:, :DEC_SEQ, :kvw]], axis=1).reshape(DEC_BATCH, WINDOW, NKV, HD)
    nv_s = jnp.concatenate([cv[:, DEC_SEQ:], kvs[:, :DEC_SEQ, kvw:]], axis=1).reshape(DEC_BATCH, WINDOW, NKV, HD)
    return xf, xp, nk_p, nv_p, nk_s, nv_s


def kernel(x_prompt, x_sample, state_conv, state_hgrn, state_ssm_re, state_ssm_im, cache_swa_k, cache_swa_v, conv_w_in, conv_b_in, conv_w_dw, conv_b_dw, conv_ln_g, conv_ln_b, conv_w_out, conv_b_out, hgrn_w_in, hgrn_lb_logits, hgrn_norm_g, hgrn_w_out, ssm_lam_re, ssm_lam_im, ssm_log_dt, ssm_b_re, ssm_b_im, ssm_c_re, ssm_c_im, ssm_d, ssm_w_glu, swa_w_qkv, swa_sinks, swa_w_out, ln1_g, ln1_b, ln2_g, ln2_b, moe_w_group, moe_b_group, moe_w_egate, moe_b_egate, moe_w1, moe_w3, moe_w2):
    lbs = jnp.cumsum(jax.nn.softmax(hgrn_lb_logits.astype(F32), axis=0), axis=0)
    lbs = lbs - lbs[0]
    xs_pad = jnp.pad(x_sample, ((0, 0), (0, TS - DEC_SEQ), (0, 0)))
    xf = jnp.concatenate([x_prompt.reshape(NP, D), xs_pad.reshape(NS, D)], axis=0)
    xb = xf.astype(BF16)
    moe = (moe_w_group, moe_b_group, moe_w_egate, moe_b_egate, moe_w1, moe_w3, moe_w2, ln2_g, ln2_b)

    xf, xp, conv_p, conv_s = conv_layer(
        xf, xb, state_conv[0], conv_w_in[0], conv_b_in[0], conv_w_dw[0], conv_b_dw[0],
        conv_ln_g[0], conv_ln_b[0], conv_w_out[0], conv_b_out[0], ln1_g[0], ln1_b[0])
    xf, xb = hier_moe_block(xf, xp, 0, *moe)

    xf, xp, hg_p, hg_s = hgrn_layer(xf, xb, state_hgrn[0], hgrn_w_in[0], lbs[1], hgrn_norm_g[0],
                                    hgrn_w_out[0], ln1_g[1], ln1_b[1])
    xf, xb = hier_moe_block(xf, xp, 1, *moe)

    xf, xp, hr_p, hi_p, hr_s, hi_s = s5_layer(
        xf, xb, state_ssm_re[0], state_ssm_im[0], ssm_lam_re[0], ssm_lam_im[0], ssm_log_dt[0],
        ssm_b_re[0], ssm_b_im[0], ssm_c_re[0], ssm_c_im[0], ssm_d[0], ssm_w_glu[0], ln1_g[2], ln1_b[2])
    xf, xb = hier_moe_block(xf, xp, 2, *moe)

    xf, xp, k_p, v_p, k_s, v_s = swa_layer(xf, xb, cache_swa_k[0], cache_swa_v[0], swa_w_qkv[0],
                                            swa_sinks[0], swa_w_out[0], ln1_g[3], ln1_b[3])
    xf, xb = hier_moe_block(xf, xp, 3, *moe)

    y_p = xf[:NP].reshape(BATCH, SEQ, D)
    y_s = _sample_rows(xf)[:, :DEC_SEQ]
    return (y_p, y_s, conv_p[None], conv_s[None], hg_p[None], hg_s[None],
            hr_p[None], hi_p[None], hr_s[None], hi_s[None],
            k_p[None], v_p[None], k_s[None], v_s[None])
```

```python
import functools

import jax
import jax.numpy as jnp
from jax import lax
from jax.experimental import pallas as pl
from jax.experimental.pallas import tpu as pltpu

F32 = jnp.float32
BF16 = jnp.bfloat16

D = 2048
BATCH = 4
SEQ = 2048
DEPTH = 4
DEC_BATCH = 32
DEC_SEQ = 4
PAST_LEN = 16384
CONV_WIDTH = 31
HG_H = 16
HG_DK = 128
HG_DV = 128
SSM_GROUP = 16
SSM_GROUPS = 128
SSM_STATE = 64
HD = 64
NH = 32
NKV = 4
WINDOW = 128
ROPE_THETA = 10000.0
MOE_E = 16
MOE_FF = 512
ALPHA = (2.0 * DEPTH) ** 0.25
LN_EPS = 1e-5
RMS_EPS = 1e-6

TS = 8
NP = BATCH * SEQ
NS = DEC_BATCH * TS
MT = NP + NS
LANES = 128
SUBLANES = 8
VMEM_LIMIT = 56 * 1024 * 1024

TM_MM = 768
TN_MM = 512
TM_LN = 384
TM_E = 512
P_SLOTS = ((2 * MT + MOE_E * (TM_E - 1) + TM_E - 1) // TM_E) * TM_E
N_ETILES = P_SLOTS // TM_E
GATHER_CH = 64
SMEM_UNROLL = 8


def _cp(*sem):
    return pltpu.CompilerParams(dimension_semantics=sem, vmem_limit_bytes=VMEM_LIMIT)


def _sigmoid(x):
    return 1.0 / (1.0 + jnp.exp(-x))


def _silu(x):
    return x * _sigmoid(x)


def _ln_rows(z, g, b):
    mu = jnp.mean(z, -1, keepdims=True)
    zc = z - mu
    var = jnp.mean(zc * zc, -1, keepdims=True)
    return zc * lax.rsqrt(var + LN_EPS) * g + b


def _cast_weights_once(pairs):
    @pl.when(pl.program_id(1) == 0)
    def _():
        for src, dst in pairs:
            dst[...] = src[...].astype(BF16)


def _mm_glu_kernel(x_ref, wa_ref, wg_ref, ba_ref, bg_ref, o_ref, wab, wgb):
    _cast_weights_once(((wa_ref, wab), (wg_ref, wgb)))
    x = x_ref[...]
    a = jnp.dot(x, wab[...], preferred_element_type=F32) + ba_ref[...]
    g = jnp.dot(x, wgb[...], preferred_element_type=F32) + bg_ref[...]
    o_ref[...] = (a * _sigmoid(g)).astype(o_ref.dtype)


def mm_glu(x, w, b, out_dtype):
    m, k = x.shape
    n = w.shape[1] // 2
    nj = n // TN_MM
    return pl.pallas_call(
        _mm_glu_kernel,
        grid=(nj, m // TM_MM),
        in_specs=[
            pl.BlockSpec((TM_MM, k), lambda j, i: (i, 0)),
            pl.BlockSpec((k, TN_MM), lambda j, i: (0, j)),
            pl.BlockSpec((k, TN_MM), lambda j, i: (0, j + nj)),
            pl.BlockSpec((1, TN_MM), lambda j, i: (0, j)),
            pl.BlockSpec((1, TN_MM), lambda j, i: (0, j + nj)),
        ],
        out_specs=pl.BlockSpec((TM_MM, TN_MM), lambda j, i: (i, j)),
        out_shape=jax.ShapeDtypeStruct((m, n), out_dtype),
        scratch_shapes=[pltpu.VMEM((k, TN_MM), BF16), pltpu.VMEM((k, TN_MM), BF16)],
        compiler_params=_cp("parallel", "arbitrary"),
        name="mm_glu",
    )(x, w, w, b, b)


def _mm_hgrn_kernel(x_ref, w_ref, lb_ref, o_ref, wb, *, nsec):
    j = pl.program_id(0)
    _cast_weights_once(((w_ref, wb),))
    acc = jnp.dot(x_ref[...], wb[...], preferred_element_type=F32)

    @pl.when(j < nsec)
    def _():
        o_ref[...] = _silu(acc)

    @pl.when((j >= nsec) & (j < 2 * nsec))
    def _():
        lb = lb_ref[...]
        o_ref[...] = jnp.log(lb + (1.0 - lb) * _sigmoid(acc))

    @pl.when((j >= 2 * nsec) & (j < 3 * nsec))
    def _():
        o_ref[...] = acc

    @pl.when(j >= 3 * nsec)
    def _():
        o_ref[...] = _silu(acc)


def mm_hgrn(x, w, lb_row):
    m, k = x.shape
    n = w.shape[1]
    tn = 2 * TN_MM
    nsec = (n // 4) // tn
    return pl.pallas_call(
        functools.partial(_mm_hgrn_kernel, nsec=nsec),
        grid=(n // tn, m // TM_MM),
        in_specs=[
            pl.BlockSpec((TM_MM, k), lambda j, i: (i, 0)),
            pl.BlockSpec((k, tn), lambda j, i: (0, j)),
            pl.BlockSpec((1, tn), lambda j, i: (0, j)),
        ],
        out_specs=pl.BlockSpec((TM_MM, tn), lambda j, i: (i, j)),
        out_shape=jax.ShapeDtypeStruct((m, n), F32),
        scratch_shapes=[pltpu.VMEM((k, tn), BF16)],
        compiler_params=_cp("parallel", "arbitrary"),
        name="mm_hgrn",
    )(x, w, lb_row)


def _swap_half_heads(x, lane):
    return jnp.where((lane % HD) < HD // 2, pltpu.roll(x, LANES - HD // 2, 1),
                     pltpu.roll(x, HD // 2, 1))


def _mm_rope_kernel(x_ref, w_ref, cos_ref, sin_ref, o_ref, wb, *, n_rope, scale):
    _cast_weights_once(((w_ref, wb),))
    acc = jnp.dot(x_ref[...], wb[...], preferred_element_type=F32)
    cos = cos_ref[...]
    sin = sin_ref[...]
    lane = lax.broadcasted_iota(jnp.int32, cos.shape, 1)
    for c in range(acc.shape[1] // LANES):
        xc = acc[:, c * LANES:(c + 1) * LANES]
        if c < n_rope:
            xc = (xc * cos + _swap_half_heads(xc, lane) * sin) * scale
        o_ref[:, c * LANES:(c + 1) * LANES] = xc


def mm_rope(x, w, col0, n, cos, sin, n_rope, scale):
    m, k = x.shape
    j0 = col0 // TN_MM
    return pl.pallas_call(
        functools.partial(_mm_rope_kernel, n_rope=n_rope, scale=scale),
        grid=(n // TN_MM, m // TM_MM),
        in_specs=[
            pl.BlockSpec((TM_MM, k), lambda j, i: (i, 0)),
            pl.BlockSpec((k, TN_MM), lambda j, i: (0, j + j0)),
            pl.BlockSpec((TM_MM, LANES), lambda j, i: (i, 0)),
            pl.BlockSpec((TM_MM, LANES), lambda j, i: (i, 0)),
        ],
        out_specs=pl.BlockSpec((TM_MM, TN_MM), lambda j, i: (i, j)),
        out_shape=jax.ShapeDtypeStruct((m, n), F32),
        scratch_shapes=[pltpu.VMEM((k, TN_MM), BF16)],
        compiler_params=_cp("parallel", "arbitrary"),
        name="mm_rope",
    )(x, w, cos, sin)


def _mm_res_ln_kernel(x_ref, w_ref, b_ref, res_ref, g_ref, beta_ref, of_ref, ob_ref):
    y = jnp.dot(x_ref[...], w_ref[...], preferred_element_type=F32) + b_ref[...]
    out = _ln_rows(ALPHA * res_ref[...] + y, g_ref[...], beta_ref[...])
    of_ref[...] = out
    ob_ref[...] = out.astype(BF16)


def mm_res_ln(x, w, b, res, g, beta):
    m, k = x.shape
    row = lambda i: (i, 0)
    fix = lambda i: (0, 0)
    return pl.pallas_call(
        _mm_res_ln_kernel,
        grid=(m // TM_LN,),
        in_specs=[
            pl.BlockSpec((TM_LN, k), row),
            pl.BlockSpec((k, D), fix),
            pl.BlockSpec((1, D), fix),
            pl.BlockSpec((TM_LN, D), row),
            pl.BlockSpec((1, D), fix),
            pl.BlockSpec((1, D), fix),
        ],
        out_specs=[pl.BlockSpec((TM_LN, D), row), pl.BlockSpec((TM_LN, D), row)],
        out_shape=[jax.ShapeDtypeStruct((m, D), F32), jax.ShapeDtypeStruct((m, D), BF16)],
        compiler_params=_cp("parallel"),
        name="mm_res_ln",
    )(x, w, b, res, g, beta)


def _res_ln_kernel(res_ref, y_ref, g_ref, beta_ref, of_ref, ob_ref):
    out = _ln_rows(ALPHA * res_ref[...] + y_ref[...].astype(F32), g_ref[...], beta_ref[...])
    of_ref[...] = out
    ob_ref[...] = out.astype(BF16)


def res_ln(res, y, g, beta):
    m = res.shape[0]
    row = lambda i: (i, 0)
    fix = lambda i: (0, 0)
    return pl.pallas_call(
        _res_ln_kernel,
        grid=(m // TM_LN,),
        in_specs=[pl.BlockSpec((TM_LN, D), row), pl.BlockSpec((TM_LN, D), row),
                  pl.BlockSpec((1, D), fix), pl.BlockSpec((1, D), fix)],
        out_specs=[pl.BlockSpec((TM_LN, D), row), pl.BlockSpec((TM_LN, D), row)],
        out_shape=[jax.ShapeDtypeStruct((m, D), F32), jax.ShapeDtypeStruct((m, D), BF16)],
        compiler_params=_cp("parallel"),
        name="res_ln",
    )(res, y, g, beta)


HALO = 32
CONV_RC = 64
CONV_LC = 256


def _conv_kernel(u_ref, buf_ref, wdw_ref, bdw_ref, g_ref, b_ref, z_ref, ext_ref, sh_ref, c_ref, *, tt):
    i = pl.program_id(1)

    @pl.when(i == 0)
    def _():
        ext_ref[0:HALO, :] = buf_ref[0]

    @pl.when(i > 0)
    def _():
        ext_ref[0:HALO, :] = ext_ref[tt:tt + HALO, :]

    ext_ref[HALO:HALO + tt, :] = u_ref[...]
    nsh = tt + HALO - SUBLANES
    for b in range(1, SUBLANES):
        sh_ref[b - 1] = ext_ref[b:b + nsh, :]
    pad = HALO - (CONV_WIDTH - 1)
    rc = min(CONV_RC, tt)
    for r0 in range(0, tt, rc):
        for c0 in range(0, D, CONV_LC):
            acc = jnp.zeros((rc, CONV_LC), F32) + bdw_ref[:, c0:c0 + CONV_LC]
            for w in range(CONV_WIDTH):
                a, b = divmod(pad + w, SUBLANES)
                rows = slice(r0 + a * SUBLANES, r0 + a * SUBLANES + rc)
                if b == 0:
                    tap = ext_ref[rows, c0:c0 + CONV_LC]
                else:
                    tap = sh_ref[b - 1, rows, c0:c0 + CONV_LC]
                acc = acc + tap * wdw_ref[w:w + 1, c0:c0 + CONV_LC]
            c_ref[r0:r0 + rc, c0:c0 + CONV_LC] = acc
    z_ref[...] = _silu(_ln_rows(c_ref[...], g_ref[...], b_ref[...])).astype(z_ref.dtype)


def conv_ln_silu(u, row0_blocks, nseq, t, tt, buf, wdw, bdw, g, b, out_dtype):
    nt = t // tt
    fix = lambda s, i: (0, 0)
    return pl.pallas_call(
        functools.partial(_conv_kernel, tt=tt),
        grid=(nseq, nt),
        in_specs=[
            pl.BlockSpec((tt, D), lambda s, i: (row0_blocks + s * nt + i, 0)),
            pl.BlockSpec((1, HALO, D), lambda s, i: (s, 0, 0)),
            pl.BlockSpec((CONV_WIDTH, D), fix),
            pl.BlockSpec((1, D), fix),
            pl.BlockSpec((1, D), fix),
            pl.BlockSpec((1, D), fix),
        ],
        out_specs=pl.BlockSpec((tt, D), lambda s, i: (s * nt + i, 0)),
        out_shape=jax.ShapeDtypeStruct((nseq * t, D), out_dtype),
        scratch_shapes=[pltpu.VMEM((HALO + tt, D), F32),
                        pltpu.VMEM((SUBLANES - 1, tt + HALO - SUBLANES, D), F32),
                        pltpu.VMEM((tt, D), F32)],
        compiler_params=_cp("parallel", "arbitrary"),
        name="conv_ln_silu",
    )(u, buf, wdw, bdw, g, b)


def _hgrn_kernel(p_ref, s0_ref, ng_ref, o_ref, sf_ref, st_ref, pst_ref, ob_ref, bl_ref, *, tt, sc, nvalid):
    i = pl.program_id(1)
    hk = HG_H * HG_DK

    @pl.when(i == 0)
    def _():
        for h in range(HG_H):
            st_ref[h * HG_DV:(h + 1) * HG_DV, :] = s0_ref[0, h].T

    ones = jnp.ones((LANES, LANES), BF16)
    row = lax.broadcasted_iota(jnp.int32, (sc, hk), 0)
    first = [(s // SUBLANES) * SUBLANES for s in range(nvalid)]
    offs = [sum(sc - f for f in first[:s]) for s in range(nvalid + 1)]

    if nvalid == sc:
        rr = lax.broadcasted_iota(jnp.int32, (tt, tt), 0)
        cc = lax.broadcasted_iota(jnp.int32, (tt, tt), 1)
        tril = ((cc <= rr) & (cc // sc == rr // sc)).astype(F32)
        for h in range(HG_H):
            sl = slice(h * HG_DK, (h + 1) * HG_DK)
            bl_ref[:, sl] = jnp.dot(tril, p_ref[:, hk + h * HG_DK:hk + (h + 1) * HG_DK],
                                    preferred_element_type=F32, precision=lax.Precision.HIGHEST)

    def sub_chunk(ci, carry):
        r0 = pl.multiple_of(ci * sc, sc)
        q = p_ref[pl.ds(r0, sc), 0:hk]
        lf = p_ref[pl.ds(r0, sc), hk:2 * hk]
        v = p_ref[pl.ds(r0, sc), 2 * hk:3 * hk]
        if nvalid < sc:
            lf = jnp.where(row < nvalid, lf, 0.0)
            bl = jnp.zeros((sc, hk), F32)
            for s in range(nvalid):
                bl = bl + jnp.where(row >= s, lf[s:s + 1, :], 0.0)
        else:
            bl = bl_ref[pl.ds(r0, sc), :]
        kk = 1.0 - jnp.exp(lf)
        btot = bl[sc - 1:sc, :]
        qd = q * jnp.exp(bl)
        kd = kk * jnp.exp(btot - bl)
        for s in range(nvalid):
            f = first[s]
            rowf = lax.broadcasted_iota(jnp.int32, (sc - f, hk), 0) + f
            e = jnp.exp(jnp.where(rowf >= s, bl[f:] - bl[s:s + 1, :], -jnp.inf))
            pst_ref[offs[s]:offs[s + 1], :] = q[f:] * e * kk[s:s + 1, :]
        o_acc = []
        for h in range(HG_H):
            sl = slice(h * HG_DK, (h + 1) * HG_DK)
            rs = jnp.dot(pst_ref[0:offs[nvalid], sl].astype(BF16), ones,
                         preferred_element_type=F32)
            vh = v[:, sl]
            st_h = st_ref[h * HG_DV:(h + 1) * HG_DV, :]
            oh = lax.dot_general(qd[:, sl].astype(BF16), st_h.astype(BF16),
                                 (((1,), (1,)), ((), ())), preferred_element_type=F32)
            tiles = [oh[f:f + SUBLANES] for f in range(0, sc, SUBLANES)]
            for s in range(nvalid):
                for ti, f in enumerate(range(first[s], sc, SUBLANES)):
                    piece = rs[offs[s] + ti * SUBLANES:offs[s] + (ti + 1) * SUBLANES, :]
                    tiles[f // SUBLANES] = tiles[f // SUBLANES] + piece * vh[s:s + 1, :]
            o_acc.append(jnp.concatenate(tiles, axis=0) if len(tiles) > 1 else tiles[0])
            upd = jnp.dot(vh.T.astype(BF16), kd[:, sl].astype(BF16), preferred_element_type=F32)
            st_ref[h * HG_DV:(h + 1) * HG_DV, :] = jnp.exp(btot[:, sl]) * st_h + upd
        ob_ref[pl.ds(r0, sc), :] = jnp.concatenate(o_acc, axis=1)
        return carry

    lax.fori_loop(0, tt // sc, sub_chunk, 0)

    o = ob_ref[...]
    o2 = o * o
    parts = []
    for h in range(HG_H):
        sl = slice(h * HG_DV, (h + 1) * HG_DV)
        ms = jnp.dot(o2[:, sl].astype(BF16), ones, preferred_element_type=F32) * (1.0 / HG_DV)
        parts.append(o[:, sl] * lax.rsqrt(ms + RMS_EPS))
    on = jnp.concatenate(parts, axis=1) * ng_ref[...] * p_ref[:, 3 * hk:4 * hk]
    o_ref[...] = on.astype(o_ref.dtype)

    @pl.when(i == pl.num_programs(1) - 1)
    def _():
        for h in range(HG_H):
            sf_ref[0, h] = st_ref[h * HG_DV:(h + 1) * HG_DV, :].T


def hgrn_scan(p, row0_blocks, nseq, t, tt, sc, nvalid, s0, norm_g, out_dtype):
    nt = t // tt
    return pl.pallas_call(
        functools.partial(_hgrn_kernel, tt=tt, sc=sc, nvalid=nvalid),
        grid=(nseq, nt),
        in_specs=[
            pl.BlockSpec((tt, 4 * D), lambda s, i: (row0_blocks + s * nt + i, 0)),
            pl.BlockSpec((1, HG_H, HG_DK, HG_DV), lambda s, i: (s, 0, 0, 0)),
            pl.BlockSpec((1, D), lambda s, i: (0, 0)),
        ],
        out_specs=[
            pl.BlockSpec((tt, D), lambda s, i: (s * nt + i, 0)),
            pl.BlockSpec((1, HG_H, HG_DK, HG_DV), lambda s, i: (s, 0, 0, 0)),
        ],
        out_shape=[jax.ShapeDtypeStruct((nseq * t, D), out_dtype),
                   jax.ShapeDtypeStruct((nseq, HG_H, HG_DK, HG_DV), F32)],
        scratch_shapes=[pltpu.VMEM((HG_H * HG_DV, HG_DK), F32),
                        pltpu.VMEM((sc * sc, D), F32),
                        pltpu.VMEM((tt, D), F32),
                        pltpu.VMEM((tt, D), F32)],
        compiler_params=_cp("parallel", "arbitrary"),
        name="hgrn_scan",
    )(p, s0, norm_g)


S5_GPB = LANES // SSM_GROUP
S5_NJ = D // LANES
S5_SW = S5_GPB * SSM_STATE
S5_LC = 8


def _spread_groups(blk, rep, mask):
    full = jnp.dot(blk.astype(BF16), rep, preferred_element_type=F32)
    return jnp.where(mask, full, 0.0).astype(BF16)


def _group_mask(rows, row_w, cols, col_w):
    r = lax.broadcasted_iota(jnp.int32, (rows, cols), 0) // row_w
    c = lax.broadcasted_iota(jnp.int32, (rows, cols), 1) // col_w
    return r == c


def _rep_matrix(w, cols):
    r = lax.broadcasted_iota(jnp.int32, (w, cols), 0)
    c = lax.broadcasted_iota(jnp.int32, (w, cols), 1) % w
    return (r == c).astype(BF16)


def _s5_kernel(u_ref, h0_ref, klt_ref, wtr_ref, wti_ref, car_ref, cai_ref, ar_ref, ai_ref, y_ref, hf_ref,
               t_s, bp_s, cp_s, con_ref, hp_ref, *, nc, nv):
    sw = S5_SW
    lc = S5_LC
    c, p = SSM_GROUP, SSM_STATE
    rep_c = _rep_matrix(c, LANES)
    rep_p = _rep_matrix(p, sw)
    t_s[...] = jnp.zeros(t_s.shape, BF16)
    mask_t = _group_mask(LANES, c, LANES, c)
    for l in range(lc):
        blk = _spread_groups(klt_ref[0, l], rep_c, mask_t)
        for s in range(lc - l):
            t_s[s * LANES:(s + 1) * LANES, (s + l) * LANES:(s + l + 1) * LANES] = blk
    mask_b = _group_mask(LANES, c, sw, p)
    for s in range(lc):
        if s < nv:
            bp_s[s * LANES:(s + 1) * LANES, 0:sw] = _spread_groups(wtr_ref[0, nv - 1 - s], rep_p, mask_b)
            bp_s[s * LANES:(s + 1) * LANES, sw:2 * sw] = _spread_groups(wti_ref[0, nv - 1 - s], rep_p, mask_b)
        else:
            bp_s[s * LANES:(s + 1) * LANES, :] = jnp.zeros((LANES, 2 * sw), BF16)
    mask_c = _group_mask(sw, p, LANES, c)
    for t in range(lc):
        cp_s[0:sw, t * LANES:(t + 1) * LANES] = _spread_groups(car_ref[0, t], rep_c, mask_c)
        cp_s[sw:2 * sw, t * LANES:(t + 1) * LANES] = _spread_groups(cai_ref[0, t], rep_c, mask_c)

    u = u_ref[0]
    con_ref[...] = jnp.dot(u, bp_s[...], preferred_element_type=F32)
    ar = ar_ref[0]
    ai = ai_ref[0]

    def step(hr, hi, cr, ci):
        return ar * hr - ai * hi + cr, ar * hi + ai * hr + ci

    if nc == 1:
        hr, hi = h0_ref[0, :, 0:sw], h0_ref[0, :, sw:2 * sw]
        hp_ref[:, 0:sw] = hr
        hp_ref[:, sw:2 * sw] = hi
        hr, hi = step(hr, hi, con_ref[:, 0:sw], con_ref[:, sw:2 * sw])
    else:
        half = 4
        top = lax.broadcasted_iota(jnp.int32, (2 * half, sw), 0) < half

        def pair(k, carry):
            hr, hi = carry
            r0 = pl.multiple_of(k * 2 * half, 2 * half)
            cr = con_ref[pl.ds(r0, 2 * half), 0:sw]
            ci = con_ref[pl.ds(r0, 2 * half), sw:2 * sw]
            ar1, ai1 = step(hr, hi, cr, ci)
            ar1 = jnp.where(top, ar1, pltpu.roll(ar1, half, 0))
            ai1 = jnp.where(top, ai1, pltpu.roll(ai1, half, 0))
            hp_ref[pl.ds(r0, 2 * half), 0:sw] = jnp.where(top, hr, ar1)
            hp_ref[pl.ds(r0, 2 * half), sw:2 * sw] = jnp.where(top, hi, ai1)
            br, bi = step(ar1, ai1, cr, ci)
            br = jnp.where(top, pltpu.roll(br, half, 0), br)
            bi = jnp.where(top, pltpu.roll(bi, half, 0), bi)
            return br, bi

        hr, hi = lax.fori_loop(0, nc // 2, pair, (h0_ref[0, :, 0:sw], h0_ref[0, :, sw:2 * sw]))
    hf_ref[0, :, 0:sw] = hr
    hf_ref[0, :, sw:2 * sw] = hi
    y = (jnp.dot(u, t_s[...], preferred_element_type=F32)
         + jnp.dot(hp_ref[...].astype(BF16), cp_s[...], preferred_element_type=F32))
    y_ref[0] = jax.nn.gelu(y).astype(y_ref.dtype)


def s5_lti(u, h0, ops, nc, nv):
    klt, wtr, wti, car, cai, pw_re, pw_im = ops
    nj, rows, lw = u.shape
    hrows = h0.shape[1]
    sw, lc, c, p = S5_SW, S5_LC, SSM_GROUP, SSM_STATE
    ar = pw_re[:, nv].reshape(nj, 1, sw)
    ai = pw_im[:, nv].reshape(nj, 1, sw)
    blk = lambda j: (j, 0, 0)
    blk4 = lambda j: (j, 0, 0, 0)
    return pl.pallas_call(
        functools.partial(_s5_kernel, nc=nc, nv=nv),
        grid=(nj,),
        in_specs=[
            pl.BlockSpec((1, rows, lw), blk),
            pl.BlockSpec((1, hrows, 2 * sw), blk),
            pl.BlockSpec((1, lc, LANES, c), blk4),
            pl.BlockSpec((1, lc, LANES, p), blk4),
            pl.BlockSpec((1, lc, LANES, p), blk4),
            pl.BlockSpec((1, lc, sw, c), blk4),
            pl.BlockSpec((1, lc, sw, c), blk4),
            pl.BlockSpec((1, 1, sw), blk),
            pl.BlockSpec((1, 1, sw), blk),
        ],
        out_specs=[pl.BlockSpec((1, rows, lw), blk), pl.BlockSpec((1, hrows, 2 * sw), blk)],
        out_shape=[jax.ShapeDtypeStruct((nj, rows, lw), BF16),
                   jax.ShapeDtypeStruct((nj, hrows, 2 * sw), F32)],
        scratch_shapes=[pltpu.VMEM((lw, lw), BF16), pltpu.VMEM((lw, 2 * sw), BF16),
                        pltpu.VMEM((2 * sw, lw), BF16),
                        pltpu.VMEM((rows, 2 * sw), F32), pltpu.VMEM((rows, 2 * sw), F32)],
        compiler_params=_cp("parallel"),
        name="s5_lti",
    )(u, h0, klt, wtr, wti, car, cai, ar, ai)


def _s5_operators(lam_re, lam_im, log_dt, b_re, b_im, c_re, c_im, d_skip):
    hp = lax.Precision.HIGHEST
    gpb, nj, c, p, lc = S5_GPB, S5_NJ, SSM_GROUP, SSM_STATE, S5_LC
    dt = jnp.exp(log_dt)[:, None]
    lr, li = lam_re, lam_im
    mag = jnp.exp(lr * dt)
    ab_re, ab_im = mag * jnp.cos(li * dt), mag * jnp.sin(li * dt)
    nr, ni = ab_re - 1.0, ab_im
    den = lr * lr + li * li
    z_re, z_im = (nr * lr + ni * li) / den, (ni * lr - nr * li) / den
    bbt_re = (z_re[..., None] * b_re - z_im[..., None] * b_im).transpose(0, 2, 1)
    bbt_im = (z_re[..., None] * b_im + z_im[..., None] * b_re).transpose(0, 2, 1)
    l = jnp.arange(lc + 1, dtype=F32)[None, :, None]
    pmag = jnp.exp(l * (lr * dt)[:, None, :])
    pw_re = pmag * jnp.cos(l * (li * dt)[:, None, :])
    pw_im = pmag * jnp.sin(l * (li * dt)[:, None, :])
    wt_re = pw_re[:, :lc, None, :] * bbt_re[:, None] - pw_im[:, :lc, None, :] * bbt_im[:, None]
    wt_im = pw_re[:, :lc, None, :] * bbt_im[:, None] + pw_im[:, :lc, None, :] * bbt_re[:, None]
    klt = (jnp.einsum('glcp,gop->glco', wt_re, c_re, precision=hp)
           - jnp.einsum('glcp,gop->glco', wt_im, c_im, precision=hp))
    klt = klt.at[:, 0].add(d_skip.reshape(SSM_GROUPS, c)[:, :, None] * jnp.eye(c, dtype=F32)[None])
    ct_re, ct_im = c_re.transpose(0, 2, 1)[:, None], c_im.transpose(0, 2, 1)[:, None]
    ca_re = ct_re * pw_re[:, 1:, :, None] - ct_im * pw_im[:, 1:, :, None]
    ca_im = ct_re * pw_im[:, 1:, :, None] + ct_im * pw_re[:, 1:, :, None]

    def by_block(a):
        _, ll, r, w = a.shape
        return a.reshape(nj, gpb, ll, r, w).transpose(0, 2, 1, 3, 4).reshape(nj, ll, gpb * r, w)

    def pw_block(a):
        return a.reshape(nj, gpb, lc + 1, p).transpose(0, 2, 1, 3).reshape(nj, lc + 1, gpb * p)

    return (by_block(klt), by_block(wt_re), by_block(wt_im), by_block(ca_re), by_block(-ca_im),
            pw_block(pw_re), pw_block(pw_im))


def _attn_kernel(q_ref, kv_ref, pk_ref, pv_ref, sink_ref, o_ref, *, tq, nblk, always_prev):
    idx = pl.program_id(0)
    kvw = NKV * HD
    group = NH // NKV
    pkn = pk_ref.shape[1]
    to = kv_ref.shape[1]
    nk = pkn + to
    rows = group * tq
    a = lax.broadcasted_iota(jnp.int32, (rows, nk), 0) % tq
    col = lax.broadcasted_iota(jnp.int32, (rows, nk), 1)
    prev_ok = col >= a
    if not always_prev:
        prev_ok = prev_ok & ((idx % nblk) > 0)
    mask = ((col < pkn) & prev_ok) | ((col >= pkn) & ((col - pkn) <= a))
    hrow = lax.broadcasted_iota(jnp.int32, (rows, 1), 0) // tq
    low = lax.broadcasted_iota(jnp.int32, (tq, LANES), 1) < HD
    nt = (((1,), (1,)), ((), ()))
    for kb in range(NKV // 2):
        lanes = slice(kb * LANES, (kb + 1) * LANES)
        kcat = jnp.concatenate([pk_ref[0, :, lanes], kv_ref[0, :, lanes]], axis=0).astype(BF16)
        vcat = jnp.concatenate([pv_ref[0, :, lanes],
                                kv_ref[0, :, kvw + kb * LANES:kvw + (kb + 1) * LANES]], axis=0).astype(BF16)
        for hi in range(2):
            j = 2 * kb + hi
            parts = []
            sk = jnp.zeros((rows, 1), F32)
            for h8 in range(group):
                qb, e = (group * j + h8) // 2, h8 % 2
                q2 = q_ref[0, :, qb * LANES:(qb + 1) * LANES]
                qe = jnp.where(low if e == 0 else jnp.logical_not(low), q2, 0.0)
                if e != hi:
                    qe = pltpu.roll(qe, HD, 1)
                parts.append(qe)
                sk = jnp.where(hrow == h8, sink_ref[group * j + h8], sk)
            q8 = jnp.concatenate(parts, axis=0).astype(BF16)
            s = lax.dot_general(q8, kcat, nt, preferred_element_type=F32)
            s = jnp.where(mask, s, -jnp.inf)
            m = jnp.maximum(jnp.max(s, -1, keepdims=True), sk)
            p = jnp.exp(s - m)
            inv = 1.0 / (jnp.sum(p, -1, keepdims=True) + jnp.exp(sk - m))
            o8 = jnp.dot((p * inv).astype(BF16), vcat, preferred_element_type=F32)
            for pr in range(group // 2):
                qb = (group * j) // 2 + pr
                pair = []
                for e in range(2):
                    oe = o8[(2 * pr + e) * tq:(2 * pr + e + 1) * tq, :]
                    if e != hi:
                        oe = pltpu.roll(oe, HD, 1)
                    pair.append(oe)
                o_ref[0, :, qb * LANES:(qb + 1) * LANES] = jnp.where(low, pair[0], pair[1]).astype(o_ref.dtype)


def swa_attention(q3, kv3, pk3, pv3, nb, blk0, pk_map, pv_map, sinks, nblk, always_prev):
    tq = q3.shape[1]
    to = kv3.shape[1]
    kvw = NKV * HD
    return pl.pallas_call(
        functools.partial(_attn_kernel, tq=tq, nblk=nblk, always_prev=always_prev),
        grid=(nb,),
        in_specs=[
            pl.BlockSpec((1, tq, D), lambda i: (i + blk0, 0, 0)),
            pl.BlockSpec((1, to, 2 * kvw), lambda i: (i, 0, 0)),
            pl.BlockSpec((1, pk3.shape[1], kvw), pk_map),
            pl.BlockSpec((1, pv3.shape[1], kvw), pv_map),
            pl.BlockSpec(memory_space=pltpu.SMEM),
        ],
        out_specs=pl.BlockSpec((1, tq, D), lambda i: (i, 0, 0)),
        out_shape=jax.ShapeDtypeStruct((nb, tq, D), BF16),
        compiler_params=_cp("parallel"),
        name="swa_attention",
    )(q3, kv3, pk3, pv3, sinks)


R_E1, R_E2, R_W1, R_W2, R_R1, R_R2 = 0, 1, 2, 3, 4, 5


def _router_kernel(x_ref, wh_ref, wl_ref, b_ref, o_ref, cnt_ref, carry_ref):
    i = pl.program_id(0)

    @pl.when(i == 0)
    def _():
        carry_ref[...] = jnp.zeros(carry_ref.shape, F32)

    x = x_ref[...]
    xh = x.astype(BF16)
    xl = (x - xh.astype(F32)).astype(BF16)
    wh = wh_ref[...]
    l = (jnp.dot(xh, wh, preferred_element_type=F32) + jnp.dot(xl, wh, preferred_element_type=F32)
         + jnp.dot(xh, wl_ref[...], preferred_element_type=F32) + b_ref[...])
    tm = l.shape[0]
    lane = lax.broadcasted_iota(jnp.int32, l.shape, 1).astype(F32)
    big = 1e9
    ninf = -jnp.inf
    gmask = lane < 4.0
    gl = jnp.where(gmask, l, ninf)
    gm = jnp.max(gl, -1, keepdims=True)
    gsel = jnp.min(jnp.where(gl == gm, lane, big), -1, keepdims=True)
    gsum = jnp.sum(jnp.where(gmask, jnp.exp(gl - gm), 0.0), -1, keepdims=True)
    gval = 1.0 / gsum
    lo = 4.0 + 4.0 * gsel
    emask = (lane >= lo) & (lane < lo + 4.0)
    el = jnp.where(emask, l, ninf)
    m1 = jnp.max(el, -1, keepdims=True)
    i1 = jnp.min(jnp.where(emask & (el == m1), lane, big), -1, keepdims=True)
    emask2 = emask & (lane != i1)
    el2 = jnp.where(emask2, l, ninf)
    m2 = jnp.max(el2, -1, keepdims=True)
    i2 = jnp.min(jnp.where(emask2 & (el2 == m2), lane, big), -1, keepdims=True)
    r = jnp.exp(m2 - m1)
    w1 = gval / (1.0 + r)
    w2 = gval * r / (1.0 + r)
    e1 = i1 - 4.0
    e2 = i2 - 4.0
    oh = ((lane == e1) | (lane == e2)).astype(F32)
    tri = (lax.broadcasted_iota(jnp.int32, (tm, tm), 1)
           < lax.broadcasted_iota(jnp.int32, (tm, tm), 0)).astype(BF16)
    before = jnp.dot(tri, oh.astype(BF16), preferred_element_type=F32) + carry_ref[...]
    r1 = jnp.sum(jnp.where(lane == e1, before, 0.0), -1, keepdims=True)
    r2 = jnp.sum(jnp.where(lane == e2, before, 0.0), -1, keepdims=True)
    carry_ref[...] = carry_ref[...] + jnp.sum(oh, axis=0, keepdims=True)
    cnt_ref[...] = carry_ref[...]
    rec = jnp.zeros(l.shape, F32)
    for ln, val in ((R_E1, e1), (R_E2, e2), (R_W1, w1), (R_W2, w2), (R_R1, r1), (R_R2, r2)):
        rec = jnp.where(lane == float(ln), val, rec)
    o_ref[...] = rec


def moe_router(x, w, b):
    m = x.shape[0]
    wh = w.astype(BF16)
    wl = (w - wh.astype(F32)).astype(BF16)
    return pl.pallas_call(
        _router_kernel,
        grid=(m // TM_LN,),
        in_specs=[pl.BlockSpec((TM_LN, D), lambda i: (i, 0)),
                  pl.BlockSpec((D, LANES), lambda i: (0, 0)),
                  pl.BlockSpec((D, LANES), lambda i: (0, 0)),
                  pl.BlockSpec((1, LANES), lambda i: (0, 0))],
        out_specs=[pl.BlockSpec((TM_LN, LANES), lambda i: (i, 0)),
                   pl.BlockSpec((1, LANES), lambda i: (0, 0))],
        out_shape=[jax.ShapeDtypeStruct((m, LANES), F32), jax.ShapeDtypeStruct((1, LANES), F32)],
        scratch_shapes=[pltpu.VMEM((1, LANES), F32)],
        compiler_params=_cp("arbitrary"),
        name="moe_router",
    )(x, wh, wl, b)


def _slots_kernel(rec_ref, starts_ref, o_ref):
    rec = rec_ref[...]
    lane = lax.broadcasted_iota(jnp.int32, rec.shape, 1).astype(F32)
    starts = starts_ref[...]
    out = jnp.zeros(rec.shape, F32)
    for k, (le, lr) in enumerate(((R_E1, R_R1), (R_E2, R_R2))):
        e = rec[:, le:le + 1]
        s = jnp.sum(jnp.where(lane == e, starts, 0.0), -1, keepdims=True) + rec[:, lr:lr + 1]
        out = jnp.where(lane == float(k), s, out)
    o_ref[...] = out


def moe_slots(rec, starts_row):
    m = rec.shape[0]
    return pl.pallas_call(
        _slots_kernel,
        grid=(m // TM_MM,),
        in_specs=[pl.BlockSpec((TM_MM, LANES), lambda i: (i, 0)), pl.BlockSpec((1, LANES), lambda i: (0, 0))],
        out_specs=pl.BlockSpec((TM_MM, LANES), lambda i: (i, 0)),
        out_shape=jax.ShapeDtypeStruct((m, LANES), F32),
        compiler_params=_cp("parallel"),
        name="moe_slots",
    )(rec, starts_row)


def _row_gather_start(idx_ref, base, nchunks, src_hbm, dst, sem, split_priorities=False):
    def body(c, carry):
        off = pl.multiple_of(c * GATHER_CH, GATHER_CH)
        dst_c = dst.at[pl.ds(off, GATHER_CH)]
        for u in range(GATHER_CH):
            pltpu.make_async_copy(src_hbm.at[pl.ds(idx_ref[base + off + u], 1)],
                                  dst_c.at[pl.ds(u, 1)], sem).start(
                                      priority=u % 2 if split_priorities else 0)
        return carry
    lax.fori_loop(0, nchunks, body, 0)


def _row_gather_wait(nchunks, src_hbm, dst, sem):
    def body(c, carry):
        pltpu.make_async_copy(src_hbm.at[pl.ds(0, GATHER_CH)], dst.at[pl.ds(0, GATHER_CH)], sem).wait()
        return carry
    lax.fori_loop(0, nchunks, body, 0)


def _ffn_kernel(te_ref, nt_ref, nch_ref, pos_ref, x_hbm, w1_ref, w3_ref, w2_ref, y_ref, xbuf, sem,
                w1b, w3b, w2b, src_ref):
    i = pl.program_id(0)
    nt = nt_ref[0]
    slot = i % 2

    @pl.when(i == 0)
    def _():
        xbuf[...] = jnp.zeros(xbuf.shape, F32)
        def clear(k, carry):
            for u in range(SMEM_UNROLL):
                src_ref[k * SMEM_UNROLL + u] = 0
            return carry
        lax.fori_loop(0, P_SLOTS // SMEM_UNROLL, clear, 0)

        def scatter(k, carry):
            t, rem = k // (TM_LN // SMEM_UNROLL), k % (TM_LN // SMEM_UNROLL)
            for half in range(2):
                for u in range(SMEM_UNROLL):
                    r = rem * SMEM_UNROLL + u
                    src_ref[pos_ref[t * 2 * TM_LN + half * TM_LN + r]] = t * TM_LN + r
            return carry
        lax.fori_loop(0, MT // SMEM_UNROLL, scatter, 0)
        _row_gather_start(src_ref, 0, nch_ref[0], x_hbm, xbuf.at[0], sem.at[0])

    @pl.when(i + 1 < nt)
    def _():
        _row_gather_start(src_ref, (i + 1) * TM_E, nch_ref[i + 1], x_hbm,
                          xbuf.at[1 - slot], sem.at[1 - slot])

    @pl.when(i < nt)
    def _():
        prev = te_ref[jnp.maximum(i - 1, 0)]

        @pl.when((i == 0) | (te_ref[i] != prev))
        def _():
            w1b[...] = w1_ref[0, 0].astype(BF16)
            w3b[...] = w3_ref[0, 0].astype(BF16)
            w2b[...] = w2_ref[0, 0].astype(BF16)

        _row_gather_wait(nch_ref[i], x_hbm, xbuf.at[slot], sem.at[slot])
        x = xbuf[slot].astype(BF16)
        h1 = jnp.dot(x, w1b[...], preferred_element_type=F32)
        h3 = jnp.dot(x, w3b[...], preferred_element_type=F32)
        h = _silu(h1) * h3
        y_ref[...] = jnp.dot(h.astype(BF16), w2b[...], preferred_element_type=F32)

    @pl.when(i >= nt)
    def _():
        y_ref[...] = jnp.zeros(y_ref.shape, y_ref.dtype)


def moe_ffn(layer, tile_expert, n_tiles, n_chunks, pos_tiles, x, w1, w3, w2):
    wmap = lambda i, te, nt, nch, pos: (layer, te[i], 0, 0)
    grid_spec = pltpu.PrefetchScalarGridSpec(
        num_scalar_prefetch=4,
        grid=(N_ETILES,),
        in_specs=[
            pl.BlockSpec(memory_space=pl.ANY),
            pl.BlockSpec((1, 1, D, MOE_FF), wmap),
            pl.BlockSpec((1, 1, D, MOE_FF), wmap),
            pl.BlockSpec((1, 1, MOE_FF, D), wmap),
        ],
        out_specs=pl.BlockSpec((TM_E, D), lambda i, te, nt, nch, pos: (i, 0)),
        scratch_shapes=[pltpu.VMEM((2, TM_E, D), F32), pltpu.SemaphoreType.DMA((2,)),
                        pltpu.VMEM((D, MOE_FF), BF16), pltpu.VMEM((D, MOE_FF), BF16),
                        pltpu.VMEM((MOE_FF, D), BF16), pltpu.SMEM((P_SLOTS,), jnp.int32)],
    )
    return pl.pallas_call(
        _ffn_kernel,
        grid_spec=grid_spec,
        out_shape=jax.ShapeDtypeStruct((P_SLOTS, D), F32),
        compiler_params=_cp("arbitrary"),
        name="moe_ffn",
    )(tile_expert, n_tiles, n_chunks, pos_tiles, x, w1, w3, w2)


def _combine_ln_kernel(pos_ref, res_ref, rec_ref, g_ref, beta_ref, ys_hbm, of_ref, ob_ref, ybuf, sem):
    i = pl.program_id(0)
    slot = i % 2
    n = 2 * TM_LN
    nch = n // GATHER_CH

    @pl.when(i == 0)
    def _():
        _row_gather_start(pos_ref, 0, nch, ys_hbm, ybuf.at[0], sem.at[0], split_priorities=True)

    @pl.when(i + 1 < pl.num_programs(0))
    def _():
        _row_gather_start(pos_ref, (i + 1) * n, nch, ys_hbm, ybuf.at[1 - slot], sem.at[1 - slot],
                          split_priorities=True)

    _row_gather_wait(nch, ys_hbm, ybuf.at[slot], sem.at[slot])
    rec = rec_ref[...]
    y = (rec[:, R_W1:R_W1 + 1] * ybuf[slot, 0:TM_LN, :]
         + rec[:, R_W2:R_W2 + 1] * ybuf[slot, TM_LN:2 * TM_LN, :])
    out = _ln_rows(ALPHA * res_ref[...] + y, g_ref[...], beta_ref[...])
    of_ref[...] = out
    ob_ref[...] = out.astype(BF16)


def combine_ln(pos_tiles, res, rec, g, beta, ys):
    m = res.shape[0]
    row = lambda i, pos: (i, 0)
    fix = lambda i, pos: (0, 0)
    grid_spec = pltpu.PrefetchScalarGridSpec(
        num_scalar_prefetch=1,
        grid=(m // TM_LN,),
        in_specs=[pl.BlockSpec((TM_LN, D), row), pl.BlockSpec((TM_LN, LANES), row),
                  pl.BlockSpec((1, D), fix), pl.BlockSpec((1, D), fix),
                  pl.BlockSpec(memory_space=pl.ANY)],
        out_specs=[pl.BlockSpec((TM_LN, D), row), pl.BlockSpec((TM_LN, D), row)],
        scratch_shapes=[pltpu.VMEM((2, 2 * TM_LN, D), F32), pltpu.SemaphoreType.DMA((2,))],
    )
    return pl.pallas_call(
        _combine_ln_kernel,
        grid_spec=grid_spec,
        out_shape=[jax.ShapeDtypeStruct((m, D), F32), jax.ShapeDtypeStruct((m, D), BF16)],
        compiler_params=_cp("arbitrary"),
        name="combine_ln",
    )(pos_tiles, res, rec, g, beta, ys)


def _dispatch_plan(rec, counts_row):
    counts = counts_row[0, :MOE_E].astype(jnp.int32)
    padded = ((counts + TM_E - 1) // TM_E) * TM_E
    ends = jnp.cumsum(padded)
    starts = ends - padded
    starts_row = jnp.pad(starts.astype(F32), (0, LANES - MOE_E))[None, :]
    pos = moe_slots(rec, starts_row)[:, 0:2].astype(jnp.int32)
    n_tiles = (ends[-1] // TM_E).astype(jnp.int32)
    tile_start = jnp.arange(N_ETILES, dtype=jnp.int32) * TM_E
    te = jnp.sum((tile_start[:, None] >= ends[None, :]).astype(jnp.int32), axis=1)
    last = jnp.sum(jnp.where(jnp.arange(N_ETILES) == n_tiles - 1, te, 0))
    live = jnp.arange(N_ETILES) < n_tiles
    te = jnp.minimum(jnp.where(live, te, last), MOE_E - 1).astype(jnp.int32)
    ids = jnp.arange(MOE_E, dtype=jnp.int32)[None, :]
    seg_end = jnp.sum(jnp.where(te[:, None] == ids, (starts + counts)[None, :], 0), axis=1)
    n_rows = jnp.clip(seg_end - tile_start, 0, TM_E)
    n_chunks = jnp.where(live, (n_rows + GATHER_CH - 1) // GATHER_CH, 0).astype(jnp.int32)
    pos_tiles = pos.reshape(MT // TM_LN, TM_LN, 2).transpose(0, 2, 1).reshape(-1)
    return te, n_tiles.reshape(1), n_chunks, pos_tiles


def hier_moe_block(xf, i, moe_w_group, moe_b_group, moe_w_egate, moe_b_egate,
                   moe_w1, moe_w3, moe_w2, ln_g, ln_b):
    wr = jnp.concatenate([moe_w_group[i], moe_w_egate[i].transpose(1, 0, 2).reshape(D, MOE_E)], axis=1)
    wr = jnp.pad(wr, ((0, 0), (0, LANES - wr.shape[1])))
    br = jnp.concatenate([moe_b_group[i], moe_b_egate[i].reshape(MOE_E)])
    br = jnp.pad(br, (0, LANES - br.shape[0]))[None, :]
    rec, counts = moe_router(xf, wr, br)
    te, n_tiles, n_chunks, pos_tiles = _dispatch_plan(rec, counts)
    ys = moe_ffn(i, te, n_tiles, n_chunks, pos_tiles, xf, moe_w1, moe_w3, moe_w2)
    return combine_ln(pos_tiles, xf, rec, ln_g[i][None, :], ln_b[i][None, :], ys)


def _sample_rows(a):
    return a[NP:].reshape(DEC_BATCH, TS, a.shape[-1])


def _prompt_tail(a, n):
    return jnp.stack([a[(b + 1) * SEQ - n:(b + 1) * SEQ] for b in range(BATCH)])


def conv_layer(xf, xb, state, w_in, b_in, w_dw, b_dw, g, b, w_out, b_out, ln_g, ln_b):
    u = mm_glu(xb, w_in, b_in[None, :], F32)
    tt_p = 256
    zbuf = jnp.zeros((BATCH, HALO, D), F32)
    z_p = conv_ln_silu(u, 0, BATCH, SEQ, tt_p, zbuf, w_dw, b_dw[None, :], g[None, :], b[None, :], BF16)
    pad = HALO - (CONV_WIDTH - 1)
    sbuf = jnp.pad(state, ((0, 0), (pad, 0), (0, 0)))
    z_s = conv_ln_silu(u, NP // TS, DEC_BATCH, TS, TS, sbuf, w_dw, b_dw[None, :], g[None, :], b[None, :], F32)
    z = jnp.concatenate([z_p, z_s.astype(BF16)], axis=0)
    xf, xb = mm_res_ln(z, w_out.astype(BF16), b_out[None, :], xf, ln_g[None, :], ln_b[None, :])
    new_p = _prompt_tail(u, CONV_WIDTH - 1)
    new_s = jnp.concatenate([state[:, DEC_SEQ:], _sample_rows(u)[:, :DEC_SEQ]], axis=1)
    return xf, xb, new_p, new_s


def hgrn_layer(xf, xb, state, w_in, lb, norm_g, w_out, ln_g, ln_b):
    hk = HG_H * HG_DK
    lb_row = jnp.concatenate([jnp.zeros((hk,), F32), lb, jnp.zeros((2 * hk,), F32)])[None, :]
    p = mm_hgrn(xb, w_in, lb_row)
    s0_p = jnp.zeros((BATCH, HG_H, HG_DK, HG_DV), F32)
    o_p, sf_p = hgrn_scan(p, 0, BATCH, SEQ, 128, 16, 16, s0_p, norm_g[None, :], BF16)
    o_s, sf_s = hgrn_scan(p, NP // TS, DEC_BATCH, TS, TS, TS, DEC_SEQ, state, norm_g[None, :], F32)
    o = jnp.concatenate([o_p, o_s.astype(BF16)], axis=0)
    zero_b = jnp.zeros((1, D), F32)
    xf, xb = mm_res_ln(o, w_out.astype(BF16), zero_b, xf, ln_g[None, :], ln_b[None, :])
    return xf, xb, sf_p, sf_s


def s5_layer(xf, xb, h_re, h_im, lam_re, lam_im, log_dt, b_re, b_im, c_re, c_im, d_skip, w_glu,
             ln_g, ln_b):
    nj, sw, lc = S5_NJ, S5_SW, S5_LC
    ops = _s5_operators(lam_re, lam_im, log_dt, b_re, b_im, c_re, c_im, d_skip)
    nc = SEQ // lc
    up = xb[:NP].reshape(BATCH, nc, lc, nj, LANES).transpose(3, 1, 0, 2, 4).reshape(nj, nc * BATCH, lc * LANES)
    y_p, hf_p = s5_lti(up, jnp.zeros((nj, 2 * BATCH, 2 * sw), F32), ops, nc, lc)
    y_p = y_p.reshape(nj, nc, BATCH, lc, LANES).transpose(2, 1, 3, 0, 4).reshape(NP, D)
    hf_p = hf_p[:, :BATCH].transpose(1, 0, 2)
    us = _sample_rows(xb).reshape(DEC_BATCH, TS, nj, LANES).transpose(2, 0, 1, 3).reshape(nj, DEC_BATCH, TS * LANES)
    h0 = jnp.concatenate([h_re.reshape(DEC_BATCH, nj, sw), h_im.reshape(DEC_BATCH, nj, sw)], -1).transpose(1, 0, 2)
    y_s, hf_s = s5_lti(us, h0, ops, 1, DEC_SEQ)
    y_s = y_s.reshape(nj, DEC_BATCH, TS, LANES).transpose(1, 2, 0, 3).reshape(NS, D)
    hf_s = hf_s.transpose(1, 0, 2)
    y = jnp.concatenate([y_p, y_s], axis=0)
    m = mm_glu(y, w_glu, jnp.zeros((1, 2 * D), F32), BF16)
    xf, xb = res_ln(xf, m, ln_g[None, :], ln_b[None, :])
    st = lambda h, n: (h[..., :sw].reshape(n, SSM_GROUPS, SSM_STATE), h[..., sw:].reshape(n, SSM_GROUPS, SSM_STATE))
    return (xf, xb) + st(hf_p, BATCH) + st(hf_s, DEC_BATCH)


def _rope_tables(pos):
    half = HD // 2
    inv = ROPE_THETA ** (-jnp.arange(half, dtype=F32) / half)
    ang = pos.astype(F32)[:, None] * inv[None, :]
    cos, sin = jnp.cos(ang), jnp.sin(ang)
    cos_t = jnp.tile(jnp.concatenate([cos, cos], -1), (1, LANES // HD))
    sin_t = jnp.tile(jnp.concatenate([-sin, sin], -1), (1, LANES // HD))
    return cos_t, sin_t


def swa_layer(xf, xb, cache_k, cache_v, w_qkv, sinks, w_out, ln_g, ln_b):
    hq = NH * HD
    kvw = NKV * HD
    pos = jnp.concatenate([jnp.tile(jnp.arange(SEQ), BATCH),
                           jnp.tile(PAST_LEN + jnp.arange(TS), DEC_BATCH)])
    cos_t, sin_t = _rope_tables(pos)
    q = mm_rope(xb, w_qkv, 0, hq, cos_t, sin_t, TN_MM // LANES, HD ** -0.5)
    kv = mm_rope(xb, w_qkv, hq, 2 * kvw, cos_t, sin_t, kvw // LANES, 1.0)
    nblk = SEQ // WINDOW
    qw = q.reshape(MT // WINDOW, WINDOW, D)
    kvwin = kv.reshape(MT // WINDOW, WINDOW, 2 * kvw)
    prev = lambda i: (jnp.maximum(i - 1, 0), 0, 0)
    prev_v = lambda i: (jnp.maximum(i - 1, 0), 0, 1)
    o_p = swa_attention(qw, kvwin, kvwin, kvwin, BATCH * nblk, 0, prev, prev_v, sinks, nblk, False).reshape(NP, D)
    kvs = _sample_rows(kv)
    kvs_pad = jnp.pad(kvs, ((0, 0), (0, WINDOW - TS), (0, 0)))
    ck = cache_k.reshape(DEC_BATCH, WINDOW, kvw)
    cv = cache_v.reshape(DEC_BATCH, WINDOW, kvw)
    same = lambda i: (i, 0, 0)
    o_s = swa_attention(q.reshape(MT // TS, TS, D), kvs_pad, ck, cv, DEC_BATCH, NP // TS, same, same,
                        sinks, 1, True).reshape(NS, D)
    o = jnp.concatenate([o_p, o_s], axis=0)
    xf, xb = mm_res_ln(o, w_out.astype(BF16), jnp.zeros((1, D), F32), xf, ln_g[None, :], ln_b[None, :])
    kvp = _prompt_tail(kv, WINDOW)
    nk_p = kvp[..., :kvw].reshape(BATCH, WINDOW, NKV, HD)
    nv_p = kvp[..., kvw:].reshape(BATCH, WINDOW, NKV, HD)
    nk_s = jnp.concatenate([ck[:, DEC_SEQ:], kvs[:, :DEC_SEQ, :kvw]], axis=1).reshape(DEC_BATCH, WINDOW, NKV, HD)
    nv_s = jnp.concatenate([cv[:, DEC_SEQ:], kvs[:, :DEC_SEQ, kvw:]], axis=1).reshape(DEC_BATCH, WINDOW, NKV, HD)
    return xf, xb, nk_p, nv_p, nk_s, nv_s


def kernel(x_prompt, x_sample, state_conv, state_hgrn, state_ssm_re, state_ssm_im, cache_swa_k, cache_swa_v, conv_w_in, conv_b_in, conv_w_dw, conv_b_dw, conv_ln_g, conv_ln_b, conv_w_out, conv_b_out, hgrn_w_in, hgrn_lb_logits, hgrn_norm_g, hgrn_w_out, ssm_lam_re, ssm_lam_im, ssm_log_dt, ssm_b_re, ssm_b_im, ssm_c_re, ssm_c_im, ssm_d, ssm_w_glu, swa_w_qkv, swa_sinks, swa_w_out, ln1_g, ln1_b, ln2_g, ln2_b, moe_w_group, moe_b_group, moe_w_egate, moe_b_egate, moe_w1, moe_w3, moe_w2):
    lbs = jnp.cumsum(jax.nn.softmax(hgrn_lb_logits.astype(F32), axis=0), axis=0)
    lbs = lbs - lbs[0]
    xs_pad = jnp.pad(x_sample, ((0, 0), (0, TS - DEC_SEQ), (0, 0)))
    xf = jnp.concatenate([x_prompt.reshape(NP, D), xs_pad.reshape(NS, D)], axis=0)
    xb = xf.astype(BF16)
    moe = (moe_w_group, moe_b_group, moe_w_egate, moe_b_egate, moe_w1, moe_w3, moe_w2, ln2_g, ln2_b)

    xf, xb, conv_p, conv_s = conv_layer(
        xf, xb, state_conv[0], conv_w_in[0], conv_b_in[0], conv_w_dw[0], conv_b_dw[0],
        conv_ln_g[0], conv_ln_b[0], conv_w_out[0], conv_b_out[0], ln1_g[0], ln1_b[0])
    xf, xb = hier_moe_block(xf, 0, *moe)

    xf, xb, hg_p, hg_s = hgrn_layer(xf, xb, state_hgrn[0], hgrn_w_in[0], lbs[1], hgrn_norm_g[0],
                                    hgrn_w_out[0], ln1_g[1], ln1_b[1])
    xf, xb = hier_moe_block(xf, 1, *moe)

    xf, xb, hr_p, hi_p, hr_s, hi_s = s5_layer(
        xf, xb, state_ssm_re[0], state_ssm_im[0], ssm_lam_re[0], ssm_lam_im[0], ssm_log_dt[0],
        ssm_b_re[0], ssm_b_im[0], ssm_c_re[0], ssm_c_im[0], ssm_d[0], ssm_w_glu[0], ln1_g[2], ln1_b[2])
    xf, xb = hier_moe_block(xf, 2, *moe)

    xf, xb, k_p, v_p, k_s, v_s = swa_layer(xf, xb, cache_swa_k[0], cache_swa_v[0], swa_w_qkv[0],
                                            swa_sinks[0], swa_w_out[0], ln1_g[3], ln1_b[3])
    xf, xb = hier_moe_block(xf, 3, *moe)

    y_p = xf[:NP].reshape(BATCH, SEQ, D)
    y_s = _sample_rows(xf)[:, :DEC_SEQ]
    return (y_p, y_s, conv_p[None], conv_s[None], hg_p[None], hg_s[None],
            hr_p[None], hi_p[None], hr_s[None], hi_s[None],
            k_p[None], v_p[None], k_s[None], v_s[None])
```

```python
import functools

import jax
import jax.numpy as jnp
from jax import lax
from jax.experimental import pallas as pl
from jax.experimental.pallas import tpu as pltpu

F32 = jnp.float32
BF16 = jnp.bfloat16

D = 2048
BATCH = 4
SEQ = 2048
DEPTH = 4
DEC_BATCH = 32
DEC_SEQ = 4
PAST_LEN = 16384
CONV_WIDTH = 31
HG_H = 16
HG_DK = 128
HG_DV = 128
SSM_GROUP = 16
SSM_GROUPS = 128
SSM_STATE = 64
HD = 64
NH = 32
NKV = 4
WINDOW = 128
ROPE_THETA = 10000.0
MOE_E = 16
MOE_FF = 512
ALPHA = (2.0 * DEPTH) ** 0.25
LN_EPS = 1e-5
RMS_EPS = 1e-6

TS = 8
NP = BATCH * SEQ
NS = DEC_BATCH * TS
MT = NP + NS
LANES = 128
SUBLANES = 8
VMEM_LIMIT = 56 * 1024 * 1024

TM_MM = 768
TN_MM = 512
TM_LN = 384
TM_E = 512
P_SLOTS = ((2 * MT + MOE_E * (TM_E - 1) + TM_E - 1) // TM_E) * TM_E
N_ETILES = P_SLOTS // TM_E
GATHER_CH = 64
SMEM_UNROLL = 8


def _cp(*sem):
    return pltpu.CompilerParams(dimension_semantics=sem, vmem_limit_bytes=VMEM_LIMIT)


def _sigmoid(x):
    return 1.0 / (1.0 + jnp.exp(-x))


def _silu(x):
    return x * _sigmoid(x)


def _ln_rows(z, g, b):
    mu = jnp.mean(z, -1, keepdims=True)
    zc = z - mu
    var = jnp.mean(zc * zc, -1, keepdims=True)
    return zc * lax.rsqrt(var + LN_EPS) * g + b


def _cast_weights_once(pairs):
    @pl.when(pl.program_id(1) == 0)
    def _():
        for src, dst in pairs:
            dst[...] = src[...].astype(BF16)


def _mm_glu_kernel(x_ref, wa_ref, wg_ref, ba_ref, bg_ref, o_ref, wab, wgb):
    _cast_weights_once(((wa_ref, wab), (wg_ref, wgb)))
    x = x_ref[...]
    a = jnp.dot(x, wab[...], preferred_element_type=F32) + ba_ref[...]
    g = jnp.dot(x, wgb[...], preferred_element_type=F32) + bg_ref[...]
    o_ref[...] = (a * _sigmoid(g)).astype(o_ref.dtype)


def mm_glu(x, w, b, out_dtype):
    m, k = x.shape
    n = w.shape[1] // 2
    nj = n // TN_MM
    return pl.pallas_call(
        _mm_glu_kernel,
        grid=(nj, m // TM_MM),
        in_specs=[
            pl.BlockSpec((TM_MM, k), lambda j, i: (i, 0)),
            pl.BlockSpec((k, TN_MM), lambda j, i: (0, j)),
            pl.BlockSpec((k, TN_MM), lambda j, i: (0, j + nj)),
            pl.BlockSpec((1, TN_MM), lambda j, i: (0, j)),
            pl.BlockSpec((1, TN_MM), lambda j, i: (0, j + nj)),
        ],
        out_specs=pl.BlockSpec((TM_MM, TN_MM), lambda j, i: (i, j)),
        out_shape=jax.ShapeDtypeStruct((m, n), out_dtype),
        scratch_shapes=[pltpu.VMEM((k, TN_MM), BF16), pltpu.VMEM((k, TN_MM), BF16)],
        compiler_params=_cp("parallel", "arbitrary"),
        name="mm_glu",
    )(x, w, w, b, b)


def _mm_hgrn_kernel(x_ref, w_ref, lb_ref, o_ref, wb, *, nsec):
    j = pl.program_id(0)
    _cast_weights_once(((w_ref, wb),))
    acc = jnp.dot(x_ref[...], wb[...], preferred_element_type=F32)

    @pl.when(j < nsec)
    def _():
        o_ref[...] = _silu(acc)

    @pl.when((j >= nsec) & (j < 2 * nsec))
    def _():
        lb = lb_ref[...]
        o_ref[...] = jnp.log(lb + (1.0 - lb) * _sigmoid(acc))

    @pl.when((j >= 2 * nsec) & (j < 3 * nsec))
    def _():
        o_ref[...] = acc

    @pl.when(j >= 3 * nsec)
    def _():
        o_ref[...] = _silu(acc)


def mm_hgrn(x, w, lb_row):
    m, k = x.shape
    n = w.shape[1]
    tn = 2 * TN_MM
    nsec = (n // 4) // tn
    return pl.pallas_call(
        functools.partial(_mm_hgrn_kernel, nsec=nsec),
        grid=(n // tn, m // TM_MM),
        in_specs=[
            pl.BlockSpec((TM_MM, k), lambda j, i: (i, 0)),
            pl.BlockSpec((k, tn), lambda j, i: (0, j)),
            pl.BlockSpec((1, tn), lambda j, i: (0, j)),
        ],
        out_specs=pl.BlockSpec((TM_MM, tn), lambda j, i: (i, j)),
        out_shape=jax.ShapeDtypeStruct((m, n), F32),
        scratch_shapes=[pltpu.VMEM((k, tn), BF16)],
        compiler_params=_cp("parallel", "arbitrary"),
        name="mm_hgrn",
    )(x, w, lb_row)


def _swap_half_heads(x, lane):
    return jnp.where((lane % HD) < HD // 2, pltpu.roll(x, LANES - HD // 2, 1),
                     pltpu.roll(x, HD // 2, 1))


def _mm_rope_kernel(x_ref, w_ref, cos_ref, sin_ref, o_ref, wb, *, n_rope, scale):
    _cast_weights_once(((w_ref, wb),))
    acc = jnp.dot(x_ref[...], wb[...], preferred_element_type=F32)
    cos = cos_ref[...]
    sin = sin_ref[...]
    lane = lax.broadcasted_iota(jnp.int32, cos.shape, 1)
    for c in range(acc.shape[1] // LANES):
        xc = acc[:, c * LANES:(c + 1) * LANES]
        if c < n_rope:
            xc = (xc * cos + _swap_half_heads(xc, lane) * sin) * scale
        o_ref[:, c * LANES:(c + 1) * LANES] = xc


def mm_rope(x, w, col0, n, cos, sin, n_rope, scale):
    m, k = x.shape
    j0 = col0 // TN_MM
    return pl.pallas_call(
        functools.partial(_mm_rope_kernel, n_rope=n_rope, scale=scale),
        grid=(n // TN_MM, m // TM_MM),
        in_specs=[
            pl.BlockSpec((TM_MM, k), lambda j, i: (i, 0)),
            pl.BlockSpec((k, TN_MM), lambda j, i: (0, j + j0)),
            pl.BlockSpec((TM_MM, LANES), lambda j, i: (i, 0)),
            pl.BlockSpec((TM_MM, LANES), lambda j, i: (i, 0)),
        ],
        out_specs=pl.BlockSpec((TM_MM, TN_MM), lambda j, i: (i, j)),
        out_shape=jax.ShapeDtypeStruct((m, n), F32),
        scratch_shapes=[pltpu.VMEM((k, TN_MM), BF16)],
        compiler_params=_cp("parallel", "arbitrary"),
        name="mm_rope",
    )(x, w, cos, sin)


def _mm_res_ln_kernel(x_ref, w_ref, b_ref, res_ref, g_ref, beta_ref, of_ref, ob_ref):
    y = jnp.dot(x_ref[...], w_ref[...], preferred_element_type=F32) + b_ref[...]
    out = _ln_rows(ALPHA * res_ref[...] + y, g_ref[...], beta_ref[...])
    of_ref[...] = out
    ob_ref[...] = out.astype(BF16)


def mm_res_ln(x, w, b, res, g, beta):
    m, k = x.shape
    row = lambda i: (i, 0)
    fix = lambda i: (0, 0)
    return pl.pallas_call(
        _mm_res_ln_kernel,
        grid=(m // TM_LN,),
        in_specs=[
            pl.BlockSpec((TM_LN, k), row),
            pl.BlockSpec((k, D), fix),
            pl.BlockSpec((1, D), fix),
            pl.BlockSpec((TM_LN, D), row),
            pl.BlockSpec((1, D), fix),
            pl.BlockSpec((1, D), fix),
        ],
        out_specs=[pl.BlockSpec((TM_LN, D), row), pl.BlockSpec((TM_LN, D), row)],
        out_shape=[jax.ShapeDtypeStruct((m, D), F32), jax.ShapeDtypeStruct((m, D), BF16)],
        compiler_params=_cp("parallel"),
        name="mm_res_ln",
    )(x, w, b, res, g, beta)


def _res_ln_kernel(res_ref, y_ref, g_ref, beta_ref, of_ref, ob_ref):
    out = _ln_rows(ALPHA * res_ref[...] + y_ref[...].astype(F32), g_ref[...], beta_ref[...])
    of_ref[...] = out
    ob_ref[...] = out.astype(BF16)


def res_ln(res, y, g, beta):
    m = res.shape[0]
    row = lambda i: (i, 0)
    fix = lambda i: (0, 0)
    return pl.pallas_call(
        _res_ln_kernel,
        grid=(m // TM_LN,),
        in_specs=[pl.BlockSpec((TM_LN, D), row), pl.BlockSpec((TM_LN, D), row),
                  pl.BlockSpec((1, D), fix), pl.BlockSpec((1, D), fix)],
        out_specs=[pl.BlockSpec((TM_LN, D), row), pl.BlockSpec((TM_LN, D), row)],
        out_shape=[jax.ShapeDtypeStruct((m, D), F32), jax.ShapeDtypeStruct((m, D), BF16)],
        compiler_params=_cp("parallel"),
        name="res_ln",
    )(res, y, g, beta)


HALO = 32
CONV_RC = 64
CONV_LC = 256


def _conv_kernel(u_ref, buf_ref, wdw_ref, bdw_ref, g_ref, b_ref, z_ref, ext_ref, sh_ref, c_ref, *, tt):
    i = pl.program_id(1)

    @pl.when(i == 0)
    def _():
        ext_ref[0:HALO, :] = buf_ref[0]

    @pl.when(i > 0)
    def _():
        ext_ref[0:HALO, :] = ext_ref[tt:tt + HALO, :]

    ext_ref[HALO:HALO + tt, :] = u_ref[...]
    nsh = tt + HALO - SUBLANES
    for b in range(1, SUBLANES):
        sh_ref[b - 1] = ext_ref[b:b + nsh, :]
    pad = HALO - (CONV_WIDTH - 1)
    rc = min(CONV_RC, tt)
    for r0 in range(0, tt, rc):
        for c0 in range(0, D, CONV_LC):
            acc = jnp.zeros((rc, CONV_LC), F32) + bdw_ref[:, c0:c0 + CONV_LC]
            for w in range(CONV_WIDTH):
                a, b = divmod(pad + w, SUBLANES)
                rows = slice(r0 + a * SUBLANES, r0 + a * SUBLANES + rc)
                if b == 0:
                    tap = ext_ref[rows, c0:c0 + CONV_LC]
                else:
                    tap = sh_ref[b - 1, rows, c0:c0 + CONV_LC]
                acc = acc + tap * wdw_ref[w:w + 1, c0:c0 + CONV_LC]
            c_ref[r0:r0 + rc, c0:c0 + CONV_LC] = acc
    z_ref[...] = _silu(_ln_rows(c_ref[...], g_ref[...], b_ref[...])).astype(z_ref.dtype)


def conv_ln_silu(u, row0_blocks, nseq, t, tt, buf, wdw, bdw, g, b, out_dtype):
    nt = t // tt
    fix = lambda s, i: (0, 0)
    return pl.pallas_call(
        functools.partial(_conv_kernel, tt=tt),
        grid=(nseq, nt),
        in_specs=[
            pl.BlockSpec((tt, D), lambda s, i: (row0_blocks + s * nt + i, 0)),
            pl.BlockSpec((1, HALO, D), lambda s, i: (s, 0, 0)),
            pl.BlockSpec((CONV_WIDTH, D), fix),
            pl.BlockSpec((1, D), fix),
            pl.BlockSpec((1, D), fix),
            pl.BlockSpec((1, D), fix),
        ],
        out_specs=pl.BlockSpec((tt, D), lambda s, i: (s * nt + i, 0)),
        out_shape=jax.ShapeDtypeStruct((nseq * t, D), out_dtype),
        scratch_shapes=[pltpu.VMEM((HALO + tt, D), F32),
                        pltpu.VMEM((SUBLANES - 1, tt + HALO - SUBLANES, D), F32),
                        pltpu.VMEM((tt, D), F32)],
        compiler_params=_cp("parallel", "arbitrary"),
        name="conv_ln_silu",
    )(u, buf, wdw, bdw, g, b)


def _hgrn_kernel(p_ref, s0_ref, ng_ref, o_ref, sf_ref, st_ref, pst_ref, ob_ref, bl_ref, *, tt, sc, nvalid):
    i = pl.program_id(1)
    hk = HG_H * HG_DK

    @pl.when(i == 0)
    def _():
        for h in range(HG_H):
            st_ref[h * HG_DV:(h + 1) * HG_DV, :] = s0_ref[0, h].T

    ones = jnp.ones((LANES, LANES), BF16)
    row = lax.broadcasted_iota(jnp.int32, (sc, hk), 0)
    first = [(s // SUBLANES) * SUBLANES for s in range(nvalid)]
    offs = [sum(sc - f for f in first[:s]) for s in range(nvalid + 1)]

    if nvalid == sc:
        rr = lax.broadcasted_iota(jnp.int32, (tt, tt), 0)
        cc = lax.broadcasted_iota(jnp.int32, (tt, tt), 1)
        tril = ((cc <= rr) & (cc // sc == rr // sc)).astype(F32)
        for h in range(HG_H):
            sl = slice(h * HG_DK, (h + 1) * HG_DK)
            bl_ref[:, sl] = jnp.dot(tril, p_ref[:, hk + h * HG_DK:hk + (h + 1) * HG_DK],
                                    preferred_element_type=F32, precision=lax.Precision.HIGHEST)

    def sub_chunk(ci, carry):
        r0 = pl.multiple_of(ci * sc, sc)
        q = p_ref[pl.ds(r0, sc), 0:hk]
        lf = p_ref[pl.ds(r0, sc), hk:2 * hk]
        v = p_ref[pl.ds(r0, sc), 2 * hk:3 * hk]
        if nvalid < sc:
            lf = jnp.where(row < nvalid, lf, 0.0)
            bl = jnp.zeros((sc, hk), F32)
            for s in range(nvalid):
                bl = bl + jnp.where(row >= s, lf[s:s + 1, :], 0.0)
        else:
            bl = bl_ref[pl.ds(r0, sc), :]
        kk = 1.0 - jnp.exp(lf)
        btot = bl[sc - 1:sc, :]
        qd = q * jnp.exp(bl)
        kd = kk * jnp.exp(btot - bl)
        for s in range(nvalid):
            f = first[s]
            rowf = lax.broadcasted_iota(jnp.int32, (sc - f, hk), 0) + f
            e = jnp.exp(jnp.where(rowf >= s, bl[f:] - bl[s:s + 1, :], -jnp.inf))
            pst_ref[offs[s]:offs[s + 1], :] = q[f:] * e * kk[s:s + 1, :]
        o_acc = []
        for h in range(HG_H):
            sl = slice(h * HG_DK, (h + 1) * HG_DK)
            rs = jnp.dot(pst_ref[0:offs[nvalid], sl].astype(BF16), ones,
                         preferred_element_type=F32)
            vh = v[:, sl]
            st_h = st_ref[h * HG_DV:(h + 1) * HG_DV, :]
            oh = lax.dot_general(qd[:, sl].astype(BF16), st_h.astype(BF16),
                                 (((1,), (1,)), ((), ())), preferred_element_type=F32)
            tiles = [oh[f:f + SUBLANES] for f in range(0, sc, SUBLANES)]
            for s in range(nvalid):
                for ti, f in enumerate(range(first[s], sc, SUBLANES)):
                    piece = rs[offs[s] + ti * SUBLANES:offs[s] + (ti + 1) * SUBLANES, :]
                    tiles[f // SUBLANES] = tiles[f // SUBLANES] + piece * vh[s:s + 1, :]
            o_acc.append(jnp.concatenate(tiles, axis=0) if len(tiles) > 1 else tiles[0])
            upd = jnp.dot(vh.T.astype(BF16), kd[:, sl].astype(BF16), preferred_element_type=F32)
            st_ref[h * HG_DV:(h + 1) * HG_DV, :] = jnp.exp(btot[:, sl]) * st_h + upd
        ob_ref[pl.ds(r0, sc), :] = jnp.concatenate(o_acc, axis=1)
        return carry

    lax.fori_loop(0, tt // sc, sub_chunk, 0)

    o = ob_ref[...]
    o2 = o * o
    parts = []
    for h in range(HG_H):
        sl = slice(h * HG_DV, (h + 1) * HG_DV)
        ms = jnp.dot(o2[:, sl].astype(BF16), ones, preferred_element_type=F32) * (1.0 / HG_DV)
        parts.append(o[:, sl] * lax.rsqrt(ms + RMS_EPS))
    on = jnp.concatenate(parts, axis=1) * ng_ref[...] * p_ref[:, 3 * hk:4 * hk]
    o_ref[...] = on.astype(o_ref.dtype)

    @pl.when(i == pl.num_programs(1) - 1)
    def _():
        for h in range(HG_H):
            sf_ref[0, h] = st_ref[h * HG_DV:(h + 1) * HG_DV, :].T


def hgrn_scan(p, row0_blocks, nseq, t, tt, sc, nvalid, s0, norm_g, out_dtype):
    nt = t // tt
    return pl.pallas_call(
        functools.partial(_hgrn_kernel, tt=tt, sc=sc, nvalid=nvalid),
        grid=(nseq, nt),
        in_specs=[
            pl.BlockSpec((tt, 4 * D), lambda s, i: (row0_blocks + s * nt + i, 0)),
            pl.BlockSpec((1, HG_H, HG_DK, HG_DV), lambda s, i: (s, 0, 0, 0)),
            pl.BlockSpec((1, D), lambda s, i: (0, 0)),
        ],
        out_specs=[
            pl.BlockSpec((tt, D), lambda s, i: (s * nt + i, 0)),
            pl.BlockSpec((1, HG_H, HG_DK, HG_DV), lambda s, i: (s, 0, 0, 0)),
        ],
        out_shape=[jax.ShapeDtypeStruct((nseq * t, D), out_dtype),
                   jax.ShapeDtypeStruct((nseq, HG_H, HG_DK, HG_DV), F32)],
        scratch_shapes=[pltpu.VMEM((HG_H * HG_DV, HG_DK), F32),
                        pltpu.VMEM((sc * sc, D), F32),
                        pltpu.VMEM((tt, D), F32),
                        pltpu.VMEM((tt, D), F32)],
        compiler_params=_cp("parallel", "arbitrary"),
        name="hgrn_scan",
    )(p, s0, norm_g)


S5_GPB = LANES // SSM_GROUP
S5_NJ = D // LANES
S5_SW = S5_GPB * SSM_STATE
S5_LC = 8


def _spread_groups(blk, rep, mask):
    full = jnp.dot(blk.astype(BF16), rep, preferred_element_type=F32)
    return jnp.where(mask, full, 0.0).astype(BF16)


def _group_mask(rows, row_w, cols, col_w):
    r = lax.broadcasted_iota(jnp.int32, (rows, cols), 0) // row_w
    c = lax.broadcasted_iota(jnp.int32, (rows, cols), 1) // col_w
    return r == c


def _rep_matrix(w, cols):
    r = lax.broadcasted_iota(jnp.int32, (w, cols), 0)
    c = lax.broadcasted_iota(jnp.int32, (w, cols), 1) % w
    return (r == c).astype(BF16)


def _s5_kernel(u_ref, h0_ref, klt_ref, wtr_ref, wti_ref, car_ref, cai_ref, ar_ref, ai_ref, y_ref, hf_ref,
               t_s, bp_s, cp_s, con_ref, hp_ref, *, nc, nv):
    sw = S5_SW
    lc = S5_LC
    c, p = SSM_GROUP, SSM_STATE
    rep_c = _rep_matrix(c, LANES)
    rep_p = _rep_matrix(p, sw)
    t_s[...] = jnp.zeros(t_s.shape, BF16)
    mask_t = _group_mask(LANES, c, LANES, c)
    for l in range(lc):
        blk = _spread_groups(klt_ref[0, l], rep_c, mask_t)
        for s in range(lc - l):
            t_s[s * LANES:(s + 1) * LANES, (s + l) * LANES:(s + l + 1) * LANES] = blk
    mask_b = _group_mask(LANES, c, sw, p)
    for s in range(lc):
        if s < nv:
            bp_s[s * LANES:(s + 1) * LANES, 0:sw] = _spread_groups(wtr_ref[0, nv - 1 - s], rep_p, mask_b)
            bp_s[s * LANES:(s + 1) * LANES, sw:2 * sw] = _spread_groups(wti_ref[0, nv - 1 - s], rep_p, mask_b)
        else:
            bp_s[s * LANES:(s + 1) * LANES, :] = jnp.zeros((LANES, 2 * sw), BF16)
    mask_c = _group_mask(sw, p, LANES, c)
    for t in range(lc):
        cp_s[0:sw, t * LANES:(t + 1) * LANES] = _spread_groups(car_ref[0, t], rep_c, mask_c)
        cp_s[sw:2 * sw, t * LANES:(t + 1) * LANES] = _spread_groups(cai_ref[0, t], rep_c, mask_c)

    u = u_ref[0]
    con_ref[...] = jnp.dot(u, bp_s[...], preferred_element_type=F32)
    ar = ar_ref[0]
    ai = ai_ref[0]

    def step(hr, hi, cr, ci):
        return ar * hr - ai * hi + cr, ar * hi + ai * hr + ci

    if nc == 1:
        hr, hi = h0_ref[0, :, 0:sw], h0_ref[0, :, sw:2 * sw]
        hp_ref[:, 0:sw] = hr
        hp_ref[:, sw:2 * sw] = hi
        hr, hi = step(hr, hi, con_ref[:, 0:sw], con_ref[:, sw:2 * sw])
    else:
        half = 4
        top = lax.broadcasted_iota(jnp.int32, (2 * half, sw), 0) < half

        def pair(k, carry):
            hr, hi = carry
            r0 = pl.multiple_of(k * 2 * half, 2 * half)
            cr = con_ref[pl.ds(r0, 2 * half), 0:sw]
            ci = con_ref[pl.ds(r0, 2 * half), sw:2 * sw]
            ar1, ai1 = step(hr, hi, cr, ci)
            ar1 = jnp.where(top, ar1, pltpu.roll(ar1, half, 0))
            ai1 = jnp.where(top, ai1, pltpu.roll(ai1, half, 0))
            hp_ref[pl.ds(r0, 2 * half), 0:sw] = jnp.where(top, hr, ar1)
            hp_ref[pl.ds(r0, 2 * half), sw:2 * sw] = jnp.where(top, hi, ai1)
            br, bi = step(ar1, ai1, cr, ci)
            br = jnp.where(top, pltpu.roll(br, half, 0), br)
            bi = jnp.where(top, pltpu.roll(bi, half, 0), bi)
            return br, bi

        hr, hi = lax.fori_loop(0, nc // 2, pair, (h0_ref[0, :, 0:sw], h0_ref[0, :, sw:2 * sw]))
    hf_ref[0, :, 0:sw] = hr
    hf_ref[0, :, sw:2 * sw] = hi
    y = (jnp.dot(u, t_s[...], preferred_element_type=F32)
         + jnp.dot(hp_ref[...].astype(BF16), cp_s[...], preferred_element_type=F32))
    y_ref[0] = jax.nn.gelu(y).astype(y_ref.dtype)


def s5_lti(u, h0, ops, nc, nv):
    klt, wtr, wti, car, cai, pw_re, pw_im = ops
    nj, rows, lw = u.shape
    hrows = h0.shape[1]
    sw, lc, c, p = S5_SW, S5_LC, SSM_GROUP, SSM_STATE
    ar = pw_re[:, nv].reshape(nj, 1, sw)
    ai = pw_im[:, nv].reshape(nj, 1, sw)
    blk = lambda j: (j, 0, 0)
    blk4 = lambda j: (j, 0, 0, 0)
    return pl.pallas_call(
        functools.partial(_s5_kernel, nc=nc, nv=nv),
        grid=(nj,),
        in_specs=[
            pl.BlockSpec((1, rows, lw), blk),
            pl.BlockSpec((1, hrows, 2 * sw), blk),
            pl.BlockSpec((1, lc, LANES, c), blk4),
            pl.BlockSpec((1, lc, LANES, p), blk4),
            pl.BlockSpec((1, lc, LANES, p), blk4),
            pl.BlockSpec((1, lc, sw, c), blk4),
            pl.BlockSpec((1, lc, sw, c), blk4),
            pl.BlockSpec((1, 1, sw), blk),
            pl.BlockSpec((1, 1, sw), blk),
        ],
        out_specs=[pl.BlockSpec((1, rows, lw), blk), pl.BlockSpec((1, hrows, 2 * sw), blk)],
        out_shape=[jax.ShapeDtypeStruct((nj, rows, lw), BF16),
                   jax.ShapeDtypeStruct((nj, hrows, 2 * sw), F32)],
        scratch_shapes=[pltpu.VMEM((lw, lw), BF16), pltpu.VMEM((lw, 2 * sw), BF16),
                        pltpu.VMEM((2 * sw, lw), BF16),
                        pltpu.VMEM((rows, 2 * sw), F32), pltpu.VMEM((rows, 2 * sw), F32)],
        compiler_params=_cp("parallel"),
        name="s5_lti",
    )(u, h0, klt, wtr, wti, car, cai, ar, ai)


def _s5_operators(lam_re, lam_im, log_dt, b_re, b_im, c_re, c_im, d_skip):
    hp = lax.Precision.HIGHEST
    gpb, nj, c, p, lc = S5_GPB, S5_NJ, SSM_GROUP, SSM_STATE, S5_LC
    dt = jnp.exp(log_dt)[:, None]
    lr, li = lam_re, lam_im
    mag = jnp.exp(lr * dt)
    ab_re, ab_im = mag * jnp.cos(li * dt), mag * jnp.sin(li * dt)
    nr, ni = ab_re - 1.0, ab_im
    den = lr * lr + li * li
    z_re, z_im = (nr * lr + ni * li) / den, (ni * lr - nr * li) / den
    bbt_re = (z_re[..., None] * b_re - z_im[..., None] * b_im).transpose(0, 2, 1)
    bbt_im = (z_re[..., None] * b_im + z_im[..., None] * b_re).transpose(0, 2, 1)
    l = jnp.arange(lc + 1, dtype=F32)[None, :, None]
    pmag = jnp.exp(l * (lr * dt)[:, None, :])
    pw_re = pmag * jnp.cos(l * (li * dt)[:, None, :])
    pw_im = pmag * jnp.sin(l * (li * dt)[:, None, :])
    wt_re = pw_re[:, :lc, None, :] * bbt_re[:, None] - pw_im[:, :lc, None, :] * bbt_im[:, None]
    wt_im = pw_re[:, :lc, None, :] * bbt_im[:, None] + pw_im[:, :lc, None, :] * bbt_re[:, None]
    klt = (jnp.einsum('glcp,gop->glco', wt_re, c_re, precision=hp)
           - jnp.einsum('glcp,gop->glco', wt_im, c_im, precision=hp))
    klt = klt.at[:, 0].add(d_skip.reshape(SSM_GROUPS, c)[:, :, None] * jnp.eye(c, dtype=F32)[None])
    ct_re, ct_im = c_re.transpose(0, 2, 1)[:, None], c_im.transpose(0, 2, 1)[:, None]
    ca_re = ct_re * pw_re[:, 1:, :, None] - ct_im * pw_im[:, 1:, :, None]
    ca_im = ct_re * pw_im[:, 1:, :, None] + ct_im * pw_re[:, 1:, :, None]

    def by_block(a):
        _, ll, r, w = a.shape
        return a.reshape(nj, gpb, ll, r, w).transpose(0, 2, 1, 3, 4).reshape(nj, ll, gpb * r, w)

    def pw_block(a):
        return a.reshape(nj, gpb, lc + 1, p).transpose(0, 2, 1, 3).reshape(nj, lc + 1, gpb * p)

    return (by_block(klt), by_block(wt_re), by_block(wt_im), by_block(ca_re), by_block(-ca_im),
            pw_block(pw_re), pw_block(pw_im))


def _attn_kernel(q_ref, kv_ref, pk_ref, pv_ref, sink_ref, o_ref, *, tq, nblk, always_prev):
    idx = pl.program_id(0)
    kvw = NKV * HD
    group = NH // NKV
    pkn = pk_ref.shape[1]
    to = kv_ref.shape[1]
    nk = pkn + to
    rows = group * tq
    a = lax.broadcasted_iota(jnp.int32, (rows, nk), 0) % tq
    col = lax.broadcasted_iota(jnp.int32, (rows, nk), 1)
    prev_ok = col >= a
    if not always_prev:
        prev_ok = prev_ok & ((idx % nblk) > 0)
    mask = ((col < pkn) & prev_ok) | ((col >= pkn) & ((col - pkn) <= a))
    hrow = lax.broadcasted_iota(jnp.int32, (rows, 1), 0) // tq
    low = lax.broadcasted_iota(jnp.int32, (tq, LANES), 1) < HD
    nt = (((1,), (1,)), ((), ()))
    for kb in range(NKV // 2):
        lanes = slice(kb * LANES, (kb + 1) * LANES)
        kcat = jnp.concatenate([pk_ref[0, :, lanes], kv_ref[0, :, lanes]], axis=0).astype(BF16)
        vcat = jnp.concatenate([pv_ref[0, :, lanes],
                                kv_ref[0, :, kvw + kb * LANES:kvw + (kb + 1) * LANES]], axis=0).astype(BF16)
        for hi in range(2):
            j = 2 * kb + hi
            parts = []
            sk = jnp.zeros((rows, 1), F32)
            for h8 in range(group):
                qb, e = (group * j + h8) // 2, h8 % 2
                q2 = q_ref[0, :, qb * LANES:(qb + 1) * LANES]
                qe = jnp.where(low if e == 0 else jnp.logical_not(low), q2, 0.0)
                if e != hi:
                    qe = pltpu.roll(qe, HD, 1)
                parts.append(qe)
                sk = jnp.where(hrow == h8, sink_ref[group * j + h8], sk)
            q8 = jnp.concatenate(parts, axis=0).astype(BF16)
            s = lax.dot_general(q8, kcat, nt, preferred_element_type=F32)
            s = jnp.where(mask, s, -jnp.inf)
            m = jnp.maximum(jnp.max(s, -1, keepdims=True), sk)
            p = jnp.exp(s - m)
            inv = 1.0 / (jnp.sum(p, -1, keepdims=True) + jnp.exp(sk - m))
            o8 = jnp.dot((p * inv).astype(BF16), vcat, preferred_element_type=F32)
            for pr in range(group // 2):
                qb = (group * j) // 2 + pr
                pair = []
                for e in range(2):
                    oe = o8[(2 * pr + e) * tq:(2 * pr + e + 1) * tq, :]
                    if e != hi:
                        oe = pltpu.roll(oe, HD, 1)
                    pair.append(oe)
                o_ref[0, :, qb * LANES:(qb + 1) * LANES] = jnp.where(low, pair[0], pair[1]).astype(o_ref.dtype)


def swa_attention(q3, kv3, pk3, pv3, nb, blk0, pk_map, pv_map, sinks, nblk, always_prev):
    tq = q3.shape[1]
    to = kv3.shape[1]
    kvw = NKV * HD
    return pl.pallas_call(
        functools.partial(_attn_kernel, tq=tq, nblk=nblk, always_prev=always_prev),
        grid=(nb,),
        in_specs=[
            pl.BlockSpec((1, tq, D), lambda i: (i + blk0, 0, 0)),
            pl.BlockSpec((1, to, 2 * kvw), lambda i: (i, 0, 0)),
            pl.BlockSpec((1, pk3.shape[1], kvw), pk_map),
            pl.BlockSpec((1, pv3.shape[1], kvw), pv_map),
            pl.BlockSpec(memory_space=pltpu.SMEM),
        ],
        out_specs=pl.BlockSpec((1, tq, D), lambda i: (i, 0, 0)),
        out_shape=jax.ShapeDtypeStruct((nb, tq, D), BF16),
        compiler_params=_cp("parallel"),
        name="swa_attention",
    )(q3, kv3, pk3, pv3, sinks)


R_E1, R_E2, R_W1, R_W2, R_R1, R_R2 = 0, 1, 2, 3, 4, 5


def _router_kernel(x_ref, wh_ref, wl_ref, b_ref, o_ref, cnt_ref, carry_ref):
    i = pl.program_id(0)

    @pl.when(i == 0)
    def _():
        carry_ref[...] = jnp.zeros(carry_ref.shape, F32)

    x = x_ref[...]
    xh = x.astype(BF16)
    xl = (x - xh.astype(F32)).astype(BF16)
    wh = wh_ref[...]
    l = (jnp.dot(xh, wh, preferred_element_type=F32) + jnp.dot(xl, wh, preferred_element_type=F32)
         + jnp.dot(xh, wl_ref[...], preferred_element_type=F32) + b_ref[...])
    tm = l.shape[0]
    lane = lax.broadcasted_iota(jnp.int32, l.shape, 1).astype(F32)
    big = 1e9
    ninf = -jnp.inf
    gmask = lane < 4.0
    gl = jnp.where(gmask, l, ninf)
    gm = jnp.max(gl, -1, keepdims=True)
    gsel = jnp.min(jnp.where(gl == gm, lane, big), -1, keepdims=True)
    gsum = jnp.sum(jnp.where(gmask, jnp.exp(gl - gm), 0.0), -1, keepdims=True)
    gval = 1.0 / gsum
    lo = 4.0 + 4.0 * gsel
    emask = (lane >= lo) & (lane < lo + 4.0)
    el = jnp.where(emask, l, ninf)
    m1 = jnp.max(el, -1, keepdims=True)
    i1 = jnp.min(jnp.where(emask & (el == m1), lane, big), -1, keepdims=True)
    emask2 = emask & (lane != i1)
    el2 = jnp.where(emask2, l, ninf)
    m2 = jnp.max(el2, -1, keepdims=True)
    i2 = jnp.min(jnp.where(emask2 & (el2 == m2), lane, big), -1, keepdims=True)
    r = jnp.exp(m2 - m1)
    w1 = gval / (1.0 + r)
    w2 = gval * r / (1.0 + r)
    e1 = i1 - 4.0
    e2 = i2 - 4.0
    oh = ((lane == e1) | (lane == e2)).astype(F32)
    tri = (lax.broadcasted_iota(jnp.int32, (tm, tm), 1)
           < lax.broadcasted_iota(jnp.int32, (tm, tm), 0)).astype(BF16)
    before = jnp.dot(tri, oh.astype(BF16), preferred_element_type=F32) + carry_ref[...]
    r1 = jnp.sum(jnp.where(lane == e1, before, 0.0), -1, keepdims=True)
    r2 = jnp.sum(jnp.where(lane == e2, before, 0.0), -1, keepdims=True)
    carry_ref[...] = carry_ref[...] + jnp.sum(oh, axis=0, keepdims=True)
    cnt_ref[...] = carry_ref[...]
    rec = jnp.zeros(l.shape, F32)
    for ln, val in ((R_E1, e1), (R_E2, e2), (R_W1, w1), (R_W2, w2), (R_R1, r1), (R_R2, r2)):
        rec = jnp.where(lane == float(ln), val, rec)
    o_ref[...] = rec


def moe_router(x, w, b):
    m = x.shape[0]
    wh = w.astype(BF16)
    wl = (w - wh.astype(F32)).astype(BF16)
    return pl.pallas_call(
        _router_kernel,
        grid=(m // TM_LN,),
        in_specs=[pl.BlockSpec((TM_LN, D), lambda i: (i, 0)),
                  pl.BlockSpec((D, LANES), lambda i: (0, 0)),
                  pl.BlockSpec((D, LANES), lambda i: (0, 0)),
                  pl.BlockSpec((1, LANES), lambda i: (0, 0))],
        out_specs=[pl.BlockSpec((TM_LN, LANES), lambda i: (i, 0)),
                   pl.BlockSpec((1, LANES), lambda i: (0, 0))],
        out_shape=[jax.ShapeDtypeStruct((m, LANES), F32), jax.ShapeDtypeStruct((1, LANES), F32)],
        scratch_shapes=[pltpu.VMEM((1, LANES), F32)],
        compiler_params=_cp("arbitrary"),
        name="moe_router",
    )(x, wh, wl, b)


def _slots_kernel(rec_ref, starts_ref, o_ref):
    rec = rec_ref[...]
    lane = lax.broadcasted_iota(jnp.int32, rec.shape, 1).astype(F32)
    starts = starts_ref[...]
    out = jnp.zeros(rec.shape, F32)
    for k, (le, lr) in enumerate(((R_E1, R_R1), (R_E2, R_R2))):
        e = rec[:, le:le + 1]
        s = jnp.sum(jnp.where(lane == e, starts, 0.0), -1, keepdims=True) + rec[:, lr:lr + 1]
        out = jnp.where(lane == float(k), s, out)
    o_ref[...] = out


def moe_slots(rec, starts_row):
    m = rec.shape[0]
    return pl.pallas_call(
        _slots_kernel,
        grid=(m // TM_MM,),
        in_specs=[pl.BlockSpec((TM_MM, LANES), lambda i: (i, 0)), pl.BlockSpec((1, LANES), lambda i: (0, 0))],
        out_specs=pl.BlockSpec((TM_MM, LANES), lambda i: (i, 0)),
        out_shape=jax.ShapeDtypeStruct((m, LANES), F32),
        compiler_params=_cp("parallel"),
        name="moe_slots",
    )(rec, starts_row)


def _row_gather_start(idx_ref, base, nchunks, src_hbm, dst, sem, split_priorities=False):
    def body(c, carry):
        off = pl.multiple_of(c * GATHER_CH, GATHER_CH)
        dst_c = dst.at[pl.ds(off, GATHER_CH)]
        for u in range(GATHER_CH):
            pltpu.make_async_copy(src_hbm.at[pl.ds(idx_ref[base + off + u], 1)],
                                  dst_c.at[pl.ds(u, 1)], sem).start(
                                      priority=u % 2 if split_priorities else 0)
        return carry
    lax.fori_loop(0, nchunks, body, 0)


def _row_gather_wait(nchunks, src_hbm, dst, sem):
    def body(c, carry):
        pltpu.make_async_copy(src_hbm.at[pl.ds(0, GATHER_CH)], dst.at[pl.ds(0, GATHER_CH)], sem).wait()
        return carry
    lax.fori_loop(0, nchunks, body, 0)


def _ffn_kernel(te_ref, nt_ref, nch_ref, pos_ref, x_hbm, w1_ref, w3_ref, w2_ref, y_ref, xbuf, sem,
                w1b, w3b, w2b, src_ref):
    i = pl.program_id(0)
    nt = nt_ref[0]
    slot = i % 2

    @pl.when(i == 0)
    def _():
        xbuf[...] = jnp.zeros(xbuf.shape, F32)
        def clear(k, carry):
            for u in range(SMEM_UNROLL):
                src_ref[k * SMEM_UNROLL + u] = 0
            return carry
        lax.fori_loop(0, P_SLOTS // SMEM_UNROLL, clear, 0)

        def scatter(k, carry):
            t, rem = k // (TM_LN // SMEM_UNROLL), k % (TM_LN // SMEM_UNROLL)
            for half in range(2):
                for u in range(SMEM_UNROLL):
                    r = rem * SMEM_UNROLL + u
                    src_ref[pos_ref[t * 2 * TM_LN + half * TM_LN + r]] = t * TM_LN + r
            return carry
        lax.fori_loop(0, MT // SMEM_UNROLL, scatter, 0)
        _row_gather_start(src_ref, 0, nch_ref[0], x_hbm, xbuf.at[0], sem.at[0], split_priorities=True)

    @pl.when(i + 1 < nt)
    def _():
        _row_gather_start(src_ref, (i + 1) * TM_E, nch_ref[i + 1], x_hbm,
                          xbuf.at[1 - slot], sem.at[1 - slot], split_priorities=True)

    @pl.when(i < nt)
    def _():
        prev = te_ref[jnp.maximum(i - 1, 0)]

        @pl.when((i == 0) | (te_ref[i] != prev))
        def _():
            w1b[...] = w1_ref[0, 0].astype(BF16)
            w3b[...] = w3_ref[0, 0].astype(BF16)
            w2b[...] = w2_ref[0, 0].astype(BF16)

        _row_gather_wait(nch_ref[i], x_hbm, xbuf.at[slot], sem.at[slot])
        x = xbuf[slot].astype(BF16)
        h1 = jnp.dot(x, w1b[...], preferred_element_type=F32)
        h3 = jnp.dot(x, w3b[...], preferred_element_type=F32)
        h = _silu(h1) * h3
        y_ref[...] = jnp.dot(h.astype(BF16), w2b[...], preferred_element_type=F32)

    @pl.when(i >= nt)
    def _():
        y_ref[...] = jnp.zeros(y_ref.shape, y_ref.dtype)


def moe_ffn(layer, tile_expert, n_tiles, n_chunks, pos_tiles, x, w1, w3, w2):
    wmap = lambda i, te, nt, nch, pos: (layer, te[i], 0, 0)
    grid_spec = pltpu.PrefetchScalarGridSpec(
        num_scalar_prefetch=4,
        grid=(N_ETILES,),
        in_specs=[
            pl.BlockSpec(memory_space=pl.ANY),
            pl.BlockSpec((1, 1, D, MOE_FF), wmap),
            pl.BlockSpec((1, 1, D, MOE_FF), wmap),
            pl.BlockSpec((1, 1, MOE_FF, D), wmap),
        ],
        out_specs=pl.BlockSpec((TM_E, D), lambda i, te, nt, nch, pos: (i, 0)),
        scratch_shapes=[pltpu.VMEM((2, TM_E, D), F32), pltpu.SemaphoreType.DMA((2,)),
                        pltpu.VMEM((D, MOE_FF), BF16), pltpu.VMEM((D, MOE_FF), BF16),
                        pltpu.VMEM((MOE_FF, D), BF16), pltpu.SMEM((P_SLOTS,), jnp.int32)],
    )
    return pl.pallas_call(
        _ffn_kernel,
        grid_spec=grid_spec,
        out_shape=jax.ShapeDtypeStruct((P_SLOTS, D), F32),
        compiler_params=_cp("arbitrary"),
        name="moe_ffn",
    )(tile_expert, n_tiles, n_chunks, pos_tiles, x, w1, w3, w2)


def _combine_ln_kernel(pos_ref, res_ref, rec_ref, g_ref, beta_ref, ys_hbm, of_ref, ob_ref, ybuf, sem):
    i = pl.program_id(0)
    slot = i % 2
    n = 2 * TM_LN
    nch = n // GATHER_CH

    @pl.when(i == 0)
    def _():
        _row_gather_start(pos_ref, 0, nch, ys_hbm, ybuf.at[0], sem.at[0], split_priorities=True)

    @pl.when(i + 1 < pl.num_programs(0))
    def _():
        _row_gather_start(pos_ref, (i + 1) * n, nch, ys_hbm, ybuf.at[1 - slot], sem.at[1 - slot],
                          split_priorities=True)

    _row_gather_wait(nch, ys_hbm, ybuf.at[slot], sem.at[slot])
    rec = rec_ref[...]
    y = (rec[:, R_W1:R_W1 + 1] * ybuf[slot, 0:TM_LN, :]
         + rec[:, R_W2:R_W2 + 1] * ybuf[slot, TM_LN:2 * TM_LN, :])
    out = _ln_rows(ALPHA * res_ref[...] + y, g_ref[...], beta_ref[...])
    of_ref[...] = out
    ob_ref[...] = out.astype(BF16)


def combine_ln(pos_tiles, res, rec, g, beta, ys):
    m = res.shape[0]
    row = lambda i, pos: (i, 0)
    fix = lambda i, pos: (0, 0)
    grid_spec = pltpu.PrefetchScalarGridSpec(
        num_scalar_prefetch=1,
        grid=(m // TM_LN,),
        in_specs=[pl.BlockSpec((TM_LN, D), row), pl.BlockSpec((TM_LN, LANES), row),
                  pl.BlockSpec((1, D), fix), pl.BlockSpec((1, D), fix),
                  pl.BlockSpec(memory_space=pl.ANY)],
        out_specs=[pl.BlockSpec((TM_LN, D), row), pl.BlockSpec((TM_LN, D), row)],
        scratch_shapes=[pltpu.VMEM((2, 2 * TM_LN, D), F32), pltpu.SemaphoreType.DMA((2,))],
    )
    return pl.pallas_call(
        _combine_ln_kernel,
        grid_spec=grid_spec,
        out_shape=[jax.ShapeDtypeStruct((m, D), F32), jax.ShapeDtypeStruct((m, D), BF16)],
        compiler_params=_cp("arbitrary"),
        name="combine_ln",
    )(pos_tiles, res, rec, g, beta, ys)


def _dispatch_plan(rec, counts_row):
    counts = counts_row[0, :MOE_E].astype(jnp.int32)
    padded = ((counts + TM_E - 1) // TM_E) * TM_E
    ends = jnp.cumsum(padded)
    starts = ends - padded
    starts_row = jnp.pad(starts.astype(F32), (0, LANES - MOE_E))[None, :]
    pos = moe_slots(rec, starts_row)[:, 0:2].astype(jnp.int32)
    n_tiles = (ends[-1] // TM_E).astype(jnp.int32)
    tile_start = jnp.arange(N_ETILES, dtype=jnp.int32) * TM_E
    te = jnp.sum((tile_start[:, None] >= ends[None, :]).astype(jnp.int32), axis=1)
    last = jnp.sum(jnp.where(jnp.arange(N_ETILES) == n_tiles - 1, te, 0))
    live = jnp.arange(N_ETILES) < n_tiles
    te = jnp.minimum(jnp.where(live, te, last), MOE_E - 1).astype(jnp.int32)
    ids = jnp.arange(MOE_E, dtype=jnp.int32)[None, :]
    seg_end = jnp.sum(jnp.where(te[:, None] == ids, (starts + counts)[None, :], 0), axis=1)
    n_rows = jnp.clip(seg_end - tile_start, 0, TM_E)
    n_chunks = jnp.where(live, (n_rows + GATHER_CH - 1) // GATHER_CH, 0).astype(jnp.int32)
    pos_tiles = pos.reshape(MT // TM_LN, TM_LN, 2).transpose(0, 2, 1).reshape(-1)
    return te, n_tiles.reshape(1), n_chunks, pos_tiles


def hier_moe_block(xf, i, moe_w_group, moe_b_group, moe_w_egate, moe_b_egate,
                   moe_w1, moe_w3, moe_w2, ln_g, ln_b):
    wr = jnp.concatenate([moe_w_group[i], moe_w_egate[i].transpose(1, 0, 2).reshape(D, MOE_E)], axis=1)
    wr = jnp.pad(wr, ((0, 0), (0, LANES - wr.shape[1])))
    br = jnp.concatenate([moe_b_group[i], moe_b_egate[i].reshape(MOE_E)])
    br = jnp.pad(br, (0, LANES - br.shape[0]))[None, :]
    rec, counts = moe_router(xf, wr, br)
    te, n_tiles, n_chunks, pos_tiles = _dispatch_plan(rec, counts)
    ys = moe_ffn(i, te, n_tiles, n_chunks, pos_tiles, xf, moe_w1, moe_w3, moe_w2)
    return combine_ln(pos_tiles, xf, rec, ln_g[i][None, :], ln_b[i][None, :], ys)


def _sample_rows(a):
    return a[NP:].reshape(DEC_BATCH, TS, a.shape[-1])


def _prompt_tail(a, n):
    return jnp.stack([a[(b + 1) * SEQ - n:(b + 1) * SEQ] for b in range(BATCH)])


def conv_layer(xf, xb, state, w_in, b_in, w_dw, b_dw, g, b, w_out, b_out, ln_g, ln_b):
    u = mm_glu(xb, w_in, b_in[None, :], F32)
    tt_p = 256
    zbuf = jnp.zeros((BATCH, HALO, D), F32)
    z_p = conv_ln_silu(u, 0, BATCH, SEQ, tt_p, zbuf, w_dw, b_dw[None, :], g[None, :], b[None, :], BF16)
    pad = HALO - (CONV_WIDTH - 1)
    sbuf = jnp.pad(state, ((0, 0), (pad, 0), (0, 0)))
    z_s = conv_ln_silu(u, NP // TS, DEC_BATCH, TS, TS, sbuf, w_dw, b_dw[None, :], g[None, :], b[None, :], F32)
    z = jnp.concatenate([z_p, z_s.astype(BF16)], axis=0)
    xf, xb = mm_res_ln(z, w_out.astype(BF16), b_out[None, :], xf, ln_g[None, :], ln_b[None, :])
    new_p = _prompt_tail(u, CONV_WIDTH - 1)
    new_s = jnp.concatenate([state[:, DEC_SEQ:], _sample_rows(u)[:, :DEC_SEQ]], axis=1)
    return xf, xb, new_p, new_s


def hgrn_layer(xf, xb, state, w_in, lb, norm_g, w_out, ln_g, ln_b):
    hk = HG_H * HG_DK
    lb_row = jnp.concatenate([jnp.zeros((hk,), F32), lb, jnp.zeros((2 * hk,), F32)])[None, :]
    p = mm_hgrn(xb, w_in, lb_row)
    s0_p = jnp.zeros((BATCH, HG_H, HG_DK, HG_DV), F32)
    o_p, sf_p = hgrn_scan(p, 0, BATCH, SEQ, 128, 16, 16, s0_p, norm_g[None, :], BF16)
    o_s, sf_s = hgrn_scan(p, NP // TS, DEC_BATCH, TS, TS, TS, DEC_SEQ, state, norm_g[None, :], F32)
    o = jnp.concatenate([o_p, o_s.astype(BF16)], axis=0)
    zero_b = jnp.zeros((1, D), F32)
    xf, xb = mm_res_ln(o, w_out.astype(BF16), zero_b, xf, ln_g[None, :], ln_b[None, :])
    return xf, xb, sf_p, sf_s


def s5_layer(xf, xb, h_re, h_im, lam_re, lam_im, log_dt, b_re, b_im, c_re, c_im, d_skip, w_glu,
             ln_g, ln_b):
    nj, sw, lc = S5_NJ, S5_SW, S5_LC
    ops = _s5_operators(lam_re, lam_im, log_dt, b_re, b_im, c_re, c_im, d_skip)
    nc = SEQ // lc
    up = xb[:NP].reshape(BATCH, nc, lc, nj, LANES).transpose(3, 1, 0, 2, 4).reshape(nj, nc * BATCH, lc * LANES)
    y_p, hf_p = s5_lti(up, jnp.zeros((nj, 2 * BATCH, 2 * sw), F32), ops, nc, lc)
    y_p = y_p.reshape(nj, nc, BATCH, lc, LANES).transpose(2, 1, 3, 0, 4).reshape(NP, D)
    hf_p = hf_p[:, :BATCH].transpose(1, 0, 2)
    us = _sample_rows(xb).reshape(DEC_BATCH, TS, nj, LANES).transpose(2, 0, 1, 3).reshape(nj, DEC_BATCH, TS * LANES)
    h0 = jnp.concatenate([h_re.reshape(DEC_BATCH, nj, sw), h_im.reshape(DEC_BATCH, nj, sw)], -1).transpose(1, 0, 2)
    y_s, hf_s = s5_lti(us, h0, ops, 1, DEC_SEQ)
    y_s = y_s.reshape(nj, DEC_BATCH, TS, LANES).transpose(1, 2, 0, 3).reshape(NS, D)
    hf_s = hf_s.transpose(1, 0, 2)
    y = jnp.concatenate([y_p, y_s], axis=0)
    m = mm_glu(y, w_glu, jnp.zeros((1, 2 * D), F32), BF16)
    xf, xb = res_ln(xf, m, ln_g[None, :], ln_b[None, :])
    st = lambda h, n: (h[..., :sw].reshape(n, SSM_GROUPS, SSM_STATE), h[..., sw:].reshape(n, SSM_GROUPS, SSM_STATE))
    return (xf, xb) + st(hf_p, BATCH) + st(hf_s, DEC_BATCH)


def _rope_tables(pos):
    half = HD // 2
    inv = ROPE_THETA ** (-jnp.arange(half, dtype=F32) / half)
    ang = pos.astype(F32)[:, None] * inv[None, :]
    cos, sin = jnp.cos(ang), jnp.sin(ang)
    cos_t = jnp.tile(jnp.concatenate([cos, cos], -1), (1, LANES // HD))
    sin_t = jnp.tile(jnp.concatenate([-sin, sin], -1), (1, LANES // HD))
    return cos_t, sin_t


def swa_layer(xf, xb, cache_k, cache_v, w_qkv, sinks, w_out, ln_g, ln_b):
    hq = NH * HD
    kvw = NKV * HD
    pos = jnp.concatenate([jnp.tile(jnp.arange(SEQ), BATCH),
                           jnp.tile(PAST_LEN + jnp.arange(TS), DEC_BATCH)])
    cos_t, sin_t = _rope_tables(pos)
    q = mm_rope(xb, w_qkv, 0, hq, cos_t, sin_t, TN_MM // LANES, HD ** -0.5)
    kv = mm_rope(xb, w_qkv, hq, 2 * kvw, cos_t, sin_t, kvw // LANES, 1.0)
    nblk = SEQ // WINDOW
    qw = q.reshape(MT // WINDOW, WINDOW, D)
    kvwin = kv.reshape(MT // WINDOW, WINDOW, 2 * kvw)
    prev = lambda i: (jnp.maximum(i - 1, 0), 0, 0)
    prev_v = lambda i: (jnp.maximum(i - 1, 0), 0, 1)
    o_p = swa_attention(qw, kvwin, kvwin, kvwin, BATCH * nblk, 0, prev, prev_v, sinks, nblk, False).reshape(NP, D)
    kvs = _sample_rows(kv)
    kvs_pad = jnp.pad(kvs, ((0, 0), (0, WINDOW - TS), (0, 0)))
    ck = cache_k.reshape(DEC_BATCH, WINDOW, kvw)
    cv = cache_v.reshape(DEC_BATCH, WINDOW, kvw)
    same = lambda i: (i, 0, 0)
    o_s = swa_attention(q.reshape(MT // TS, TS, D), kvs_pad, ck, cv, DEC_BATCH, NP // TS, same, same,
                        sinks, 1, True).reshape(NS, D)
    o = jnp.concatenate([o_p, o_s], axis=0)
    xf, xb = mm_res_ln(o, w_out.astype(BF16), jnp.zeros((1, D), F32), xf, ln_g[None, :], ln_b[None, :])
    kvp = _prompt_tail(kv, WINDOW)
    nk_p = kvp[..., :kvw].reshape(BATCH, WINDOW, NKV, HD)
    nv_p = kvp[..., kvw:].reshape(BATCH, WINDOW, NKV, HD)
    nk_s = jnp.concatenate([ck[:, DEC_SEQ:], kvs[:, :DEC_SEQ, :kvw]], axis=1).reshape(DEC_BATCH, WINDOW, NKV, HD)
    nv_s = jnp.concatenate([cv[:, DEC_SEQ:], kvs[:, :DEC_SEQ, kvw:]], axis=1).reshape(DEC_BATCH, WINDOW, NKV, HD)
    return xf, xb, nk_p, nv_p, nk_s, nv_s


def kernel(x_prompt, x_sample, state_conv, state_hgrn, state_ssm_re, state_ssm_im, cache_swa_k, cache_swa_v, conv_w_in, conv_b_in, conv_w_dw, conv_b_dw, conv_ln_g, conv_ln_b, conv_w_out, conv_b_out, hgrn_w_in, hgrn_lb_logits, hgrn_norm_g, hgrn_w_out, ssm_lam_re, ssm_lam_im, ssm_log_dt, ssm_b_re, ssm_b_im, ssm_c_re, ssm_c_im, ssm_d, ssm_w_glu, swa_w_qkv, swa_sinks, swa_w_out, ln1_g, ln1_b, ln2_g, ln2_b, moe_w_group, moe_b_group, moe_w_egate, moe_b_egate, moe_w1, moe_w3, moe_w2):
    lbs = jnp.cumsum(jax.nn.softmax(hgrn_lb_logits.astype(F32), axis=0), axis=0)
    lbs = lbs - lbs[0]
    xs_pad = jnp.pad(x_sample, ((0, 0), (0, TS - DEC_SEQ), (0, 0)))
    xf = jnp.concatenate([x_prompt.reshape(NP, D), xs_pad.reshape(NS, D)], axis=0)
    xb = xf.astype(BF16)
    moe = (moe_w_group, moe_b_group, moe_w_egate, moe_b_egate, moe_w1, moe_w3, moe_w2, ln2_g, ln2_b)

    xf, xb, conv_p, conv_s = conv_layer(
        xf, xb, state_conv[0], conv_w_in[0], conv_b_in[0], conv_w_dw[0], conv_b_dw[0],
        conv_ln_g[0], conv_ln_b[0], conv_w_out[0], conv_b_out[0], ln1_g[0], ln1_b[0])
    xf, xb = hier_moe_block(xf, 0, *moe)

    xf, xb, hg_p, hg_s = hgrn_layer(xf, xb, state_hgrn[0], hgrn_w_in[0], lbs[1], hgrn_norm_g[0],
                                    hgrn_w_out[0], ln1_g[1], ln1_b[1])
    xf, xb = hier_moe_block(xf, 1, *moe)

    xf, xb, hr_p, hi_p, hr_s, hi_s = s5_layer(
        xf, xb, state_ssm_re[0], state_ssm_im[0], ssm_lam_re[0], ssm_lam_im[0], ssm_log_dt[0],
        ssm_b_re[0], ssm_b_im[0], ssm_c_re[0], ssm_c_im[0], ssm_d[0], ssm_w_glu[0], ln1_g[2], ln1_b[2])
    xf, xb = hier_moe_block(xf, 2, *moe)

    xf, xb, k_p, v_p, k_s, v_s = swa_layer(xf, xb, cache_swa_k[0], cache_swa_v[0], swa_w_qkv[0],
                                            swa_sinks[0], swa_w_out[0], ln1_g[3], ln1_b[3])
    xf, xb = hier_moe_block(xf, 3, *moe)

    y_p = xf[:NP].reshape(BATCH, SEQ, D)
    y_s = _sample_rows(xf)[:, :DEC_SEQ]
    return (y_p, y_s, conv_p[None], conv_s[None], hg_p[None], hg_s[None],
            hr_p[None], hi_p[None], hr_s[None], hi_s[None],
            k_p[None], v_p[None], k_s[None], v_s[None])
```
